```python
import jax, jax.numpy as jnp
from jax import lax
import numpy as np

D_MODEL = 1024
BATCH = 2
SEQ = 16384
DEPTH = 2

EPS = 1e-6
CONV_DIM = 512
CONV_WIDTH = 31
NSA_HEADS = 8
NSA_KV_HEADS = 2
HEAD_DIM = 64
NSA_GROUP = NSA_HEADS // NSA_KV_HEADS
NSA_Q_DIM = NSA_HEADS * HEAD_DIM
NSA_KV_DIM = NSA_KV_HEADS * HEAD_DIM
CMP_LEN = 32
CMP_STRIDE = 16
CMP_HIDDEN = 128
SEL_BLOCK = 64
SEL_TOPK = 16
WINDOW = 512
Q_BLOCK = 128
FORCE_SCORE = 1e4
LRU_DIM = 512
LRU_HEADS = 8
LRU_BLOCK = LRU_DIM // LRU_HEADS
LRU_CONV_WIDTH = 4
LRU_C = 8.0
FFN_DIM = ((-(-8 * D_MODEL // 3)) + 255) // 256 * 256
IN_SIZES = [2 * CONV_DIM, NSA_Q_DIM, 6 * NSA_KV_DIM, 3 * NSA_HEADS, LRU_DIM, LRU_DIM]
IN_DIM = sum(IN_SIZES)
IN_SPLITS = [int(v) for v in np.cumsum(IN_SIZES)[:-1]]
N_BRANCH = 3

kernel_name = "hybrid_conv_nsa_rglru_adaln"


def rms_norm(x, g):
    xf = x.astype(jnp.float32)
    y = xf * lax.rsqrt(jnp.mean(xf * xf, axis=-1, keepdims=True) + EPS)
    return (y * g.astype(jnp.float32)).astype(x.dtype)


def layer_norm(x, g, b):
    xf = x.astype(jnp.float32)
    mu = jnp.mean(xf, axis=-1, keepdims=True)
    var = jnp.mean(jnp.square(xf - mu), axis=-1, keepdims=True)
    y = (xf - mu) * lax.rsqrt(var + EPS)
    return (y * g.astype(jnp.float32) + b.astype(jnp.float32)).astype(x.dtype)


def causal_depthwise_conv(x, w, b):
    k = w.shape[0]
    y = lax.conv_general_dilated(x, w[:, None, :].astype(x.dtype), window_strides=(1,),
                                 padding=((k - 1, 0),), dimension_numbers=('NWC', 'WIO', 'NWC'),
                                 feature_group_count=x.shape[-1])
    return y + b


def alibi_slopes(n):
    return jnp.asarray(np.array([2.0 ** (-8.0 * (h + 1) / n) for h in range(n)], dtype=np.float32))


def masked_softmax(s, mask):
    s = jnp.where(mask, s.astype(jnp.float32), -jnp.inf)
    m = jnp.max(s, axis=-1, keepdims=True)
    m = jnp.where(jnp.isfinite(m), m, 0.0)
    p = jnp.exp(s - m)
    den = jnp.sum(p, axis=-1, keepdims=True)
    return p / jnp.where(den > 0, den, 1.0)


def gather_blocks(blocks, ix):
    return jax.vmap(jax.vmap(lambda b_, i_: b_[i_]))(blocks, ix)


def conformer_conv(u, w_dw, b_dw, ln_g, ln_b, w_o):
    a, g = jnp.split(u, 2, axis=-1)
    v = a * jax.nn.sigmoid(g)
    v = causal_depthwise_conv(v, w_dw, b_dw)
    v = jax.nn.silu(layer_norm(v, ln_g, ln_b))
    return v @ w_o


def rglru_block(ux, ug, w_conv, b_conv, w_a, b_a, w_x, b_x, lam, w_o):
    bsz, s = ux.shape[0], ux.shape[1]
    xr = causal_depthwise_conv(ux, w_conv, b_conv)
    xh = xr.reshape(bsz, s, LRU_HEADS, LRU_BLOCK)
    r = jax.nn.sigmoid(jnp.einsum('bshi,hij->bshj', xh, w_a).reshape(bsz, s, LRU_DIM) + b_a)
    i = jax.nn.sigmoid(jnp.einsum('bshi,hij->bshj', xh, w_x).reshape(bsz, s, LRU_DIM) + b_x)
    log_a = -LRU_C * r.astype(jnp.float32) * jax.nn.softplus(-lam.astype(jnp.float32))
    a = jnp.exp(log_a)
    bterm = jnp.sqrt(-jnp.expm1(2.0 * log_a)) * (i * xr).astype(jnp.float32)

    def combine(left, right):
        a_l, b_l = left
        a_r, b_r = right
        return a_r * a_l, a_r * b_l + b_r

    _, h = lax.associative_scan(combine, (a, bterm), axis=1)
    y = h.astype(ux.dtype) * jax.nn.gelu(ug)
    return y @ w_o


def compress_blocks(k, cidx, pe, w1, w2):
    bsz, n_cmp = k.shape[0], cidx.shape[0]
    blk = k[:, cidx] + pe[:, None, :]
    blk = blk.transpose(0, 1, 3, 2, 4).reshape(bsz, n_cmp, NSA_KV_HEADS, CMP_LEN * HEAD_DIM)
    return jax.nn.gelu(blk @ w1) @ w2


def nsa_mixer(q, k_c, v_c, k_s, v_s, k_w, v_w, gate_logits, pe_k, w_ck1, w_ck2, pe_v, w_cv1, w_cv2, w_o):
    bsz, s = q.shape[0], q.shape[1]
    n_cmp = (s - CMP_LEN) // CMP_STRIDE + 1
    n_sel = s // SEL_BLOCK
    k_top = min(SEL_TOPK, n_sel)
    n_qb = s // Q_BLOCK
    scale = HEAD_DIM ** -0.5

    cidx = jnp.arange(n_cmp)[:, None] * CMP_STRIDE + jnp.arange(CMP_LEN)[None, :]
    kc = compress_blocks(k_c, cidx, pe_k, w_ck1, w_ck2)
    vc = compress_blocks(v_c, cidx, pe_v, w_cv1, w_cv2)
    cmp_start = jnp.arange(n_cmp) * CMP_STRIDE
    cmp_end = cmp_start + CMP_LEN - 1
    sel_start = jnp.arange(n_sel) * SEL_BLOCK
    ov = jnp.clip(jnp.minimum(cmp_start[:, None] + CMP_LEN, sel_start[None, :] + SEL_BLOCK)
                  - jnp.maximum(cmp_start[:, None], sel_start[None, :]), 0).astype(jnp.float32) / CMP_LEN

    kb = k_s.reshape(bsz, n_sel, SEL_BLOCK, NSA_KV_HEADS, HEAD_DIM).transpose(0, 3, 1, 2, 4)
    vb = v_s.reshape(bsz, n_sel, SEL_BLOCK, NSA_KV_HEADS, HEAD_DIM).transpose(0, 3, 1, 2, 4)
    kwp = jnp.pad(k_w, ((0, 0), (WINDOW, 0), (0, 0), (0, 0)))
    vwp = jnp.pad(v_w, ((0, 0), (WINDOW, 0), (0, 0), (0, 0)))
    slopes = alibi_slopes(NSA_HEADS).reshape(NSA_KV_HEADS, NSA_GROUP)
    gates = jax.nn.sigmoid(gate_logits.astype(jnp.float32)).reshape(bsz, s, NSA_KV_HEADS, NSA_GROUP, 3)
    jsel = jnp.arange(n_sel)
    sub = jnp.arange(SEL_BLOCK)

    def attend_block(qb):
        q0 = qb * Q_BLOCK
        qpos = q0 + jnp.arange(Q_BLOCK)
        qblk = lax.dynamic_slice_in_dim(q, q0, Q_BLOCK, axis=1) * scale
        gblk = lax.dynamic_slice_in_dim(gates, q0, Q_BLOCK, axis=1)
        dist_c = qpos[:, None] - cmp_end[None, :]
        s_c = jnp.einsum('bqgrd,bcgd->bgrqc', qblk, kc) - slopes[:, :, None, None] * dist_c
        p_c = masked_softmax(s_c, dist_c >= 0)
        o_c = jnp.einsum('bgrqc,bcgd->bqgrd', p_c.astype(vc.dtype), vc)
        imp = jnp.einsum('bgrqc,cj->bgqj', p_c, ov)
        qblk_id = qpos // SEL_BLOCK
        forced = (jsel[None, :] == 0) | (jsel[None, :] == qblk_id[:, None]) | (jsel[None, :] == qblk_id[:, None] - 1)
        valid = sel_start[None, :] <= qpos[:, None]
        imp = jnp.where(forced, FORCE_SCORE, jnp.where(valid, imp, -1.0))
        _, top = lax.top_k(imp, k_top)
        ks = gather_blocks(kb, top).reshape(bsz, NSA_KV_HEADS, Q_BLOCK, k_top * SEL_BLOCK, HEAD_DIM)
        vs = gather_blocks(vb, top).reshape(bsz, NSA_KV_HEADS, Q_BLOCK, k_top * SEL_BLOCK, HEAD_DIM)
        spos = (top[..., None] * SEL_BLOCK + sub).reshape(bsz, NSA_KV_HEADS, Q_BLOCK, k_top * SEL_BLOCK)
        dist_s = qpos[None, None, :, None] - spos
        s_s = jnp.einsum('bqgrd,bgqkd->bgrqk', qblk, ks) - slopes[None, :, :, None, None] * dist_s[:, :, None]
        p_s = masked_softmax(s_s, (dist_s >= 0)[:, :, None])
        o_s = jnp.einsum('bgrqk,bgqkd->bqgrd', p_s.astype(vs.dtype), vs)
        kw = lax.dynamic_slice_in_dim(kwp, q0, Q_BLOCK + WINDOW, axis=1)
        vw = lax.dynamic_slice_in_dim(vwp, q0, Q_BLOCK + WINDOW, axis=1)
        kpos = q0 - WINDOW + jnp.arange(Q_BLOCK + WINDOW)
        dist_w = qpos[:, None] - kpos[None, :]
        mask_w = (dist_w >= 0) & (dist_w < WINDOW) & (kpos[None, :] >= 0)
        s_w = jnp.einsum('bqgrd,bkgd->bgrqk', qblk, kw) - slopes[:, :, None, None] * dist_w
        p_w = masked_softmax(s_w, mask_w)
        o_w = jnp.einsum('bgrqk,bkgd->bqgrd', p_w.astype(vw.dtype), vw)
        o = gblk[..., 0:1] * o_c + gblk[..., 1:2] * o_s + gblk[..., 2:3] * o_w
        return o.astype(q.dtype)

    o = lax.map(attend_block, jnp.arange(n_qb))
    o = jnp.moveaxis(o, 0, 1).reshape(bsz, s, NSA_Q_DIM)
    return o @ w_o


def setup_inputs(seed: int = 0) -> dict:
    key = jax.random.key(seed)
    ks = jax.random.split(key, 33)
    f32 = jnp.float32

    def nrm(k, shape, scale):
        return jax.random.normal(k, shape, f32) * scale

    u = jax.random.uniform(ks[22], (DEPTH, LRU_DIM), f32, minval=0.9, maxval=0.999)
    sa = u ** (1.0 / LRU_C)
    lam = jnp.log(sa) - jnp.log1p(-sa)
    return {
        "x": nrm(ks[0], (BATCH, SEQ, D_MODEL), 1.0),
        "c": nrm(ks[1], (BATCH, D_MODEL), 1.0),
        "w_mod": nrm(ks[2], (DEPTH, D_MODEL, 6 * D_MODEL), 0.5 * D_MODEL ** -0.5),
        "b_mod": nrm(ks[3], (DEPTH, 6 * D_MODEL), 0.01),
        "norm_mix": 1.0 + nrm(ks[4], (DEPTH, D_MODEL), 0.05),
        "norm_ffn": 1.0 + nrm(ks[5], (DEPTH, D_MODEL), 0.05),
        "w_in": nrm(ks[6], (DEPTH, D_MODEL, IN_DIM), D_MODEL ** -0.5),
        "conv_w_dw": nrm(ks[7], (DEPTH, CONV_WIDTH, CONV_DIM), CONV_WIDTH ** -0.5),
        "conv_b_dw": nrm(ks[8], (DEPTH, CONV_DIM), 0.01),
        "conv_ln_g": 1.0 + nrm(ks[9], (DEPTH, CONV_DIM), 0.05),
        "conv_ln_b": nrm(ks[10], (DEPTH, CONV_DIM), 0.01),
        "conv_w_out": nrm(ks[11], (DEPTH, CONV_DIM, D_MODEL), CONV_DIM ** -0.5),
        "nsa_pe_k": nrm(ks[12], (DEPTH, CMP_LEN, HEAD_DIM), 0.02),
        "nsa_w_ck1": nrm(ks[13], (DEPTH, CMP_LEN * HEAD_DIM, CMP_HIDDEN), (CMP_LEN * HEAD_DIM) ** -0.5),
        "nsa_w_ck2": nrm(ks[14], (DEPTH, CMP_HIDDEN, HEAD_DIM), CMP_HIDDEN ** -0.5),
        "nsa_pe_v": nrm(ks[15], (DEPTH, CMP_LEN, HEAD_DIM), 0.02),
        "nsa_w_cv1": nrm(ks[16], (DEPTH, CMP_LEN * HEAD_DIM, CMP_HIDDEN), (CMP_LEN * HEAD_DIM) ** -0.5),
        "nsa_w_cv2": nrm(ks[17], (DEPTH, CMP_HIDDEN, HEAD_DIM), CMP_HIDDEN ** -0.5),
        "nsa_w_out": nrm(ks[18], (DEPTH, NSA_Q_DIM, D_MODEL), NSA_Q_DIM ** -0.5),
        "lru_w_conv": nrm(ks[19], (DEPTH, LRU_CONV_WIDTH, LRU_DIM), LRU_CONV_WIDTH ** -0.5),
        "lru_b_conv": nrm(ks[20], (DEPTH, LRU_DIM), 0.01),
        "lru_w_a": nrm(ks[21], (DEPTH, LRU_HEADS, LRU_BLOCK, LRU_BLOCK), LRU_BLOCK ** -0.5),
        "lru_b_a": nrm(ks[23], (DEPTH, LRU_DIM), 0.01),
        "lru_w_x": nrm(ks[24], (DEPTH, LRU_HEADS, LRU_BLOCK, LRU_BLOCK), LRU_BLOCK ** -0.5),
        "lru_b_x": nrm(ks[25], (DEPTH, LRU_DIM), 0.01),
        "lru_lam": lam,
        "lru_w_out": nrm(ks[26], (DEPTH, LRU_DIM, D_MODEL), LRU_DIM ** -0.5),
        "w_merge": nrm(ks[27], (DEPTH, D_MODEL, N_BRANCH * D_MODEL), D_MODEL ** -0.5),
        "b_merge": nrm(ks[28], (DEPTH, N_BRANCH * D_MODEL), 0.01),
        "w_out": nrm(ks[29], (DEPTH, D_MODEL, D_MODEL), D_MODEL ** -0.5),
        "w_ffn_in": nrm(ks[30], (DEPTH, D_MODEL, 2 * FFN_DIM), D_MODEL ** -0.5),
        "w_ffn_out": nrm(ks[31], (DEPTH, FFN_DIM, D_MODEL), FFN_DIM ** -0.5),
        "final_norm": 1.0 + nrm(ks[32], (D_MODEL,), 0.05),
    }


def reference(x, c, w_mod, b_mod, norm_mix, norm_ffn, w_in, conv_w_dw, conv_b_dw, conv_ln_g, conv_ln_b,
              conv_w_out, nsa_pe_k, nsa_w_ck1, nsa_w_ck2, nsa_pe_v, nsa_w_cv1, nsa_w_cv2, nsa_w_out,
              lru_w_conv, lru_b_conv, lru_w_a, lru_b_a, lru_w_x, lru_b_x, lru_lam, lru_w_out,
              w_merge, b_merge, w_out, w_ffn_in, w_ffn_out, final_norm):
    bsz, s = x.shape[0], x.shape[1]
    for l in range(DEPTH):
        mod = (c @ w_mod[l] + b_mod[l])[:, None, :]
        sh1, sc1, g1, sh2, sc2, g2 = jnp.split(mod, 6, axis=-1)
        h = rms_norm(x, norm_mix[l]) * (1.0 + sc1) + sh1
        u = h @ w_in[l]
        u_conv, u_q, u_kv, u_gate, u_lx, u_lg = jnp.split(u, IN_SPLITS, axis=-1)
        y_conv = conformer_conv(u_conv, conv_w_dw[l], conv_b_dw[l], conv_ln_g[l], conv_ln_b[l], conv_w_out[l])
        q = u_q.reshape(bsz, s, NSA_KV_HEADS, NSA_GROUP, HEAD_DIM)
        k_c, v_c, k_s, v_s, k_w, v_w = [t.reshape(bsz, s, NSA_KV_HEADS, HEAD_DIM) for t in jnp.split(u_kv, 6, axis=-1)]
        y_nsa = nsa_mixer(q, k_c, v_c, k_s, v_s, k_w, v_w, u_gate, nsa_pe_k[l], nsa_w_ck1[l], nsa_w_ck2[l],
                          nsa_pe_v[l], nsa_w_cv1[l], nsa_w_cv2[l], nsa_w_out[l])
        y_lru = rglru_block(u_lx, u_lg, lru_w_conv[l], lru_b_conv[l], lru_w_a[l], lru_b_a[l], lru_w_x[l],
                            lru_b_x[l], lru_lam[l], lru_w_out[l])
        gm_conv, gm_nsa, gm_lru = jnp.split(jax.nn.sigmoid(h @ w_merge[l] + b_merge[l]), N_BRANCH, axis=-1)
        merged = gm_conv * y_conv + gm_nsa * y_nsa + gm_lru * y_lru
        x = x + g1 * (merged @ w_out[l])
        h2 = rms_norm(x, norm_ffn[l]) * (1.0 + sc2) + sh2
        a, b = jnp.split(h2 @ w_ffn_in[l], 2, axis=-1)
        x = x + g2 * ((jax.nn.silu(a) * b) @ w_ffn_out[l])
    return rms_norm(x, final_norm)
```

```python
import functools

import numpy as np
import jax
import jax.numpy as jnp
from jax import lax
from jax.experimental import pallas as pl
from jax.experimental.pallas import tpu as pltpu

F32 = jnp.float32
BF16 = jnp.bfloat16

EPS = 1e-6
CONV_DIM = 512
CONV_WIDTH = 31
N_HEADS = 8
N_KV = 2
GROUP = N_HEADS // N_KV
HEAD_DIM = 64
CMP_LEN = 32
CMP_STRIDE = 16
CMP_HIDDEN = 128
SEL_BLOCK = 64
SEL_TOPK = 16
WINDOW = 512
Q_BLOCK = 128
FORCE_SCORE = 1e4
LRU_DIM = 512
LRU_HEADS = 8
LRU_CONV_WIDTH = 4
LRU_C = 8.0
N_BRANCH = 3

LANES = 128
NEG = -1e30
KEY_TILE = 512
WIN_TILE = 128
TOKEN_TILE = 512
BRANCH_TILE = 256
CONV_HALO = 32
LRU_HALO = 8
VMEM_LIMIT = 56 * 1024 * 1024

COL_CONV = 0
COL_Q = 1024
COL_KVC = 2048
COL_K = 2304
COL_V = 2816
COL_LRU = 3200
COL_END = 4224


def _params(n_grid):
    return pltpu.CompilerParams(dimension_semantics=("arbitrary",) * n_grid,
                                vmem_limit_bytes=VMEM_LIMIT)


def _dot(a, b):
    return jnp.dot(a, b, preferred_element_type=F32)


def _dot_nt(a, b):
    return lax.dot_general(a, b, (((1,), (1,)), ((), ())), preferred_element_type=F32)


def _split_bf16(a):
    hi = a.astype(BF16)
    lo = (a - hi.astype(F32)).astype(BF16)
    return hi, lo


def _dot3(a, b):
    a_hi, a_lo = _split_bf16(a)
    b_hi, b_lo = _split_bf16(b)
    return _dot(a_hi, b_hi) + (_dot(a_hi, b_lo) + _dot(a_lo, b_hi))


def _gelu(x):
    return 0.5 * x * (1.0 + jnp.tanh(0.7978845608028654 * (x + 0.044715 * (x * x * x))))


def _silu(x):
    return x * jax.nn.sigmoid(x)


def _mod_norm(x, g, sc, sh):
    y = x * lax.rsqrt(jnp.mean(x * x, axis=-1, keepdims=True) + EPS)
    return (y * g) * (1.0 + sc) + sh


def _mod_kernel(c_ref, w_ref, b_ref, o_ref):
    o_ref[...] = _dot3(c_ref[...], w_ref[...]) + b_ref[...]


def _modulation(c, w_mod, b_mod):
    depth, d, n = w_mod.shape
    bsz = c.shape[0]
    rows = 8
    c_pad = jnp.zeros((rows, d), F32).at[:bsz].set(c)
    out = pl.pallas_call(
        _mod_kernel,
        grid=(depth, n // d),
        in_specs=[pl.BlockSpec((rows, d), lambda l, j: (0, 0)),
                  pl.BlockSpec((None, d, d), lambda l, j: (l, 0, j)),
                  pl.BlockSpec((None, 1, d), lambda l, j: (l, 0, j))],
        out_specs=pl.BlockSpec((None, rows, d), lambda l, j: (l, 0, j)),
        out_shape=jax.ShapeDtypeStruct((depth, rows, n), F32),
        compiler_params=_params(2),
        name="modulation",
    )(c_pad, w_mod, b_mod.reshape(depth, 1, n))
    return out[:, :bsz].reshape(depth, bsz, n // d, 1, d)


def _premix_kernel(x_ref, g_ref, sc_ref, sh_ref, w_ref, qaug_ref,
                   uconv_ref, ulru_ref, q_ref, kvc_ref, ks_ref, kw_ref, vst_ref, vwt_ref, gt_ref):
    i = pl.program_id(1)
    tm = x_ref.shape[0]
    hb = _mod_norm(x_ref[...], g_ref[...], sc_ref[...], sh_ref[...]).astype(BF16)

    def proj(a, b):
        return _dot(hb, w_ref[:, a:b])

    uconv_ref[...] = proj(COL_CONV, COL_Q)
    ulru_ref[...] = proj(COL_LRU, COL_END)
    q_ref[...] = (proj(COL_Q, COL_KVC) + qaug_ref[...]).astype(BF16)
    kvc_ref[...] = proj(COL_KVC, COL_K).astype(BF16)
    pos = i * tm + lax.broadcasted_iota(jnp.int32, (tm, LANES), 0)
    lane = lax.broadcasted_iota(jnp.int32, (tm, LANES), 1)
    kaug = jnp.where(lane == HEAD_DIM, (pos >> 7).astype(F32),
                     jnp.where(lane == HEAD_DIM + 1, (pos & 127).astype(F32), 0.0))
    uk = proj(COL_K, COL_V)
    ks_ref[0] = (uk[:, 0:128] + kaug).astype(BF16)
    ks_ref[1] = (uk[:, 128:256] + kaug).astype(BF16)
    kw_ref[0] = (uk[:, 256:384] + kaug).astype(BF16)
    kw_ref[1] = (uk[:, 384:512] + kaug).astype(BF16)
    uv = proj(COL_V, COL_LRU)
    vst_ref[...] = uv[:, 0:128].T.astype(BF16)
    vwt = uv[:, 128:256].T.astype(BF16)
    for t in range(tm // WIN_TILE):
        vwt_ref[t] = vwt[:, t * WIN_TILE:(t + 1) * WIN_TILE]
    gt = jax.nn.sigmoid(uv[:, 256:384]).T
    gt_ref[...] = gt[0:32, :]


def _premix(x, mod, layer, norm_g, w_pack, qaug):
    bsz, s, d = x.shape
    tm = TOKEN_TILE
    assert tm == KEY_TILE and s % tm == 0
    n_t = s // tm
    row = lambda k: pl.BlockSpec((None, None, None, 1, d), lambda b, i: (layer, b, k, 0, 0))
    out_shapes = (
        jax.ShapeDtypeStruct((bsz, s, 1024), F32),
        jax.ShapeDtypeStruct((bsz, s, 1024), F32),
        jax.ShapeDtypeStruct((bsz, s, 1024), BF16),
        jax.ShapeDtypeStruct((bsz, s, 256), BF16),
        jax.ShapeDtypeStruct((bsz, N_KV, s, LANES), BF16),
        jax.ShapeDtypeStruct((bsz, N_KV, s, LANES), BF16),
        jax.ShapeDtypeStruct((bsz, n_t, LANES, KEY_TILE), BF16),
        jax.ShapeDtypeStruct((bsz, s // WIN_TILE, LANES, WIN_TILE), BF16),
        jax.ShapeDtypeStruct((bsz, 32, s), F32),
    )
    out_specs = (
        pl.BlockSpec((None, tm, 1024), lambda b, i: (b, i, 0)),
        pl.BlockSpec((None, tm, 1024), lambda b, i: (b, i, 0)),
        pl.BlockSpec((None, tm, 1024), lambda b, i: (b, i, 0)),
        pl.BlockSpec((None, tm, 256), lambda b, i: (b, i, 0)),
        pl.BlockSpec((None, N_KV, tm, LANES), lambda b, i: (b, 0, i, 0)),
        pl.BlockSpec((None, N_KV, tm, LANES), lambda b, i: (b, 0, i, 0)),
        pl.BlockSpec((None, None, LANES, KEY_TILE), lambda b, i: (b, i, 0, 0)),
        pl.BlockSpec((None, tm // WIN_TILE, LANES, WIN_TILE), lambda b, i: (b, i, 0, 0)),
        pl.BlockSpec((None, 32, tm), lambda b, i: (b, 0, i)),
    )
    return pl.pallas_call(
        _premix_kernel,
        grid=(bsz, n_t),
        in_specs=[pl.BlockSpec((None, tm, d), lambda b, i: (b, i, 0)),
                  pl.BlockSpec((1, d), lambda b, i: (0, 0)),
                  row(1), row(0),
                  pl.BlockSpec((d, COL_END), lambda b, i: (0, 0)),
                  pl.BlockSpec((1, 1024), lambda b, i: (0, 0))],
        out_specs=out_specs,
        out_shape=out_shapes,
        compiler_params=_params(2),
        name="premix",
    )(x, norm_g, mod, mod, w_pack, qaug)


def _branch_kernel(uconv_ref, ulru_ref, wdw_ref, bdw_ref, lng_ref, lnb_ref,
                   wc4_ref, bc4_ref, wa_ref, ba_ref, wx_ref, bx_ref, lam_ref,
                   vconv_ref, ylru_ref, vext_ref, xext_ref, hcar_ref):
    ts = uconv_ref.shape[0]
    c = CONV_DIM

    @pl.when(pl.program_id(1) == 0)
    def _():
        vext_ref[0:CONV_HALO, :] = jnp.zeros((CONV_HALO, c), F32)
        xext_ref[0:LRU_HALO, :] = jnp.zeros((LRU_HALO, c), F32)
        hcar_ref[...] = jnp.zeros(hcar_ref.shape, F32)

    v = uconv_ref[:, 0:c] * jax.nn.sigmoid(uconv_ref[:, c:2 * c])
    vext_ref[CONV_HALO:CONV_HALO + ts, :] = v
    acc = jnp.broadcast_to(bdw_ref[...], (ts, c))
    first = CONV_HALO - (CONV_WIDTH - 1)
    for j in range(CONV_WIDTH):
        acc = acc + wdw_ref[j:j + 1, :] * vext_ref[pl.ds(first + j, ts), :]
    vext_ref[0:CONV_HALO, :] = v[ts - CONV_HALO:ts, :]
    mu = jnp.mean(acc, axis=-1, keepdims=True)
    cen = acc - mu
    var = jnp.mean(cen * cen, axis=-1, keepdims=True)
    ln = (cen * lax.rsqrt(var + EPS)) * lng_ref[...] + lnb_ref[...]
    vconv_ref[...] = _silu(ln).astype(BF16)

    ux = ulru_ref[:, 0:c]
    xext_ref[LRU_HALO:LRU_HALO + ts, :] = ux
    xr = jnp.broadcast_to(bc4_ref[...], (ts, c))
    first = LRU_HALO - (LRU_CONV_WIDTH - 1)
    for j in range(LRU_CONV_WIDTH):
        xr = xr + wc4_ref[j:j + 1, :] * xext_ref[pl.ds(first + j, ts), :]
    xext_ref[0:LRU_HALO, :] = ux[ts - LRU_HALO:ts, :]
    xb = xr.astype(BF16)
    r = jax.nn.sigmoid(_dot(xb, wa_ref[...]) + ba_ref[...])
    gate_i = jax.nn.sigmoid(_dot(xb, wx_ref[...]) + bx_ref[...])
    z = -lam_ref[...]
    softplus = jnp.maximum(z, 0.0) + jnp.log(1.0 + jnp.exp(-jnp.abs(z)))
    log_a = (-LRU_C * r) * softplus
    a = jnp.exp(log_a)
    b = jnp.sqrt(1.0 - a * a) * (gate_i * xr)
    rows = lax.broadcasted_iota(jnp.int32, (ts, c), 0)
    shift = 1
    while shift < ts:
        a_prev = pltpu.roll(a, shift, axis=0)
        b_prev = pltpu.roll(b, shift, axis=0)
        live = rows >= shift
        b = jnp.where(live, a * b_prev + b, b)
        a = jnp.where(live, a * a_prev, a)
        shift *= 2
    h = a * hcar_ref[0:1, :] + b
    hcar_ref[...] = jnp.broadcast_to(h[ts - 1:ts, :], hcar_ref.shape)
    ylru_ref[...] = (h * _gelu(ulru_ref[:, c:2 * c])).astype(BF16)


def _branches(uconv, ulru, p):
    bsz, s, _ = uconv.shape
    ts = BRANCH_TILE
    c = CONV_DIM
    full = lambda shape: pl.BlockSpec(shape, lambda b, i: (0,) * len(shape))
    tok = lambda w: pl.BlockSpec((None, ts, w), lambda b, i: (b, i, 0))
    return pl.pallas_call(
        _branch_kernel,
        grid=(bsz, s // ts),
        in_specs=[tok(2 * c), tok(2 * c),
                  full((CONV_WIDTH, c)), full((1, c)), full((1, c)), full((1, c)),
                  full((LRU_CONV_WIDTH, c)), full((1, c)),
                  full((c, c)), full((1, c)), full((c, c)), full((1, c)), full((1, c))],
        out_specs=(tok(c), tok(c)),
        out_shape=(jax.ShapeDtypeStruct((bsz, s, c), BF16), jax.ShapeDtypeStruct((bsz, s, c), BF16)),
        scratch_shapes=[pltpu.VMEM((ts + CONV_HALO, c), F32),
                        pltpu.VMEM((ts + LRU_HALO, c), F32),
                        pltpu.VMEM((8, c), F32)],
        compiler_params=_params(2),
        name="branches",
    )(uconv, ulru, p["conv_w_dw"], p["conv_b_dw"], p["conv_ln_g"], p["conv_ln_b"],
      p["lru_w_conv"], p["lru_b_conv"], p["lru_wa_bd"], p["lru_b_a"], p["lru_wx_bd"], p["lru_b_x"],
      p["lru_lam"])


def _compress_kernel(x_ref, wlo_ref, whi_ref, pek_ref, pev_ref, w1k_ref, w1v_ref, w2_ref,
                     kc_ref, vct_ref):
    nc = x_ref.shape[0]
    x = x_ref[...]
    h_lo = _dot(x, wlo_ref[...])
    h_hi = _dot(x, whi_ref[...])
    h_next = pltpu.roll(h_hi, nc - 1, axis=0)
    bk = _dot3(pek_ref[...], w1k_ref[...])[0:1, :]
    bv = _dot3(pev_ref[...], w1v_ref[...])[0:1, :]
    bias = jnp.concatenate([bk, bk, bv, bv], axis=1)
    hid = _gelu(h_lo + h_next + bias).astype(BF16)
    out = _dot(hid, w2_ref[...])
    cend = lax.broadcasted_iota(jnp.int32, (nc, LANES), 0) * CMP_STRIDE + (CMP_LEN - 1)
    lane = lax.broadcasted_iota(jnp.int32, (nc, LANES), 1)
    aug = jnp.where(lane == HEAD_DIM, (cend >> 7).astype(F32),
                    jnp.where(lane == HEAD_DIM + 1, (cend & 127).astype(F32), 0.0))
    kc_ref[0] = (out[:, 0:128] + aug).astype(BF16)
    kc_ref[1] = (out[:, 128:256] + aug).astype(BF16)
    vct_ref[...] = out[:, 256:384].T.astype(BF16)


def _compress(kvc, p):
    bsz, s, _ = kvc.shape
    nc = s // CMP_STRIDE
    width = CMP_STRIDE * 256
    x = kvc.reshape(bsz, nc, width)
    full = lambda shape: pl.BlockSpec(shape, lambda b: (0,) * len(shape))
    return pl.pallas_call(
        _compress_kernel,
        grid=(bsz,),
        in_specs=[pl.BlockSpec((None, nc, width), lambda b: (b, 0, 0)),
                  full((width, 512)), full((width, 512)),
                  full((8, CMP_LEN * HEAD_DIM)), full((8, CMP_LEN * HEAD_DIM)),
                  full((CMP_LEN * HEAD_DIM, CMP_HIDDEN)), full((CMP_LEN * HEAD_DIM, CMP_HIDDEN)),
                  full((512, 384))],
        out_specs=(pl.BlockSpec((None, N_KV, nc, LANES), lambda b: (b, 0, 0, 0)),
                   pl.BlockSpec((None, LANES, nc), lambda b: (b, 0, 0))),
        out_shape=(jax.ShapeDtypeStruct((bsz, N_KV, nc, LANES), BF16),
                   jax.ShapeDtypeStruct((bsz, LANES, nc), BF16)),
        compiler_params=_params(1),
        name="compress",
    )(x, p["cmp_w_lo"], p["cmp_w_hi"], p["cmp_pe_k"], p["cmp_pe_v"], p["nsa_w_ck1"], p["nsa_w_cv1"],
      p["cmp_w2"])


def _attn_kernel(q_ref, kc_ref, vct_ref, ovt_ref, ks_ref, vst_ref, kw_ref, vwt_ref, gt_ref,
                 o_ref, selneg_ref):
    qb = pl.program_id(2)
    q0 = qb * Q_BLOCK
    nc = kc_ref.shape[0]
    n_sel = ovt_ref.shape[0]
    n_rows = GROUP * Q_BLOCK
    qg = jnp.concatenate([q_ref[:, r * LANES:(r + 1) * LANES] for r in range(GROUP)], axis=0)
    qpos = q0 + (lax.broadcasted_iota(jnp.int32, (1, n_rows), 1) & (Q_BLOCK - 1))

    s_c = _dot_nt(kc_ref[...], qg)
    cend = lax.broadcasted_iota(jnp.int32, (nc, n_rows), 0) * CMP_STRIDE + (CMP_LEN - 1)
    vis = cend <= qpos
    s_c = jnp.where(vis, s_c, NEG)
    m_c = jnp.max(s_c, axis=0, keepdims=True)
    p_c = jnp.where(vis, jnp.exp(s_c - m_c), 0.0)
    den = jnp.sum(p_c, axis=0, keepdims=True)
    p_c = p_c / jnp.where(den > 0, den, 1.0)
    o_c = _dot(vct_ref[...], p_c.astype(BF16))

    p_sum = p_c[:, 0:Q_BLOCK]
    for r in range(1, GROUP):
        p_sum = p_sum + p_c[:, r * Q_BLOCK:(r + 1) * Q_BLOCK]
    p_hi, p_lo = _split_bf16(p_sum)
    imp = _dot(ovt_ref[...], p_hi) + _dot(ovt_ref[...], p_lo)
    blk = lax.broadcasted_iota(jnp.int32, (n_sel, Q_BLOCK), 0)
    qp = q0 + lax.broadcasted_iota(jnp.int32, (n_sel, Q_BLOCK), 1)
    qid = qp // SEL_BLOCK
    forced = (blk == 0) | (blk == qid) | (blk == qid - 1)
    valid = blk * SEL_BLOCK <= qp
    work = jnp.where(forced, FORCE_SCORE, jnp.where(valid, imp, -1.0))
    chosen = jnp.zeros((n_sel, Q_BLOCK), F32)
    for _ in range(min(SEL_TOPK, n_sel)):
        best = jnp.max(work, axis=0, keepdims=True)
        first = jnp.min(jnp.where(work == best, blk, n_sel), axis=0, keepdims=True)
        pick = blk == first
        chosen = jnp.where(pick, 1.0, chosen)
        work = jnp.where(pick, -jnp.inf, work)
    selneg = jnp.where(chosen > 0, 0.0, NEG)
    selneg_ref[...] = jnp.concatenate([selneg] * GROUP, axis=1)

    blocks_per_tile = KEY_TILE // SEL_BLOCK
    n_tiles = q0 // KEY_TILE + 1
    krow = lax.broadcasted_iota(jnp.int32, (KEY_TILE, n_rows), 0)

    def sel_tile(kt, carry):
        m, l, acc = carry
        k0 = pl.multiple_of(kt * KEY_TILE, KEY_TILE)
        s = _dot_nt(ks_ref[pl.ds(k0, KEY_TILE), :], qg)
        sel_rows = selneg_ref[pl.ds(pl.multiple_of(kt * blocks_per_tile, blocks_per_tile),
                                    blocks_per_tile), :]
        bias = jnp.concatenate(
            [jnp.broadcast_to(sel_rows[j:j + 1, :], (SEL_BLOCK, n_rows)) for j in range(blocks_per_tile)],
            axis=0)
        s = jnp.where(k0 + krow <= qpos, s + bias, NEG)
        m_new = jnp.maximum(m, jnp.max(s, axis=0, keepdims=True))
        alpha = jnp.exp(m - m_new)
        p = jnp.exp(s - m_new)
        l = alpha * l + jnp.sum(p, axis=0, keepdims=True)
        acc = alpha * acc + _dot(vst_ref[kt], p.astype(BF16))
        return m_new, l, acc

    init = (jnp.full((1, n_rows), NEG, F32), jnp.zeros((1, n_rows), F32),
            jnp.zeros((HEAD_DIM, n_rows), F32))
    _, l_s, acc_s = lax.fori_loop(0, n_tiles, sel_tile, init)
    o_s = acc_s / l_s

    n_win = WINDOW // WIN_TILE + 1
    t0 = jnp.maximum(qb - WINDOW // WIN_TILE, 0)
    kstart = pl.multiple_of(t0 * WIN_TILE, WIN_TILE)
    s_w = _dot_nt(kw_ref[pl.ds(kstart, n_win * WIN_TILE), :], qg)
    dist = qpos - (kstart + lax.broadcasted_iota(jnp.int32, (n_win * WIN_TILE, n_rows), 0))
    s_w = jnp.where((dist >= 0) & (dist < WINDOW), s_w, NEG)
    m_w = jnp.max(s_w, axis=0, keepdims=True)
    p_w = jnp.exp(s_w - m_w)
    l_w = jnp.sum(p_w, axis=0, keepdims=True)
    p_w = p_w.astype(BF16)
    o_w = _dot(vwt_ref[t0], p_w[0:WIN_TILE, :])
    for t in range(1, n_win):
        o_w = o_w + _dot(vwt_ref[t0 + t], p_w[t * WIN_TILE:(t + 1) * WIN_TILE, :])
    o_w = o_w / l_w

    def gate(branch):
        return jnp.concatenate([gt_ref[branch * GROUP + r:branch * GROUP + r + 1, :] for r in range(GROUP)],
                               axis=1)
    o = gate(0) * o_c + gate(1) * o_s + gate(2) * o_w
    o_ref[...] = jnp.concatenate([o[:, r * Q_BLOCK:(r + 1) * Q_BLOCK].T for r in range(GROUP)],
                                 axis=1).astype(BF16)


def _attention(q, kc, vct, ovt, ks, vst, kw, vwt, gt):
    bsz, s, _ = q.shape
    nc = kc.shape[2]
    n_sel = s // SEL_BLOCK
    n_qb = s // Q_BLOCK
    n_kt = s // KEY_TILE
    n_wt = s // WIN_TILE
    assert s % KEY_TILE == 0 and s >= (WINDOW + WIN_TILE) and n_sel % 8 == 0
    return pl.pallas_call(
        _attn_kernel,
        grid=(bsz, N_KV, n_qb),
        in_specs=[pl.BlockSpec((None, Q_BLOCK, GROUP * LANES), lambda b, g, i: (b, i, g)),
                  pl.BlockSpec((None, None, nc, LANES), lambda b, g, i: (b, g, 0, 0)),
                  pl.BlockSpec((None, HEAD_DIM, nc), lambda b, g, i: (b, g, 0)),
                  pl.BlockSpec((n_sel, nc), lambda b, g, i: (0, 0)),
                  pl.BlockSpec((None, None, s, LANES), lambda b, g, i: (b, g, 0, 0)),
                  pl.BlockSpec((None, n_kt, HEAD_DIM, KEY_TILE), lambda b, g, i: (b, 0, g, 0)),
                  pl.BlockSpec((None, None, s, LANES), lambda b, g, i: (b, g, 0, 0)),
                  pl.BlockSpec((None, n_wt, HEAD_DIM, WIN_TILE), lambda b, g, i: (b, 0, g, 0)),
                  pl.BlockSpec((None, 16, Q_BLOCK), lambda b, g, i: (b, g, i))],
        out_specs=pl.BlockSpec((None, Q_BLOCK, GROUP * HEAD_DIM), lambda b, g, i: (b, i, g)),
        out_shape=jax.ShapeDtypeStruct((bsz, s, N_HEADS * HEAD_DIM), BF16),
        scratch_shapes=[pltpu.VMEM((n_sel, GROUP * Q_BLOCK), F32)],
        compiler_params=_params(3),
        name="nsa_attention",
    )(q, kc, vct, ovt, ks, vst, kw, vwt, gt)


def _merge_kernel(x_ref, g_ref, sc_ref, sh_ref, gate_ref, vconv_ref, onsa_ref, ylru_ref,
                  wm_ref, bm_ref, wc_ref, wn_ref, wl_ref, wo_ref, o_ref):
    d = x_ref.shape[1]
    x = x_ref[...]
    hb = _mod_norm(x, g_ref[...], sc_ref[...], sh_ref[...]).astype(BF16)
    merged = None
    branches = ((vconv_ref, wc_ref), (onsa_ref, wn_ref), (ylru_ref, wl_ref))
    for k, (y_ref, w_ref) in enumerate(branches):
        gm = jax.nn.sigmoid(_dot(hb, wm_ref[:, k * d:(k + 1) * d]) + bm_ref[:, k * d:(k + 1) * d])
        term = gm * _dot(y_ref[...], w_ref[...])
        merged = term if merged is None else merged + term
    o_ref[...] = x + gate_ref[...] * _dot(merged.astype(BF16), wo_ref[...])


def _merge(x, mod, layer, norm_g, vconv, onsa, ylru, p):
    bsz, s, d = x.shape
    tm = TOKEN_TILE
    c = CONV_DIM
    row = lambda k: pl.BlockSpec((None, None, None, 1, d), lambda b, i: (layer, b, k, 0, 0))
    full = lambda shape: pl.BlockSpec(shape, lambda b, i: (0,) * len(shape))
    tok = lambda w: pl.BlockSpec((None, tm, w), lambda b, i: (b, i, 0))
    return pl.pallas_call(
        _merge_kernel,
        grid=(bsz, s // tm),
        in_specs=[tok(d), full((1, d)), row(1), row(0), row(2), tok(c), tok(c), tok(c),
                  full((d, N_BRANCH * d)), full((1, N_BRANCH * d)),
                  full((c, d)), full((c, d)), full((c, d)), full((d, d))],
        out_specs=tok(d),
        out_shape=jax.ShapeDtypeStruct((bsz, s, d), F32),
        compiler_params=_params(2),
        name="merge",
    )(x, norm_g, mod, mod, mod, vconv, onsa, ylru, p["w_merge"], p["b_merge"],
      p["conv_w_out"], p["nsa_w_out"], p["lru_w_out"], p["w_out"])


def _ffn_kernel(x_ref, g_ref, sc_ref, sh_ref, gate_ref, wa_ref, wb_ref, wo_ref, fin_ref, o_ref,
                *, chunk, final):
    x = x_ref[...]
    hb = _mod_norm(x, g_ref[...], sc_ref[...], sh_ref[...]).astype(BF16)
    ffn = wa_ref.shape[1]
    acc = None
    for c0 in range(0, ffn, chunk):
        a = _dot(hb, wa_ref[:, c0:c0 + chunk])
        b = _dot(hb, wb_ref[:, c0:c0 + chunk])
        part = _dot((_silu(a) * b).astype(BF16), wo_ref[c0:c0 + chunk, :])
        acc = part if acc is None else acc + part
    y = x + gate_ref[...] * acc
    if final:
        y = (y * lax.rsqrt(jnp.mean(y * y, axis=-1, keepdims=True) + EPS)) * fin_ref[...]
    o_ref[...] = y


def _ffn(x, mod, layer, norm_g, p, final_norm, final):
    bsz, s, d = x.shape
    tm = TOKEN_TILE
    ffn = p["w_ffn_a"].shape[1]
    chunk = 256
    assert ffn % chunk == 0
    row = lambda k: pl.BlockSpec((None, None, None, 1, d), lambda b, i: (layer, b, k, 0, 0))
    full = lambda shape: pl.BlockSpec(shape, lambda b, i: (0,) * len(shape))
    tok = pl.BlockSpec((None, tm, d), lambda b, i: (b, i, 0))
    return pl.pallas_call(
        functools.partial(_ffn_kernel, chunk=chunk, final=final),
        grid=(bsz, s // tm),
        in_specs=[tok, full((1, d)), row(4), row(3), row(5),
                  full((d, ffn)), full((d, ffn)), full((ffn, d)), full((1, d))],
        out_specs=tok,
        out_shape=jax.ShapeDtypeStruct((bsz, s, d), F32),
        compiler_params=_params(2),
        name="ffn",
    )(x, norm_g, mod, mod, mod, p["w_ffn_a"], p["w_ffn_b"], p["w_ffn_out"], final_norm)


def _pack_columns():
    q0, kv0, gate0, lru0 = 1024, 1536, 2304, 2328
    src = np.full((COL_END,), -1, np.int64)
    scale = np.ones((COL_END,), np.float32)
    src[COL_CONV:COL_CONV + 1024] = np.arange(1024)
    for h in range(N_HEADS):
        base = COL_Q + h * LANES
        src[base:base + HEAD_DIM] = q0 + h * HEAD_DIM + np.arange(HEAD_DIM)
        scale[base:base + HEAD_DIM] = HEAD_DIM ** -0.5
    src[COL_KVC:COL_KVC + 256] = kv0 + np.arange(256)
    for branch, off in ((0, 256), (1, 512)):
        for g in range(N_KV):
            base = COL_K + (branch * N_KV + g) * LANES
            src[base:base + HEAD_DIM] = kv0 + off + g * HEAD_DIM + np.arange(HEAD_DIM)
    src[COL_V:COL_V + 128] = kv0 + 384 + np.arange(128)
    src[COL_V + 128:COL_V + 256] = kv0 + 640 + np.arange(128)
    for g in range(N_KV):
        for br in range(3):
            for r in range(GROUP):
                src[COL_V + 256 + g * 16 + br * GROUP + r] = gate0 + g * GROUP * 3 + r * 3 + br
    src[COL_LRU:COL_LRU + 1024] = lru0 + np.arange(1024)
    return src, scale


def _q_slope_row():
    row = np.zeros((1, N_HEADS * LANES), np.float32)
    for h in range(N_HEADS):
        slope = 2.0 ** (-8.0 * (h + 1) / N_HEADS)
        row[0, h * LANES + HEAD_DIM] = slope * 128.0
        row[0, h * LANES + HEAD_DIM + 1] = slope
    return row


def _overlap_matrix(s):
    nc = s // CMP_STRIDE
    n_sel = s // SEL_BLOCK
    cs = np.arange(nc)[:, None] * CMP_STRIDE
    ss = np.arange(n_sel)[None, :] * SEL_BLOCK
    ov = np.clip(np.minimum(cs + CMP_LEN, ss + SEL_BLOCK) - np.maximum(cs, ss), 0, None) / CMP_LEN
    return np.ascontiguousarray(ov.T.astype(np.float32))


def _block_diag(w):
    heads, n, _ = w.shape
    out = jnp.zeros((heads * n, heads * n), w.dtype)
    for h in range(heads):
        out = out.at[h * n:(h + 1) * n, h * n:(h + 1) * n].set(w[h])
    return out


def _layer_params(l, a):
    src, scale = _pack_columns()
    w_in = a["w_in"][l]
    w_pack = jnp.where(jnp.asarray(src >= 0)[None, :], w_in[:, np.maximum(src, 0)], 0.0) * jnp.asarray(scale)[None, :]
    row = lambda v: v[l].reshape(1, -1)

    def chunk_weights(half):
        wk = a["nsa_w_ck1"][l].reshape(CMP_LEN, HEAD_DIM, CMP_HIDDEN)[half * CMP_STRIDE:(half + 1) * CMP_STRIDE]
        wv = a["nsa_w_cv1"][l].reshape(CMP_LEN, HEAD_DIM, CMP_HIDDEN)[half * CMP_STRIDE:(half + 1) * CMP_STRIDE]
        w = jnp.zeros((CMP_STRIDE, 4, HEAD_DIM, 4, CMP_HIDDEN), F32)
        for slot, src_w in enumerate((wk, wk, wv, wv)):
            w = w.at[:, slot, :, slot, :].set(src_w)
        return w.reshape(CMP_STRIDE * 256, 4 * CMP_HIDDEN).astype(BF16)

    w2 = jnp.zeros((4 * CMP_HIDDEN, 384), F32)
    w2 = w2.at[0:128, 0:64].set(a["nsa_w_ck2"][l]).at[128:256, 128:192].set(a["nsa_w_ck2"][l])
    w2 = w2.at[256:384, 256:320].set(a["nsa_w_cv2"][l]).at[384:512, 320:384].set(a["nsa_w_cv2"][l])
    pe_rows = lambda pe: jnp.zeros((8, CMP_LEN * HEAD_DIM), F32).at[0].set(pe.reshape(-1))
    ffn = a["w_ffn_out"].shape[1]
    return {
        "w_pack": w_pack.astype(BF16),
        "conv_w_dw": a["conv_w_dw"][l], "conv_b_dw": row(a["conv_b_dw"]),
        "conv_ln_g": row(a["conv_ln_g"]), "conv_ln_b": row(a["conv_ln_b"]),
        "lru_w_conv": a["lru_w_conv"][l], "lru_b_conv": row(a["lru_b_conv"]),
        "lru_wa_bd": _block_diag(a["lru_w_a"][l]).astype(BF16), "lru_b_a": row(a["lru_b_a"]),
        "lru_wx_bd": _block_diag(a["lru_w_x"][l]).astype(BF16), "lru_b_x": row(a["lru_b_x"]),
        "lru_lam": row(a["lru_lam"]),
        "cmp_w_lo": chunk_weights(0), "cmp_w_hi": chunk_weights(1),
        "cmp_pe_k": pe_rows(a["nsa_pe_k"][l]), "cmp_pe_v": pe_rows(a["nsa_pe_v"][l]),
        "nsa_w_ck1": a["nsa_w_ck1"][l], "nsa_w_cv1": a["nsa_w_cv1"][l],
        "cmp_w2": w2.astype(BF16),
        "w_merge": a["w_merge"][l].astype(BF16), "b_merge": row(a["b_merge"]),
        "conv_w_out": a["conv_w_out"][l].astype(BF16), "nsa_w_out": a["nsa_w_out"][l].astype(BF16),
        "lru_w_out": a["lru_w_out"][l].astype(BF16), "w_out": a["w_out"][l].astype(BF16),
        "w_ffn_a": a["w_ffn_in"][l][:, :ffn].astype(BF16), "w_ffn_b": a["w_ffn_in"][l][:, ffn:].astype(BF16),
        "w_ffn_out": a["w_ffn_out"][l].astype(BF16),
    }


def kernel(x, c, w_mod, b_mod, norm_mix, norm_ffn, w_in, conv_w_dw, conv_b_dw, conv_ln_g, conv_ln_b,
           conv_w_out, nsa_pe_k, nsa_w_ck1, nsa_w_ck2, nsa_pe_v, nsa_w_cv1, nsa_w_cv2, nsa_w_out,
           lru_w_conv, lru_b_conv, lru_w_a, lru_b_a, lru_w_x, lru_b_x, lru_lam, lru_w_out,
           w_merge, b_merge, w_out, w_ffn_in, w_ffn_out, final_norm):
    a = dict(w_in=w_in, conv_w_dw=conv_w_dw, conv_b_dw=conv_b_dw, conv_ln_g=conv_ln_g, conv_ln_b=conv_ln_b,
             conv_w_out=conv_w_out, nsa_pe_k=nsa_pe_k, nsa_w_ck1=nsa_w_ck1, nsa_w_ck2=nsa_w_ck2,
             nsa_pe_v=nsa_pe_v, nsa_w_cv1=nsa_w_cv1, nsa_w_cv2=nsa_w_cv2, nsa_w_out=nsa_w_out,
             lru_w_conv=lru_w_conv, lru_b_conv=lru_b_conv, lru_w_a=lru_w_a, lru_b_a=lru_b_a,
             lru_w_x=lru_w_x, lru_b_x=lru_b_x, lru_lam=lru_lam, lru_w_out=lru_w_out,
             w_merge=w_merge, b_merge=b_merge, w_out=w_out, w_ffn_in=w_ffn_in, w_ffn_out=w_ffn_out)
    depth = w_in.shape[0]
    s = x.shape[1]
    mod = _modulation(c, w_mod, b_mod)
    qaug = jnp.asarray(_q_slope_row())
    ovt = jnp.asarray(_overlap_matrix(s)).astype(BF16)
    fin = final_norm.reshape(1, -1)
    for l in range(depth):
        p = _layer_params(l, a)
        g_mix = norm_mix[l].reshape(1, -1)
        g_ffn = norm_ffn[l].reshape(1, -1)
        uconv, ulru, q, kvc, ks, kw, vst, vwt, gt = _premix(x, mod, l, g_mix, p["w_pack"], qaug)
        vconv, ylru = _branches(uconv, ulru, p)
        kc, vct = _compress(kvc, p)
        onsa = _attention(q, kc, vct, ovt, ks, vst, kw, vwt, gt)
        x = _merge(x, mod, l, g_mix, vconv, onsa, ylru, p)
        x = _ffn(x, mod, l, g_ffn, p, fin, final=(l == depth - 1))
    return x
```

```python
import functools

import numpy as np
import jax
import jax.numpy as jnp
from jax import lax
from jax.experimental import pallas as pl
from jax.experimental.pallas import tpu as pltpu

F32 = jnp.float32
BF16 = jnp.bfloat16

EPS = 1e-6
CONV_DIM = 512
CONV_WIDTH = 31
N_HEADS = 8
N_KV = 2
GROUP = N_HEADS // N_KV
HEAD_DIM = 64
CMP_LEN = 32
CMP_STRIDE = 16
CMP_HIDDEN = 128
SEL_BLOCK = 64
SEL_TOPK = 16
WINDOW = 512
Q_BLOCK = 128
FORCE_SCORE = 1e4
LRU_DIM = 512
LRU_HEADS = 8
LRU_CONV_WIDTH = 4
LRU_C = 8.0
N_BRANCH = 3

LANES = 128
NEG = -1e30
KEY_TILE = 512
WIN_TILE = 128
CMP_CHUNK = 256
TOKEN_TILE = 512
BRANCH_TILE = 256
CONV_HALO = 32
LRU_HALO = 8
VMEM_LIMIT = 56 * 1024 * 1024

COL_CONV = 0
COL_Q = 1024
COL_KVC = 2048
COL_K = 2304
COL_V = 2816
COL_LRU = 3200
COL_END = 4224


def _params(n_grid):
    return pltpu.CompilerParams(dimension_semantics=("arbitrary",) * n_grid,
                                vmem_limit_bytes=VMEM_LIMIT)


def _dot(a, b):
    return jnp.dot(a, b, preferred_element_type=F32)


def _dot_nt(a, b):
    return lax.dot_general(a, b, (((1,), (1,)), ((), ())), preferred_element_type=F32)


def _split_bf16(a):
    hi = a.astype(BF16)
    lo = (a - hi.astype(F32)).astype(BF16)
    return hi, lo


def _dot3(a, b):
    a_hi, a_lo = _split_bf16(a)
    b_hi, b_lo = _split_bf16(b)
    return _dot(a_hi, b_hi) + (_dot(a_hi, b_lo) + _dot(a_lo, b_hi))


def _gelu(x):
    return 0.5 * x * (1.0 + jnp.tanh(0.7978845608028654 * (x + 0.044715 * (x * x * x))))


def _silu(x):
    return x * jax.nn.sigmoid(x)


def _mod_norm(x, g, sc, sh):
    y = x * lax.rsqrt(jnp.mean(x * x, axis=-1, keepdims=True) + EPS)
    return (y * g) * (1.0 + sc) + sh


def _mod_kernel(c_ref, w_ref, b_ref, o_ref):
    o_ref[...] = _dot3(c_ref[...], w_ref[...]) + b_ref[...]


def _modulation(c, w_mod, b_mod):
    depth, d, n = w_mod.shape
    bsz = c.shape[0]
    rows = 8
    c_pad = jnp.zeros((rows, d), F32).at[:bsz].set(c)
    out = pl.pallas_call(
        _mod_kernel,
        grid=(depth, n // d),
        in_specs=[pl.BlockSpec((rows, d), lambda l, j: (0, 0)),
                  pl.BlockSpec((None, d, d), lambda l, j: (l, 0, j)),
                  pl.BlockSpec((None, 1, d), lambda l, j: (l, 0, j))],
        out_specs=pl.BlockSpec((None, rows, d), lambda l, j: (l, 0, j)),
        out_shape=jax.ShapeDtypeStruct((depth, rows, n), F32),
        compiler_params=_params(2),
        name="modulation",
    )(c_pad, w_mod, b_mod.reshape(depth, 1, n))
    return out[:, :bsz].reshape(depth, bsz, n // d, 1, d)


def _premix_kernel(x_ref, g_ref, sc_ref, sh_ref, w_ref, qaug_ref,
                   uconv_ref, ulru_ref, q_ref, kvc_ref, ks_ref, kw_ref, vst_ref, vwt_ref, gt_ref):
    i = pl.program_id(1)
    tm = x_ref.shape[0]
    hb = _mod_norm(x_ref[...], g_ref[...], sc_ref[...], sh_ref[...]).astype(BF16)

    def proj(a, b):
        return _dot(hb, w_ref[:, a:b])

    uconv_ref[...] = proj(COL_CONV, COL_Q)
    ulru_ref[...] = proj(COL_LRU, COL_END)
    q_ref[...] = (proj(COL_Q, COL_KVC) + qaug_ref[...]).astype(BF16)
    kvc_ref[...] = proj(COL_KVC, COL_K).astype(BF16)
    pos = i * tm + lax.broadcasted_iota(jnp.int32, (tm, LANES), 0)
    lane = lax.broadcasted_iota(jnp.int32, (tm, LANES), 1)
    kaug = jnp.where(lane == HEAD_DIM, (pos >> 7).astype(F32),
                     jnp.where(lane == HEAD_DIM + 1, (pos & 127).astype(F32), 0.0))
    uk = proj(COL_K, COL_V)
    ks_ref[0] = (uk[:, 0:128] + kaug).astype(BF16)
    ks_ref[1] = (uk[:, 128:256] + kaug).astype(BF16)
    kw_ref[0] = (uk[:, 256:384] + kaug).astype(BF16)
    kw_ref[1] = (uk[:, 384:512] + kaug).astype(BF16)
    uv = proj(COL_V, COL_LRU)
    vst_ref[...] = uv[:, 0:128].T.astype(BF16)
    vwt = uv[:, 128:256].T.astype(BF16)
    for t in range(tm // WIN_TILE):
        vwt_ref[t] = vwt[:, t * WIN_TILE:(t + 1) * WIN_TILE]
    gt = jax.nn.sigmoid(uv[:, 256:384]).T
    gt_ref[...] = gt[0:32, :]


def _premix(x, mod, layer, norm_g, w_pack, qaug):
    bsz, s, d = x.shape
    tm = TOKEN_TILE
    assert tm == KEY_TILE and s % tm == 0
    n_t = s // tm
    row = lambda k: pl.BlockSpec((None, None, None, 1, d), lambda b, i: (layer, b, k, 0, 0))
    out_shapes = (
        jax.ShapeDtypeStruct((bsz, s, 1024), F32),
        jax.ShapeDtypeStruct((bsz, s, 1024), F32),
        jax.ShapeDtypeStruct((bsz, s, 1024), BF16),
        jax.ShapeDtypeStruct((bsz, s, 256), BF16),
        jax.ShapeDtypeStruct((bsz, N_KV, s, LANES), BF16),
        jax.ShapeDtypeStruct((bsz, N_KV, s, LANES), BF16),
        jax.ShapeDtypeStruct((bsz, n_t, LANES, KEY_TILE), BF16),
        jax.ShapeDtypeStruct((bsz, s // WIN_TILE, LANES, WIN_TILE), BF16),
        jax.ShapeDtypeStruct((bsz, 32, s), F32),
    )
    out_specs = (
        pl.BlockSpec((None, tm, 1024), lambda b, i: (b, i, 0)),
        pl.BlockSpec((None, tm, 1024), lambda b, i: (b, i, 0)),
        pl.BlockSpec((None, tm, 1024), lambda b, i: (b, i, 0)),
        pl.BlockSpec((None, tm, 256), lambda b, i: (b, i, 0)),
        pl.BlockSpec((None, N_KV, tm, LANES), lambda b, i: (b, 0, i, 0)),
        pl.BlockSpec((None, N_KV, tm, LANES), lambda b, i: (b, 0, i, 0)),
        pl.BlockSpec((None, None, LANES, KEY_TILE), lambda b, i: (b, i, 0, 0)),
        pl.BlockSpec((None, tm // WIN_TILE, LANES, WIN_TILE), lambda b, i: (b, i, 0, 0)),
        pl.BlockSpec((None, 32, tm), lambda b, i: (b, 0, i)),
    )
    return pl.pallas_call(
        _premix_kernel,
        grid=(bsz, n_t),
        in_specs=[pl.BlockSpec((None, tm, d), lambda b, i: (b, i, 0)),
                  pl.BlockSpec((1, d), lambda b, i: (0, 0)),
                  row(1), row(0),
                  pl.BlockSpec((d, COL_END), lambda b, i: (0, 0)),
                  pl.BlockSpec((1, 1024), lambda b, i: (0, 0))],
        out_specs=out_specs,
        out_shape=out_shapes,
        compiler_params=_params(2),
        name="premix",
    )(x, norm_g, mod, mod, w_pack, qaug)


def _branch_kernel(uconv_ref, ulru_ref, wdw_ref, bdw_ref, lng_ref, lnb_ref,
                   wc4_ref, bc4_ref, wa_ref, ba_ref, wx_ref, bx_ref, lam_ref,
                   vconv_ref, ylru_ref, vext_ref, xext_ref, hcar_ref):
    ts = uconv_ref.shape[0]
    c = CONV_DIM

    @pl.when(pl.program_id(1) == 0)
    def _():
        vext_ref[0:CONV_HALO, :] = jnp.zeros((CONV_HALO, c), F32)
        xext_ref[0:LRU_HALO, :] = jnp.zeros((LRU_HALO, c), F32)
        hcar_ref[...] = jnp.zeros(hcar_ref.shape, F32)

    v = uconv_ref[:, 0:c] * jax.nn.sigmoid(uconv_ref[:, c:2 * c])
    vext_ref[CONV_HALO:CONV_HALO + ts, :] = v
    acc = jnp.broadcast_to(bdw_ref[...], (ts, c))
    first = CONV_HALO - (CONV_WIDTH - 1)
    for j in range(CONV_WIDTH):
        acc = acc + wdw_ref[j:j + 1, :] * vext_ref[pl.ds(first + j, ts), :]
    vext_ref[0:CONV_HALO, :] = v[ts - CONV_HALO:ts, :]
    mu = jnp.mean(acc, axis=-1, keepdims=True)
    cen = acc - mu
    var = jnp.mean(cen * cen, axis=-1, keepdims=True)
    ln = (cen * lax.rsqrt(var + EPS)) * lng_ref[...] + lnb_ref[...]
    vconv_ref[...] = _silu(ln).astype(BF16)

    ux = ulru_ref[:, 0:c]
    xext_ref[LRU_HALO:LRU_HALO + ts, :] = ux
    xr = jnp.broadcast_to(bc4_ref[...], (ts, c))
    first = LRU_HALO - (LRU_CONV_WIDTH - 1)
    for j in range(LRU_CONV_WIDTH):
        xr = xr + wc4_ref[j:j + 1, :] * xext_ref[pl.ds(first + j, ts), :]
    xext_ref[0:LRU_HALO, :] = ux[ts - LRU_HALO:ts, :]
    xb = xr.astype(BF16)
    r = jax.nn.sigmoid(_dot(xb, wa_ref[...]) + ba_ref[...])
    gate_i = jax.nn.sigmoid(_dot(xb, wx_ref[...]) + bx_ref[...])
    z = -lam_ref[...]
    softplus = jnp.maximum(z, 0.0) + jnp.log(1.0 + jnp.exp(-jnp.abs(z)))
    log_a = (-LRU_C * r) * softplus
    a = jnp.exp(log_a)
    b = jnp.sqrt(1.0 - a * a) * (gate_i * xr)
    rows = lax.broadcasted_iota(jnp.int32, (ts, c), 0)
    shift = 1
    while shift < ts:
        a_prev = pltpu.roll(a, shift, axis=0)
        b_prev = pltpu.roll(b, shift, axis=0)
        live = rows >= shift
        b = jnp.where(live, a * b_prev + b, b)
        a = jnp.where(live, a * a_prev, a)
        shift *= 2
    h = a * hcar_ref[0:1, :] + b
    hcar_ref[...] = jnp.broadcast_to(h[ts - 1:ts, :], hcar_ref.shape)
    ylru_ref[...] = (h * _gelu(ulru_ref[:, c:2 * c])).astype(BF16)


def _branches(uconv, ulru, p):
    bsz, s, _ = uconv.shape
    ts = BRANCH_TILE
    c = CONV_DIM
    full = lambda shape: pl.BlockSpec(shape, lambda b, i: (0,) * len(shape))
    tok = lambda w: pl.BlockSpec((None, ts, w), lambda b, i: (b, i, 0))
    return pl.pallas_call(
        _branch_kernel,
        grid=(bsz, s // ts),
        in_specs=[tok(2 * c), tok(2 * c),
                  full((CONV_WIDTH, c)), full((1, c)), full((1, c)), full((1, c)),
                  full((LRU_CONV_WIDTH, c)), full((1, c)),
                  full((c, c)), full((1, c)), full((c, c)), full((1, c)), full((1, c))],
        out_specs=(tok(c), tok(c)),
        out_shape=(jax.ShapeDtypeStruct((bsz, s, c), BF16), jax.ShapeDtypeStruct((bsz, s, c), BF16)),
        scratch_shapes=[pltpu.VMEM((ts + CONV_HALO, c), F32),
                        pltpu.VMEM((ts + LRU_HALO, c), F32),
                        pltpu.VMEM((8, c), F32)],
        compiler_params=_params(2),
        name="branches",
    )(uconv, ulru, p["conv_w_dw"], p["conv_b_dw"], p["conv_ln_g"], p["conv_ln_b"],
      p["lru_w_conv"], p["lru_b_conv"], p["lru_wa_bd"], p["lru_b_a"], p["lru_wx_bd"], p["lru_b_x"],
      p["lru_lam"])


def _compress_kernel(x_ref, wlo_ref, whi_ref, pek_ref, pev_ref, w1k_ref, w1v_ref, w2_ref,
                     kc_ref, vct_ref):
    nc = x_ref.shape[0]
    x = x_ref[...]
    h_lo = _dot(x, wlo_ref[...])
    h_hi = _dot(x, whi_ref[...])
    h_next = pltpu.roll(h_hi, nc - 1, axis=0)
    bk = _dot3(pek_ref[...], w1k_ref[...])[0:1, :]
    bv = _dot3(pev_ref[...], w1v_ref[...])[0:1, :]
    bias = jnp.concatenate([bk, bk, bv, bv], axis=1)
    hid = _gelu(h_lo + h_next + bias).astype(BF16)
    out = _dot(hid, w2_ref[...])
    cend = lax.broadcasted_iota(jnp.int32, (nc, LANES), 0) * CMP_STRIDE + (CMP_LEN - 1)
    lane = lax.broadcasted_iota(jnp.int32, (nc, LANES), 1)
    aug = jnp.where(lane == HEAD_DIM, (cend >> 7).astype(F32),
                    jnp.where(lane == HEAD_DIM + 1, (cend & 127).astype(F32), 0.0))
    kc_ref[0] = (out[:, 0:128] + aug).astype(BF16)
    kc_ref[1] = (out[:, 128:256] + aug).astype(BF16)
    vct = out[:, 256:384].T.astype(BF16)
    for t in range(nc // CMP_CHUNK):
        vct_ref[t] = vct[:, t * CMP_CHUNK:(t + 1) * CMP_CHUNK]


def _compress(kvc, p):
    bsz, s, _ = kvc.shape
    nc = s // CMP_STRIDE
    width = CMP_STRIDE * 256
    x = kvc.reshape(bsz, nc, width)
    full = lambda shape: pl.BlockSpec(shape, lambda b: (0,) * len(shape))
    return pl.pallas_call(
        _compress_kernel,
        grid=(bsz,),
        in_specs=[pl.BlockSpec((None, nc, width), lambda b: (b, 0, 0)),
                  full((width, 512)), full((width, 512)),
                  full((8, CMP_LEN * HEAD_DIM)), full((8, CMP_LEN * HEAD_DIM)),
                  full((CMP_LEN * HEAD_DIM, CMP_HIDDEN)), full((CMP_LEN * HEAD_DIM, CMP_HIDDEN)),
                  full((512, 384))],
        out_specs=(pl.BlockSpec((None, N_KV, nc, LANES), lambda b: (b, 0, 0, 0)),
                   pl.BlockSpec((None, nc // CMP_CHUNK, LANES, CMP_CHUNK), lambda b: (b, 0, 0, 0))),
        out_shape=(jax.ShapeDtypeStruct((bsz, N_KV, nc, LANES), BF16),
                   jax.ShapeDtypeStruct((bsz, nc // CMP_CHUNK, LANES, CMP_CHUNK), BF16)),
        compiler_params=_params(1),
        name="compress",
    )(x, p["cmp_w_lo"], p["cmp_w_hi"], p["cmp_pe_k"], p["cmp_pe_v"], p["nsa_w_ck1"], p["nsa_w_cv1"],
      p["cmp_w2"])


def _attn_kernel(q_ref, kc_ref, vct_ref, ovt_ref, ks_ref, vst_ref, kw_ref, vwt_ref, gt_ref,
                 o_ref, s0_ref, s1_ref, selneg_ref, m_ref, l_ref, acc_ref, flag_ref, list_ref):
    qb = pl.program_id(2)
    q0 = qb * Q_BLOCK
    n_sel = ovt_ref.shape[1]
    n_rows = GROUP * Q_BLOCK
    qg = jnp.concatenate([q_ref[:, r * LANES:(r + 1) * LANES] for r in range(GROUP)], axis=0)
    qpos = q0 + (lax.broadcasted_iota(jnp.int32, (1, n_rows), 1) & (Q_BLOCK - 1))

    n_chunks = ((q0 + Q_BLOCK - CMP_LEN) // CMP_STRIDE + CMP_CHUNK) // CMP_CHUNK
    cend = lax.broadcasted_iota(jnp.int32, (CMP_CHUNK, n_rows), 0) * CMP_STRIDE + (CMP_LEN - 1)

    def cmp_branch(n_ch):
        def fn():
            s_all = _dot_nt(kc_ref[0:n_ch * CMP_CHUNK, :], qg)
            scores = []
            m_c = None
            for ci in range(n_ch):
                s = s_all[ci * CMP_CHUNK:(ci + 1) * CMP_CHUNK, :]
                if ci >= n_ch - 2:
                    s = jnp.where(cend + ci * CMP_CHUNK * CMP_STRIDE <= qpos, s, NEG)
                scores.append(s)
                mi = jnp.max(s, axis=0, keepdims=True)
                m_c = mi if m_c is None else jnp.maximum(m_c, mi)
            m_c = jnp.where(m_c > 0.5 * NEG, m_c, 0.0)
            probs = [jnp.exp(s - m_c) for s in scores]
            den = probs[0].sum(axis=0, keepdims=True)
            for p in probs[1:]:
                den = den + jnp.sum(p, axis=0, keepdims=True)
            inv_den = 1.0 / jnp.where(den > 0, den, 1.0)
            o_c = None
            imp = None
            for ci, p in enumerate(probs):
                p = p * inv_den
                part = _dot(vct_ref[ci], p.astype(BF16))
                o_c = part if o_c is None else o_c + part
                p_sum = p[:, 0:Q_BLOCK]
                for r in range(1, GROUP):
                    p_sum = p_sum + p[:, r * Q_BLOCK:(r + 1) * Q_BLOCK]
                p_hi, p_lo = _split_bf16(p_sum)
                part = _dot(ovt_ref[ci], p_hi) + _dot(ovt_ref[ci], p_lo)
                imp = part if imp is None else imp + part
            return o_c, imp
        return fn

    o_c, imp = lax.switch(n_chunks - 1, [cmp_branch(n) for n in range(1, kc_ref.shape[0] // CMP_CHUNK + 1)])

    n_win = WINDOW // WIN_TILE + 1
    t0 = jnp.maximum(qb - WINDOW // WIN_TILE, 0)
    kstart = pl.multiple_of(t0 * WIN_TILE, WIN_TILE)
    s_w = _dot_nt(kw_ref[pl.ds(kstart, n_win * WIN_TILE), :], qg)
    dist = qpos - (kstart + lax.broadcasted_iota(jnp.int32, (n_win * WIN_TILE, n_rows), 0))
    s_w = jnp.where((dist >= 0) & (dist < WINDOW), s_w, NEG)
    m_w = jnp.max(s_w, axis=0, keepdims=True)
    p_w = jnp.exp(s_w - m_w)
    l_w = jnp.sum(p_w, axis=0, keepdims=True)
    p_w = p_w.astype(BF16)
    o_w = _dot(vwt_ref[t0], p_w[0:WIN_TILE, :])
    for t in range(1, n_win):
        o_w = o_w + _dot(vwt_ref[t0 + t], p_w[t * WIN_TILE:(t + 1) * WIN_TILE, :])
    o_w = o_w / l_w

    blk = lax.broadcasted_iota(jnp.int32, (n_sel, Q_BLOCK), 0)
    qp = q0 + lax.broadcasted_iota(jnp.int32, (n_sel, Q_BLOCK), 1)
    qid = qp // SEL_BLOCK
    forced = (blk == 0) | (blk == qid) | (blk == qid - 1)
    valid = blk * SEL_BLOCK <= qp
    work = jnp.where(forced, -jnp.inf, jnp.where(valid, imp, -1.0))
    chosen = jnp.where(forced, 1.0, 0.0)
    for _ in range(min(SEL_TOPK, n_sel) - 3):
        best = jnp.max(work, axis=0, keepdims=True)
        first = jnp.min(jnp.where(work == best, blk, n_sel), axis=0, keepdims=True)
        pick = blk == first
        chosen = jnp.where(pick, 1.0, chosen)
        work = jnp.where(pick, -jnp.inf, work)
    selneg = jnp.where(chosen > 0, 0.0, NEG)
    selneg_ref[...] = jnp.concatenate([selneg] * GROUP, axis=1)
    blocks_per_tile = KEY_TILE // SEL_BLOCK
    any_q = jnp.max(chosen, axis=1, keepdims=True)
    for t in range(n_sel // blocks_per_tile):
        tile_any = jnp.max(any_q[t * blocks_per_tile:(t + 1) * blocks_per_tile, :])
        flag_ref[t] = (tile_any > 0).astype(jnp.int32)

    n_tiles = q0 // KEY_TILE + 1
    krow = lax.broadcasted_iota(jnp.int32, (SEL_BLOCK, n_rows), 0)
    m_ref[...] = jnp.full((1, n_rows), NEG, F32)
    l_ref[...] = jnp.zeros((1, n_rows), F32)
    acc_ref[...] = jnp.zeros((HEAD_DIM, n_rows), F32)

    def compact(kt, n):
        @pl.when(flag_ref[kt] > 0)
        def _():
            list_ref[n] = kt
        return n + flag_ref[kt]

    n_act = lax.fori_loop(0, n_tiles - 1, compact, 0)
    list_ref[n_act] = n_tiles - 1

    def scores_into(buf_ref, kt):
        k0 = pl.multiple_of(kt * KEY_TILE, KEY_TILE)
        buf_ref[...] = _dot_nt(ks_ref[pl.ds(k0, KEY_TILE), :], qg)

    def sel_tile(buf_ref, kt, causal):
        sel_rows = selneg_ref[pl.ds(pl.multiple_of(kt * blocks_per_tile, blocks_per_tile),
                                    blocks_per_tile), :]
        m_old = m_ref[...]
        m_new = m_old
        parts = []
        for j in range(blocks_per_tile):
            sj = buf_ref[j * SEL_BLOCK:(j + 1) * SEL_BLOCK, :] + sel_rows[j:j + 1, :]
            if causal:
                sj = jnp.where(kt * KEY_TILE + j * SEL_BLOCK + krow <= qpos, sj, NEG)
            parts.append(sj)
            m_new = jnp.maximum(m_new, jnp.max(sj, axis=0, keepdims=True))
        alpha = jnp.exp(m_old - m_new)
        l_new = alpha * l_ref[...]
        probs = []
        for sj in parts:
            pj = jnp.exp(sj - m_new)
            l_new = l_new + jnp.sum(pj, axis=0, keepdims=True)
            probs.append(pj.astype(BF16))
        m_ref[...] = m_new
        l_ref[...] = l_new
        acc_ref[...] = alpha * acc_ref[...] + _dot(vst_ref[kt], jnp.concatenate(probs, axis=0))

    scores_into(s0_ref, list_ref[0])

    def pipelined(i, carry):
        scores_into(s1_ref, list_ref[2 * i + 1])
        sel_tile(s0_ref, list_ref[2 * i], False)

        @pl.when(2 * i + 1 < n_act)
        def _():
            scores_into(s0_ref, list_ref[2 * i + 2])
            sel_tile(s1_ref, list_ref[2 * i + 1], False)
        return carry

    lax.fori_loop(0, (n_act + 1) // 2, pipelined, 0)

    @pl.when((n_act & 1) == 0)
    def _():
        sel_tile(s0_ref, n_tiles - 1, True)

    @pl.when((n_act & 1) == 1)
    def _():
        sel_tile(s1_ref, n_tiles - 1, True)

    o_s = acc_ref[...] / l_ref[...]

    def gate(branch):
        return jnp.concatenate([gt_ref[branch * GROUP + r:branch * GROUP + r + 1, :] for r in range(GROUP)],
                               axis=1)
    o = gate(0) * o_c + gate(1) * o_s + gate(2) * o_w
    o_ref[...] = jnp.concatenate([o[:, r * Q_BLOCK:(r + 1) * Q_BLOCK].T for r in range(GROUP)],
                                 axis=1).astype(BF16)


def _attention(q, kc, vct, ovt, ks, vst, kw, vwt, gt):
    bsz, s, _ = q.shape
    nc = kc.shape[2]
    n_sel = s // SEL_BLOCK
    n_qb = s // Q_BLOCK
    n_kt = s // KEY_TILE
    n_wt = s // WIN_TILE
    n_ch = nc // CMP_CHUNK
    n_rows = GROUP * Q_BLOCK
    assert s % KEY_TILE == 0 and s >= (WINDOW + WIN_TILE) and n_sel % 8 == 0 and nc % CMP_CHUNK == 0
    return pl.pallas_call(
        _attn_kernel,
        grid=(bsz, N_KV, n_qb),
        in_specs=[pl.BlockSpec((None, Q_BLOCK, GROUP * LANES), lambda b, g, i: (b, i, g)),
                  pl.BlockSpec((None, None, nc, LANES), lambda b, g, i: (b, g, 0, 0)),
                  pl.BlockSpec((None, n_ch, HEAD_DIM, CMP_CHUNK), lambda b, g, i: (b, 0, g, 0)),
                  pl.BlockSpec((n_ch, n_sel, CMP_CHUNK), lambda b, g, i: (0, 0, 0)),
                  pl.BlockSpec((None, None, s, LANES), lambda b, g, i: (b, g, 0, 0)),
                  pl.BlockSpec((None, n_kt, HEAD_DIM, KEY_TILE), lambda b, g, i: (b, 0, g, 0)),
                  pl.BlockSpec((None, None, s, LANES), lambda b, g, i: (b, g, 0, 0)),
                  pl.BlockSpec((None, n_wt, HEAD_DIM, WIN_TILE), lambda b, g, i: (b, 0, g, 0)),
                  pl.BlockSpec((None, 16, Q_BLOCK), lambda b, g, i: (b, g, i))],
        out_specs=pl.BlockSpec((None, Q_BLOCK, GROUP * HEAD_DIM), lambda b, g, i: (b, i, g)),
        out_shape=jax.ShapeDtypeStruct((bsz, s, N_HEADS * HEAD_DIM), BF16),
        scratch_shapes=[pltpu.VMEM((KEY_TILE, n_rows), F32),
                        pltpu.VMEM((KEY_TILE, n_rows), F32),
                        pltpu.VMEM((n_sel, n_rows), F32),
                        pltpu.VMEM((1, n_rows), F32),
                        pltpu.VMEM((1, n_rows), F32),
                        pltpu.VMEM((HEAD_DIM, n_rows), F32),
                        pltpu.SMEM((n_kt,), jnp.int32),
                        pltpu.SMEM((n_kt + 1,), jnp.int32)],
        compiler_params=_params(3),
        name="nsa_attention",
    )(q, kc, vct, ovt, ks, vst, kw, vwt, gt)


def _merge_kernel(x_ref, g_ref, sc_ref, sh_ref, gate_ref, vconv_ref, onsa_ref, ylru_ref,
                  wm_ref, bm_ref, wc_ref, wn_ref, wl_ref, wo_ref, o_ref):
    d = x_ref.shape[1]
    x = x_ref[...]
    hb = _mod_norm(x, g_ref[...], sc_ref[...], sh_ref[...]).astype(BF16)
    merged = None
    branches = ((vconv_ref, wc_ref), (onsa_ref, wn_ref), (ylru_ref, wl_ref))
    for k, (y_ref, w_ref) in enumerate(branches):
        gm = jax.nn.sigmoid(_dot(hb, wm_ref[:, k * d:(k + 1) * d]) + bm_ref[:, k * d:(k + 1) * d])
        term = gm * _dot(y_ref[...], w_ref[...])
        merged = term if merged is None else merged + term
    o_ref[...] = x + gate_ref[...] * _dot(merged.astype(BF16), wo_ref[...])


def _merge(x, mod, layer, norm_g, vconv, onsa, ylru, p):
    bsz, s, d = x.shape
    tm = TOKEN_TILE
    c = CONV_DIM
    row = lambda k: pl.BlockSpec((None, None, None, 1, d), lambda b, i: (layer, b, k, 0, 0))
    full = lambda shape: pl.BlockSpec(shape, lambda b, i: (0,) * len(shape))
    tok = lambda w: pl.BlockSpec((None, tm, w), lambda b, i: (b, i, 0))
    return pl.pallas_call(
        _merge_kernel,
        grid=(bsz, s // tm),
        in_specs=[tok(d), full((1, d)), row(1), row(0), row(2), tok(c), tok(c), tok(c),
                  full((d, N_BRANCH * d)), full((1, N_BRANCH * d)),
                  full((c, d)), full((c, d)), full((c, d)), full((d, d))],
        out_specs=tok(d),
        out_shape=jax.ShapeDtypeStruct((bsz, s, d), F32),
        compiler_params=_params(2),
        name="merge",
    )(x, norm_g, mod, mod, mod, vconv, onsa, ylru, p["w_merge"], p["b_merge"],
      p["conv_w_out"], p["nsa_w_out"], p["lru_w_out"], p["w_out"])


def _ffn_kernel(x_ref, g_ref, sc_ref, sh_ref, gate_ref, wa_ref, wb_ref, wo_ref, fin_ref, o_ref,
                *, chunk, final):
    x = x_ref[...]
    hb = _mod_norm(x, g_ref[...], sc_ref[...], sh_ref[...]).astype(BF16)
    ffn = wa_ref.shape[1]
    acc = None
    for c0 in range(0, ffn, chunk):
        a = _dot(hb, wa_ref[:, c0:c0 + chunk])
        b = _dot(hb, wb_ref[:, c0:c0 + chunk])
        part = _dot((_silu(a) * b).astype(BF16), wo_ref[c0:c0 + chunk, :])
        acc = part if acc is None else acc + part
    y = x + gate_ref[...] * acc
    if final:
        y = (y * lax.rsqrt(jnp.mean(y * y, axis=-1, keepdims=True) + EPS)) * fin_ref[...]
    o_ref[...] = y


def _ffn(x, mod, layer, norm_g, p, final_norm, final):
    bsz, s, d = x.shape
    tm = TOKEN_TILE
    ffn = p["w_ffn_a"].shape[1]
    chunk = 256
    assert ffn % chunk == 0
    row = lambda k: pl.BlockSpec((None, None, None, 1, d), lambda b, i: (layer, b, k, 0, 0))
    full = lambda shape: pl.BlockSpec(shape, lambda b, i: (0,) * len(shape))
    tok = pl.BlockSpec((None, tm, d), lambda b, i: (b, i, 0))
    return pl.pallas_call(
        functools.partial(_ffn_kernel, chunk=chunk, final=final),
        grid=(bsz, s // tm),
        in_specs=[tok, full((1, d)), row(4), row(3), row(5),
                  full((d, ffn)), full((d, ffn)), full((ffn, d)), full((1, d))],
        out_specs=tok,
        out_shape=jax.ShapeDtypeStruct((bsz, s, d), F32),
        compiler_params=_params(2),
        name="ffn",
    )(x, norm_g, mod, mod, mod, p["w_ffn_a"], p["w_ffn_b"], p["w_ffn_out"], final_norm)


def _pack_input_projection(w_in):
    d = w_in.shape[0]
    q0, kv0, gate0, lru0 = 1024, 1536, 2304, 2328

    def head_slots(w, n):
        w = w.reshape(d, n, HEAD_DIM)
        return jnp.pad(w, ((0, 0), (0, 0), (0, LANES - HEAD_DIM))).reshape(d, n * LANES)

    kv = lambda i: w_in[:, kv0 + i * 128:kv0 + (i + 1) * 128]
    gates = w_in[:, gate0:gate0 + N_KV * GROUP * 3].reshape(d, N_KV, GROUP, 3)
    gates = jnp.pad(gates.transpose(0, 1, 3, 2).reshape(d, N_KV, 3 * GROUP), ((0, 0), (0, 0), (0, 4)))
    gates = jnp.pad(gates.reshape(d, N_KV * 16), ((0, 0), (0, LANES - N_KV * 16)))
    return jnp.concatenate([
        w_in[:, 0:q0],
        head_slots(w_in[:, q0:kv0] * HEAD_DIM ** -0.5, N_HEADS),
        kv(0), kv(1),
        head_slots(kv(2), N_KV), head_slots(kv(4), N_KV),
        kv(3), kv(5), gates,
        w_in[:, lru0:lru0 + 2 * LRU_DIM]], axis=1)


def _q_slope_row():
    row = np.zeros((1, N_HEADS * LANES), np.float32)
    for h in range(N_HEADS):
        slope = 2.0 ** (-8.0 * (h + 1) / N_HEADS)
        row[0, h * LANES + HEAD_DIM] = slope * 128.0
        row[0, h * LANES + HEAD_DIM + 1] = slope
    return row


def _overlap_matrix(s):
    nc = s // CMP_STRIDE
    n_sel = s // SEL_BLOCK
    cs = np.arange(nc)[:, None] * CMP_STRIDE
    ss = np.arange(n_sel)[None, :] * SEL_BLOCK
    ov = np.clip(np.minimum(cs + CMP_LEN, ss + SEL_BLOCK) - np.maximum(cs, ss), 0, None) / CMP_LEN
    ovt = ov.T.astype(np.float32).reshape(n_sel, nc // CMP_CHUNK, CMP_CHUNK)
    return np.ascontiguousarray(ovt.transpose(1, 0, 2))


def _block_diag(w):
    heads, n, _ = w.shape
    out = jnp.zeros((heads * n, heads * n), w.dtype)
    for h in range(heads):
        out = out.at[h * n:(h + 1) * n, h * n:(h + 1) * n].set(w[h])
    return out


def _layer_params(l, a):
    w_pack = _pack_input_projection(a["w_in"][l])
    assert w_pack.shape[1] == COL_END
    row = lambda v: v[l].reshape(1, -1)

    def chunk_weights(half):
        wk = a["nsa_w_ck1"][l].reshape(CMP_LEN, HEAD_DIM, CMP_HIDDEN)[half * CMP_STRIDE:(half + 1) * CMP_STRIDE]
        wv = a["nsa_w_cv1"][l].reshape(CMP_LEN, HEAD_DIM, CMP_HIDDEN)[half * CMP_STRIDE:(half + 1) * CMP_STRIDE]
        w = jnp.zeros((CMP_STRIDE, 4, HEAD_DIM, 4, CMP_HIDDEN), F32)
        for slot, src_w in enumerate((wk, wk, wv, wv)):
            w = w.at[:, slot, :, slot, :].set(src_w)
        return w.reshape(CMP_STRIDE * 256, 4 * CMP_HIDDEN).astype(BF16)

    w2 = jnp.zeros((4 * CMP_HIDDEN, 384), F32)
    w2 = w2.at[0:128, 0:64].set(a["nsa_w_ck2"][l]).at[128:256, 128:192].set(a["nsa_w_ck2"][l])
    w2 = w2.at[256:384, 256:320].set(a["nsa_w_cv2"][l]).at[384:512, 320:384].set(a["nsa_w_cv2"][l])
    pe_rows = lambda pe: jnp.zeros((8, CMP_LEN * HEAD_DIM), F32).at[0].set(pe.reshape(-1))
    ffn = a["w_ffn_out"].shape[1]
    return {
        "w_pack": w_pack.astype(BF16),
        "conv_w_dw": a["conv_w_dw"][l], "conv_b_dw": row(a["conv_b_dw"]),
        "conv_ln_g": row(a["conv_ln_g"]), "conv_ln_b": row(a["conv_ln_b"]),
        "lru_w_conv": a["lru_w_conv"][l], "lru_b_conv": row(a["lru_b_conv"]),
        "lru_wa_bd": _block_diag(a["lru_w_a"][l]).astype(BF16), "lru_b_a": row(a["lru_b_a"]),
        "lru_wx_bd": _block_diag(a["lru_w_x"][l]).astype(BF16), "lru_b_x": row(a["lru_b_x"]),
        "lru_lam": row(a["lru_lam"]),
        "cmp_w_lo": chunk_weights(0), "cmp_w_hi": chunk_weights(1),
        "cmp_pe_k": pe_rows(a["nsa_pe_k"][l]), "cmp_pe_v": pe_rows(a["nsa_pe_v"][l]),
        "nsa_w_ck1": a["nsa_w_ck1"][l], "nsa_w_cv1": a["nsa_w_cv1"][l],
        "cmp_w2": w2.astype(BF16),
        "w_merge": a["w_merge"][l].astype(BF16), "b_merge": row(a["b_merge"]),
        "conv_w_out": a["conv_w_out"][l].astype(BF16), "nsa_w_out": a["nsa_w_out"][l].astype(BF16),
        "lru_w_out": a["lru_w_out"][l].astype(BF16), "w_out": a["w_out"][l].astype(BF16),
        "w_ffn_a": a["w_ffn_in"][l][:, :ffn].astype(BF16), "w_ffn_b": a["w_ffn_in"][l][:, ffn:].astype(BF16),
        "w_ffn_out": a["w_ffn_out"][l].astype(BF16),
    }


def kernel(x, c, w_mod, b_mod, norm_mix, norm_ffn, w_in, conv_w_dw, conv_b_dw, conv_ln_g, conv_ln_b,
           conv_w_out, nsa_pe_k, nsa_w_ck1, nsa_w_ck2, nsa_pe_v, nsa_w_cv1, nsa_w_cv2, nsa_w_out,
           lru_w_conv, lru_b_conv, lru_w_a, lru_b_a, lru_w_x, lru_b_x, lru_lam, lru_w_out,
           w_merge, b_merge, w_out, w_ffn_in, w_ffn_out, final_norm):
    a = dict(w_in=w_in, conv_w_dw=conv_w_dw, conv_b_dw=conv_b_dw, conv_ln_g=conv_ln_g, conv_ln_b=conv_ln_b,
             conv_w_out=conv_w_out, nsa_pe_k=nsa_pe_k, nsa_w_ck1=nsa_w_ck1, nsa_w_ck2=nsa_w_ck2,
             nsa_pe_v=nsa_pe_v, nsa_w_cv1=nsa_w_cv1, nsa_w_cv2=nsa_w_cv2, nsa_w_out=nsa_w_out,
             lru_w_conv=lru_w_conv, lru_b_conv=lru_b_conv, lru_w_a=lru_w_a, lru_b_a=lru_b_a,
             lru_w_x=lru_w_x, lru_b_x=lru_b_x, lru_lam=lru_lam, lru_w_out=lru_w_out,
             w_merge=w_merge, b_merge=b_merge, w_out=w_out, w_ffn_in=w_ffn_in, w_ffn_out=w_ffn_out)
    depth = w_in.shape[0]
    s = x.shape[1]
    mod = _modulation(c, w_mod, b_mod)
    qaug = jnp.asarray(_q_slope_row())
    ovt = jnp.asarray(_overlap_matrix(s)).astype(BF16)
    fin = final_norm.reshape(1, -1)
    for l in range(depth):
        p = _layer_params(l, a)
        g_mix = norm_mix[l].reshape(1, -1)
        g_ffn = norm_ffn[l].reshape(1, -1)
        uconv, ulru, q, kvc, ks, kw, vst, vwt, gt = _premix(x, mod, l, g_mix, p["w_pack"], qaug)
        vconv, ylru = _branches(uconv, ulru, p)
        kc, vct = _compress(kvc, p)
        onsa = _attention(q, kc, vct, ovt, ks, vst, kw, vwt, gt)
        x = _merge(x, mod, l, g_mix, vconv, onsa, ylru, p)
        x = _ffn(x, mod, l, g_ffn, p, fin, final=(l == depth - 1))
    return x
```

```python
import functools

import numpy as np
import jax
import jax.numpy as jnp
from jax import lax
from jax.experimental import pallas as pl
from jax.experimental.pallas import tpu as pltpu

F32 = jnp.float32
BF16 = jnp.bfloat16

EPS = 1e-6
CONV_DIM = 512
CONV_WIDTH = 31
N_HEADS = 8
N_KV = 2
GROUP = N_HEADS // N_KV
HEAD_DIM = 64
CMP_LEN = 32
CMP_STRIDE = 16
CMP_HIDDEN = 128
SEL_BLOCK = 64
SEL_TOPK = 16
WINDOW = 512
Q_BLOCK = 256
FORCE_SCORE = 1e4
LRU_DIM = 512
LRU_HEADS = 8
LRU_CONV_WIDTH = 4
LRU_C = 8.0
N_BRANCH = 3

LANES = 128
NEG = -1e30
LOG2E = 1.4426950408889634
SLOPE_PIECES = 3
KEY_TILE = 512
WIN_TILE = 128
CMP_CHUNK = 256
TOKEN_TILE = 512
BRANCH_TILE = 256
CONV_HALO = 32
LRU_HALO = 8
VMEM_LIMIT = 56 * 1024 * 1024

COL_CONV = 0
COL_Q = 1024
COL_KVC = 2048
COL_K = 2304
COL_V = 2816
COL_LRU = 3200
COL_END = 4224


def _params(n_grid):
    return pltpu.CompilerParams(dimension_semantics=("arbitrary",) * n_grid,
                                vmem_limit_bytes=VMEM_LIMIT)


def _dot(a, b):
    return jnp.dot(a, b, preferred_element_type=F32)


def _dot_nt(a, b):
    return lax.dot_general(a, b, (((1,), (1,)), ((), ())), preferred_element_type=F32)


def _split_bf16(a):
    hi = a.astype(BF16)
    lo = (a - hi.astype(F32)).astype(BF16)
    return hi, lo


def _dot3(a, b):
    a_hi, a_lo = _split_bf16(a)
    b_hi, b_lo = _split_bf16(b)
    return _dot(a_hi, b_hi) + (_dot(a_hi, b_lo) + _dot(a_lo, b_hi))


def _gelu(x):
    return 0.5 * x * (1.0 + jnp.tanh(0.7978845608028654 * (x + 0.044715 * (x * x * x))))


def _sigmoid(x):
    return 0.5 * jnp.tanh(0.5 * x) + 0.5


def _silu(x):
    return x * _sigmoid(x)


def _fold8(x, op):
    groups = [x[i:i + 8, :] for i in range(0, x.shape[0], 8)]
    while len(groups) > 1:
        groups = [op(groups[i], groups[i + 1]) if i + 1 < len(groups) else groups[i]
                  for i in range(0, len(groups), 2)]
    return groups[0]


def _position_lanes(pos, lane):
    hi = (pos >> 7).astype(F32)
    lo = (pos & 127).astype(F32)
    off = lane - HEAD_DIM
    return jnp.where((off >= 0) & (off < SLOPE_PIECES), hi,
                     jnp.where((off >= SLOPE_PIECES) & (off < 2 * SLOPE_PIECES), lo, 0.0))


def _mod_norm(x, g, sc, sh):
    y = x * lax.rsqrt(jnp.mean(x * x, axis=-1, keepdims=True) + EPS)
    return (y * g) * (1.0 + sc) + sh


def _mod_kernel(c_ref, w_ref, b_ref, o_ref):
    o_ref[...] = _dot3(c_ref[...], w_ref[...]) + b_ref[...]


def _modulation(c, w_mod, b_mod):
    depth, d, n = w_mod.shape
    bsz = c.shape[0]
    rows = 8
    c_pad = jnp.zeros((rows, d), F32).at[:bsz].set(c)
    out = pl.pallas_call(
        _mod_kernel,
        grid=(depth, n // d),
        in_specs=[pl.BlockSpec((rows, d), lambda l, j: (0, 0)),
                  pl.BlockSpec((None, d, d), lambda l, j: (l, 0, j)),
                  pl.BlockSpec((None, 1, d), lambda l, j: (l, 0, j))],
        out_specs=pl.BlockSpec((None, rows, d), lambda l, j: (l, 0, j)),
        out_shape=jax.ShapeDtypeStruct((depth, rows, n), F32),
        compiler_params=_params(2),
        name="modulation",
    )(c_pad, w_mod, b_mod.reshape(depth, 1, n))
    return out[:, :bsz].reshape(depth, bsz, n // d, 1, d)


def _premix_kernel(x_ref, g_ref, sc_ref, sh_ref, w_ref, qaug_ref,
                   uconv_ref, ulru_ref, q_ref, kvc_ref, ks_ref, kw_ref, vst_ref, vwt_ref, gt_ref):
    i = pl.program_id(1)
    tm = x_ref.shape[0]
    hb = _mod_norm(x_ref[...], g_ref[...], sc_ref[...], sh_ref[...]).astype(BF16)

    def proj(a, b):
        return _dot(hb, w_ref[:, a:b])

    uconv_ref[...] = proj(COL_CONV, COL_Q)
    ulru_ref[...] = proj(COL_LRU, COL_END)
    q_ref[...] = (proj(COL_Q, COL_KVC) + qaug_ref[...]).astype(BF16)
    kvc_ref[...] = proj(COL_KVC, COL_K).astype(BF16)
    pos = i * tm + lax.broadcasted_iota(jnp.int32, (tm, LANES), 0)
    lane = lax.broadcasted_iota(jnp.int32, (tm, LANES), 1)
    kaug = _position_lanes(pos, lane)
    uk = proj(COL_K, COL_V)
    ks_ref[0] = (uk[:, 0:128] + kaug).astype(BF16)
    ks_ref[1] = (uk[:, 128:256] + kaug).astype(BF16)
    kw_ref[0] = (uk[:, 256:384] + kaug).astype(BF16)
    kw_ref[1] = (uk[:, 384:512] + kaug).astype(BF16)
    uv = proj(COL_V, COL_LRU)
    vst_ref[...] = uv[:, 0:128].T.astype(BF16)
    vwt = uv[:, 128:256].T.astype(BF16)
    for t in range(tm // WIN_TILE):
        vwt_ref[t] = vwt[:, t * WIN_TILE:(t + 1) * WIN_TILE]
    gt = _sigmoid(uv[:, 256:384]).T
    gt_ref[...] = gt[0:32, :]


def _premix(x, mod, layer, norm_g, w_pack, qaug):
    bsz, s, d = x.shape
    tm = TOKEN_TILE
    assert tm == KEY_TILE and s % tm == 0
    n_t = s // tm
    row = lambda k: pl.BlockSpec((None, None, None, 1, d), lambda b, i: (layer, b, k, 0, 0))
    out_shapes = (
        jax.ShapeDtypeStruct((bsz, s, 1024), F32),
        jax.ShapeDtypeStruct((bsz, s, 1024), F32),
        jax.ShapeDtypeStruct((bsz, s, 1024), BF16),
        jax.ShapeDtypeStruct((bsz, s, 256), BF16),
        jax.ShapeDtypeStruct((bsz, N_KV, s, LANES), BF16),
        jax.ShapeDtypeStruct((bsz, N_KV, s, LANES), BF16),
        jax.ShapeDtypeStruct((bsz, n_t, LANES, KEY_TILE), BF16),
        jax.ShapeDtypeStruct((bsz, s // WIN_TILE, LANES, WIN_TILE), BF16),
        jax.ShapeDtypeStruct((bsz, 32, s), F32),
    )
    out_specs = (
        pl.BlockSpec((None, tm, 1024), lambda b, i: (b, i, 0)),
        pl.BlockSpec((None, tm, 1024), lambda b, i: (b, i, 0)),
        pl.BlockSpec((None, tm, 1024), lambda b, i: (b, i, 0)),
        pl.BlockSpec((None, tm, 256), lambda b, i: (b, i, 0)),
        pl.BlockSpec((None, N_KV, tm, LANES), lambda b, i: (b, 0, i, 0)),
        pl.BlockSpec((None, N_KV, tm, LANES), lambda b, i: (b, 0, i, 0)),
        pl.BlockSpec((None, None, LANES, KEY_TILE), lambda b, i: (b, i, 0, 0)),
        pl.BlockSpec((None, tm // WIN_TILE, LANES, WIN_TILE), lambda b, i: (b, i, 0, 0)),
        pl.BlockSpec((None, 32, tm), lambda b, i: (b, 0, i)),
    )
    return pl.pallas_call(
        _premix_kernel,
        grid=(bsz, n_t),
        in_specs=[pl.BlockSpec((None, tm, d), lambda b, i: (b, i, 0)),
                  pl.BlockSpec((1, d), lambda b, i: (0, 0)),
                  row(1), row(0),
                  pl.BlockSpec((d, COL_END), lambda b, i: (0, 0)),
                  pl.BlockSpec((1, 1024), lambda b, i: (0, 0))],
        out_specs=out_specs,
        out_shape=out_shapes,
        compiler_params=_params(2),
        name="premix",
    )(x, norm_g, mod, mod, w_pack, qaug)


def _branch_kernel(uconv_ref, ulru_ref, wdw_ref, bdw_ref, lng_ref, lnb_ref,
                   wc4_ref, bc4_ref, wa_ref, ba_ref, wx_ref, bx_ref, lam_ref,
                   vconv_ref, ylru_ref, vext_ref, vsh_ref, xext_ref, hcar_ref):
    ts = uconv_ref.shape[0]
    c = CONV_DIM

    @pl.when(pl.program_id(1) == 0)
    def _():
        vext_ref[0:CONV_HALO, :] = jnp.zeros((CONV_HALO, c), F32)
        xext_ref[0:LRU_HALO, :] = jnp.zeros((LRU_HALO, c), F32)
        hcar_ref[...] = jnp.zeros(hcar_ref.shape, F32)

    v = uconv_ref[:, 0:c] * _sigmoid(uconv_ref[:, c:2 * c])
    vext_ref[CONV_HALO:CONV_HALO + ts, :] = v
    span = ts + CONV_HALO - 8
    for phase in range(1, 8):
        vsh_ref[phase - 1] = vext_ref[pl.ds(phase, span), :]
    acc = jnp.broadcast_to(bdw_ref[...], (ts, c))
    first = CONV_HALO - (CONV_WIDTH - 1)
    for j in range(CONV_WIDTH):
        base, phase = (first + j) // 8 * 8, (first + j) % 8
        rows = vext_ref[base:base + ts, :] if phase == 0 else vsh_ref[phase - 1, base:base + ts, :]
        acc = acc + wdw_ref[j:j + 1, :] * rows
    vext_ref[0:CONV_HALO, :] = v[ts - CONV_HALO:ts, :]
    mu = jnp.mean(acc, axis=-1, keepdims=True)
    cen = acc - mu
    var = jnp.mean(cen * cen, axis=-1, keepdims=True)
    ln = (cen * lax.rsqrt(var + EPS)) * lng_ref[...] + lnb_ref[...]
    vconv_ref[...] = _silu(ln).astype(BF16)

    ux = ulru_ref[:, 0:c]
    xext_ref[LRU_HALO:LRU_HALO + ts, :] = ux
    xr = jnp.broadcast_to(bc4_ref[...], (ts, c))
    first = LRU_HALO - (LRU_CONV_WIDTH - 1)
    for j in range(LRU_CONV_WIDTH):
        xr = xr + wc4_ref[j:j + 1, :] * xext_ref[pl.ds(first + j, ts), :]
    xext_ref[0:LRU_HALO, :] = ux[ts - LRU_HALO:ts, :]
    xb = xr.astype(BF16)
    r = _sigmoid(_dot(xb, wa_ref[...]) + ba_ref[...])
    gate_i = _sigmoid(_dot(xb, wx_ref[...]) + bx_ref[...])
    z = -lam_ref[...]
    softplus = jnp.maximum(z, 0.0) + jnp.log(1.0 + jnp.exp(-jnp.abs(z)))
    log_a = (-LRU_C * r) * softplus
    a = jnp.exp(log_a)
    b = jnp.sqrt(1.0 - a * a) * (gate_i * xr)
    rows = lax.broadcasted_iota(jnp.int32, (ts, c), 0)
    shift = 1
    while shift < ts:
        a_prev = pltpu.roll(a, shift, axis=0)
        b_prev = pltpu.roll(b, shift, axis=0)
        live = rows >= shift
        b = jnp.where(live, a * b_prev + b, b)
        a = jnp.where(live, a * a_prev, a)
        shift *= 2
    h = a * hcar_ref[0:1, :] + b
    hcar_ref[...] = jnp.broadcast_to(h[ts - 1:ts, :], hcar_ref.shape)
    ylru_ref[...] = (h * _gelu(ulru_ref[:, c:2 * c])).astype(BF16)


def _branches(uconv, ulru, p):
    bsz, s, _ = uconv.shape
    ts = BRANCH_TILE
    c = CONV_DIM
    full = lambda shape: pl.BlockSpec(shape, lambda b, i: (0,) * len(shape))
    tok = lambda w: pl.BlockSpec((None, ts, w), lambda b, i: (b, i, 0))
    return pl.pallas_call(
        _branch_kernel,
        grid=(bsz, s // ts),
        in_specs=[tok(2 * c), tok(2 * c),
                  full((CONV_WIDTH, c)), full((1, c)), full((1, c)), full((1, c)),
                  full((LRU_CONV_WIDTH, c)), full((1, c)),
                  full((c, c)), full((1, c)), full((c, c)), full((1, c)), full((1, c))],
        out_specs=(tok(c), tok(c)),
        out_shape=(jax.ShapeDtypeStruct((bsz, s, c), BF16), jax.ShapeDtypeStruct((bsz, s, c), BF16)),
        scratch_shapes=[pltpu.VMEM((ts + CONV_HALO, c), F32),
                        pltpu.VMEM((7, ts + CONV_HALO - 8, c), F32),
                        pltpu.VMEM((ts + LRU_HALO, c), F32),
                        pltpu.VMEM((8, c), F32)],
        compiler_params=_params(2),
        name="branches",
    )(uconv, ulru, p["conv_w_dw"], p["conv_b_dw"], p["conv_ln_g"], p["conv_ln_b"],
      p["lru_w_conv"], p["lru_b_conv"], p["lru_wa_bd"], p["lru_b_a"], p["lru_wx_bd"], p["lru_b_x"],
      p["lru_lam"])


def _compress_kernel(x_ref, wlo_ref, whi_ref, pek_ref, pev_ref, w1k_ref, w1v_ref, w2_ref,
                     kc_ref, vct_ref):
    nc = x_ref.shape[0]
    x = x_ref[...]
    h_lo = _dot(x, wlo_ref[...])
    h_hi = _dot(x, whi_ref[...])
    h_next = pltpu.roll(h_hi, nc - 1, axis=0)
    bk = _dot3(pek_ref[...], w1k_ref[...])[0:1, :]
    bv = _dot3(pev_ref[...], w1v_ref[...])[0:1, :]
    bias = jnp.concatenate([bk, bk, bv, bv], axis=1)
    hid = _gelu(h_lo + h_next + bias).astype(BF16)
    out = _dot(hid, w2_ref[...])
    cend = lax.broadcasted_iota(jnp.int32, (nc, LANES), 0) * CMP_STRIDE + (CMP_LEN - 1)
    lane = lax.broadcasted_iota(jnp.int32, (nc, LANES), 1)
    aug = _position_lanes(cend, lane)
    kc_ref[0] = (out[:, 0:128] + aug).astype(BF16)
    kc_ref[1] = (out[:, 128:256] + aug).astype(BF16)
    vct = out[:, 256:384].T.astype(BF16)
    for t in range(nc // CMP_CHUNK):
        vct_ref[t] = vct[:, t * CMP_CHUNK:(t + 1) * CMP_CHUNK]


def _compress(kvc, p):
    bsz, s, _ = kvc.shape
    nc = s // CMP_STRIDE
    width = CMP_STRIDE * 256
    x = kvc.reshape(bsz, nc, width)
    full = lambda shape: pl.BlockSpec(shape, lambda b: (0,) * len(shape))
    return pl.pallas_call(
        _compress_kernel,
        grid=(bsz,),
        in_specs=[pl.BlockSpec((None, nc, width), lambda b: (b, 0, 0)),
                  full((width, 512)), full((width, 512)),
                  full((8, CMP_LEN * HEAD_DIM)), full((8, CMP_LEN * HEAD_DIM)),
                  full((CMP_LEN * HEAD_DIM, CMP_HIDDEN)), full((CMP_LEN * HEAD_DIM, CMP_HIDDEN)),
                  full((512, 384))],
        out_specs=(pl.BlockSpec((None, N_KV, nc, LANES), lambda b: (b, 0, 0, 0)),
                   pl.BlockSpec((None, nc // CMP_CHUNK, LANES, CMP_CHUNK), lambda b: (b, 0, 0, 0))),
        out_shape=(jax.ShapeDtypeStruct((bsz, N_KV, nc, LANES), BF16),
                   jax.ShapeDtypeStruct((bsz, nc // CMP_CHUNK, LANES, CMP_CHUNK), BF16)),
        compiler_params=_params(1),
        name="compress",
    )(x, p["cmp_w_lo"], p["cmp_w_hi"], p["cmp_pe_k"], p["cmp_pe_v"], p["nsa_w_ck1"], p["nsa_w_cv1"],
      p["cmp_w2"])


def _attn_kernel(q_ref, kc_ref, vct_ref, ovt_ref, ks_ref, vst_ref, kw_ref, vwt_ref, gt_ref,
                 o_ref, s0_ref, s1_ref, w_ref, selneg_ref, m_ref, l_ref, acc_ref, flag_ref, list_ref):
    qb = pl.program_id(2)
    q0 = qb * Q_BLOCK
    n_sel = ovt_ref.shape[1]
    n_rows = GROUP * Q_BLOCK
    qg = jnp.concatenate([q_ref[:, r * LANES:(r + 1) * LANES] for r in range(GROUP)], axis=0)
    qpos = q0 + (lax.broadcasted_iota(jnp.int32, (1, n_rows), 1) & (Q_BLOCK - 1))

    n_chunks = ((q0 + Q_BLOCK - CMP_LEN) // CMP_STRIDE + CMP_CHUNK) // CMP_CHUNK
    cend = lax.broadcasted_iota(jnp.int32, (CMP_CHUNK, n_rows), 0) * CMP_STRIDE + (CMP_LEN - 1)

    def cmp_branch(n_ch):
        def fn():
            s_all = _dot_nt(kc_ref[0:n_ch * CMP_CHUNK, :], qg)
            scores = []
            m_c = None
            for ci in range(n_ch):
                s = s_all[ci * CMP_CHUNK:(ci + 1) * CMP_CHUNK, :]
                if ci >= n_ch - 2:
                    s = jnp.where(cend + ci * CMP_CHUNK * CMP_STRIDE <= qpos, s, NEG)
                scores.append(s)
                mi = jnp.max(s, axis=0, keepdims=True)
                m_c = mi if m_c is None else jnp.maximum(m_c, mi)
            m_c = jnp.where(m_c > 0.5 * NEG, m_c, 0.0)
            probs = [jnp.exp2(s - m_c) for s in scores]
            den = probs[0].sum(axis=0, keepdims=True)
            for p in probs[1:]:
                den = den + jnp.sum(p, axis=0, keepdims=True)
            inv_den = 1.0 / jnp.where(den > 0, den, 1.0)
            o_c = None
            imp = None
            for ci, p in enumerate(probs):
                p = p * inv_den
                part = _dot(vct_ref[ci], p.astype(BF16))
                o_c = part if o_c is None else o_c + part
                p_sum = p[:, 0:Q_BLOCK]
                for r in range(1, GROUP):
                    p_sum = p_sum + p[:, r * Q_BLOCK:(r + 1) * Q_BLOCK]
                p_hi, p_lo = _split_bf16(p_sum)
                part = _dot(ovt_ref[ci], p_hi) + _dot(ovt_ref[ci], p_lo)
                imp = part if imp is None else imp + part
            return o_c, imp
        return fn

    o_c, imp = lax.switch(n_chunks - 1, [cmp_branch(n) for n in range(1, kc_ref.shape[0] // CMP_CHUNK + 1)])

    blk = lax.broadcasted_iota(jnp.int32, (n_sel, Q_BLOCK), 0)
    qp = q0 + lax.broadcasted_iota(jnp.int32, (n_sel, Q_BLOCK), 1)
    qid = qp // SEL_BLOCK
    forced = (blk == 0) | (blk == qid) | (blk == qid - 1)
    valid = blk * SEL_BLOCK <= qp
    work = jnp.where(forced, -jnp.inf, jnp.where(valid, imp, -1.0))
    for _ in range(min(SEL_TOPK, n_sel) - 3):
        best = jnp.max(work, axis=0, keepdims=True)
        first = jnp.min(jnp.where(work == best, blk, n_sel), axis=0, keepdims=True)
        work = jnp.where(blk == first, -jnp.inf, work)

    def scores_into(buf_ref, kt):
        k0 = pl.multiple_of(kt * KEY_TILE, KEY_TILE)
        buf_ref[...] = _dot_nt(ks_ref[pl.ds(k0, KEY_TILE), :], qg)

    scores_into(s0_ref, 0)

    n_win = (WINDOW + Q_BLOCK) // WIN_TILE
    t0 = jnp.maximum(q0 - WINDOW, 0) // WIN_TILE
    kstart = pl.multiple_of(t0 * WIN_TILE, WIN_TILE)
    w_ref[...] = _dot_nt(kw_ref[pl.ds(kstart, n_win * WIN_TILE), :], qg)
    wrow = lax.broadcasted_iota(jnp.int32, (WIN_TILE, n_rows), 0)
    qrel = qpos - kstart

    def win_scores(t):
        dist = qrel - (t * WIN_TILE + wrow)
        visible = (dist >= 0) & (dist < WINDOW) if t * WIN_TILE < Q_BLOCK else dist >= 0
        return jnp.where(visible, w_ref[t * WIN_TILE:(t + 1) * WIN_TILE, :], NEG)

    top = None
    for t in range(n_win):
        mt = _fold8(win_scores(t), jnp.maximum)
        top = mt if top is None else jnp.maximum(top, mt)
    m_w = jnp.max(top, axis=0, keepdims=True)
    total = None
    o_w = None
    for t in range(n_win):
        pt = jnp.exp2(win_scores(t) - m_w)
        lt = _fold8(pt, jnp.add)
        total = lt if total is None else total + lt
        part = _dot(vwt_ref[t0 + t], pt.astype(BF16))
        o_w = part if o_w is None else o_w + part
    o_w = o_w / jnp.sum(total, axis=0, keepdims=True)

    chosen = jnp.where(work == -jnp.inf, 1.0, 0.0)
    selneg = jnp.where(work == -jnp.inf, 0.0, NEG)
    selneg_ref[...] = jnp.concatenate([selneg] * GROUP, axis=1)
    blocks_per_tile = KEY_TILE // SEL_BLOCK
    any_q = jnp.max(chosen, axis=1, keepdims=True)
    for t in range(n_sel // blocks_per_tile):
        tile_any = jnp.max(any_q[t * blocks_per_tile:(t + 1) * blocks_per_tile, :])
        flag_ref[t] = (tile_any > 0).astype(jnp.int32)

    n_tiles = q0 // KEY_TILE + 1
    krow = lax.broadcasted_iota(jnp.int32, (SEL_BLOCK, n_rows), 0)
    m_ref[...] = jnp.full((1, n_rows), NEG, F32)
    l_ref[...] = jnp.zeros((1, n_rows), F32)
    acc_ref[...] = jnp.zeros((HEAD_DIM, n_rows), F32)

    def compact(kt, n):
        @pl.when(flag_ref[kt] > 0)
        def _():
            list_ref[n] = kt
        return n + flag_ref[kt]

    n_act = lax.fori_loop(0, n_tiles - 1, compact, 0)
    list_ref[n_act] = n_tiles - 1

    def sel_tile(buf_ref, kt, causal):
        sel_rows = selneg_ref[pl.ds(pl.multiple_of(kt * blocks_per_tile, blocks_per_tile),
                                    blocks_per_tile), :]

        def block_scores(j):
            sj = buf_ref[j * SEL_BLOCK:(j + 1) * SEL_BLOCK, :]
            if causal:
                sj = jnp.where(kt * KEY_TILE + j * SEL_BLOCK + krow <= qpos, sj, NEG)
            return sj

        m_old = m_ref[...]
        top = None
        for j in range(blocks_per_tile):
            mj = _fold8(block_scores(j), jnp.maximum) + sel_rows[j:j + 1, :]
            top = mj if top is None else jnp.maximum(top, mj)
        m_new = jnp.maximum(m_old, jnp.max(top, axis=0, keepdims=True))
        alpha = jnp.exp2(m_old - m_new)
        total = None
        probs = []
        for j in range(blocks_per_tile):
            pj = jnp.exp2(block_scores(j) - (m_new - sel_rows[j:j + 1, :]))
            lj = _fold8(pj, jnp.add)
            total = lj if total is None else total + lj
            probs.append(pj.astype(BF16))
        m_ref[...] = m_new
        l_ref[...] = alpha * l_ref[...] + jnp.sum(total, axis=0, keepdims=True)
        acc_ref[...] = alpha * acc_ref[...] + _dot(vst_ref[kt], jnp.concatenate(probs, axis=0))

    def pipelined(i, carry):
        scores_into(s1_ref, list_ref[2 * i + 1])
        sel_tile(s0_ref, list_ref[2 * i], False)

        @pl.when(2 * i + 1 < n_act)
        def _():
            scores_into(s0_ref, list_ref[2 * i + 2])
            sel_tile(s1_ref, list_ref[2 * i + 1], False)
        return carry

    lax.fori_loop(0, (n_act + 1) // 2, pipelined, 0)

    @pl.when((n_act & 1) == 0)
    def _():
        sel_tile(s0_ref, n_tiles - 1, True)

    @pl.when((n_act & 1) == 1)
    def _():
        sel_tile(s1_ref, n_tiles - 1, True)

    o_s = acc_ref[...] / l_ref[...]

    def gate(branch):
        return jnp.concatenate([gt_ref[branch * GROUP + r:branch * GROUP + r + 1, :] for r in range(GROUP)],
                               axis=1)
    o = gate(0) * o_c + gate(1) * o_s + gate(2) * o_w
    o_ref[...] = jnp.concatenate([o[:, r * Q_BLOCK:(r + 1) * Q_BLOCK].T for r in range(GROUP)],
                                 axis=1).astype(BF16)


def _attention(q, kc, vct, ovt, ks, vst, kw, vwt, gt):
    bsz, s, _ = q.shape
    nc = kc.shape[2]
    n_sel = s // SEL_BLOCK
    n_qb = s // Q_BLOCK
    n_kt = s // KEY_TILE
    n_wt = s // WIN_TILE
    n_ch = nc // CMP_CHUNK
    n_rows = GROUP * Q_BLOCK
    assert s % KEY_TILE == 0 and s >= (WINDOW + Q_BLOCK) and n_sel % 8 == 0 and nc % CMP_CHUNK == 0
    assert KEY_TILE % Q_BLOCK == 0 and Q_BLOCK % WIN_TILE == 0
    return pl.pallas_call(
        _attn_kernel,
        grid=(bsz, N_KV, n_qb),
        in_specs=[pl.BlockSpec((None, Q_BLOCK, GROUP * LANES), lambda b, g, i: (b, i, g)),
                  pl.BlockSpec((None, None, nc, LANES), lambda b, g, i: (b, g, 0, 0)),
                  pl.BlockSpec((None, n_ch, HEAD_DIM, CMP_CHUNK), lambda b, g, i: (b, 0, g, 0)),
                  pl.BlockSpec((n_ch, n_sel, CMP_CHUNK), lambda b, g, i: (0, 0, 0)),
                  pl.BlockSpec((None, None, s, LANES), lambda b, g, i: (b, g, 0, 0)),
                  pl.BlockSpec((None, n_kt, HEAD_DIM, KEY_TILE), lambda b, g, i: (b, 0, g, 0)),
                  pl.BlockSpec((None, None, s, LANES), lambda b, g, i: (b, g, 0, 0)),
                  pl.BlockSpec((None, n_wt, HEAD_DIM, WIN_TILE), lambda b, g, i: (b, 0, g, 0)),
                  pl.BlockSpec((None, 16, Q_BLOCK), lambda b, g, i: (b, g, i))],
        out_specs=pl.BlockSpec((None, Q_BLOCK, GROUP * HEAD_DIM), lambda b, g, i: (b, i, g)),
        out_shape=jax.ShapeDtypeStruct((bsz, s, N_HEADS * HEAD_DIM), BF16),
        scratch_shapes=[pltpu.VMEM((KEY_TILE, n_rows), F32),
                        pltpu.VMEM((KEY_TILE, n_rows), F32),
                        pltpu.VMEM((WINDOW + Q_BLOCK, n_rows), F32),
                        pltpu.VMEM((n_sel, n_rows), F32),
                        pltpu.VMEM((1, n_rows), F32),
                        pltpu.VMEM((1, n_rows), F32),
                        pltpu.VMEM((HEAD_DIM, n_rows), F32),
                        pltpu.SMEM((n_kt,), jnp.int32),
                        pltpu.SMEM((n_kt + 1,), jnp.int32)],
        compiler_params=_params(3),
        name="nsa_attention",
    )(q, kc, vct, ovt, ks, vst, kw, vwt, gt)


def _merge_kernel(x_ref, g_ref, sc_ref, sh_ref, gate_ref, vconv_ref, onsa_ref, ylru_ref,
                  wm_ref, bm_ref, wc_ref, wn_ref, wl_ref, wo_ref, o_ref):
    d = x_ref.shape[1]
    x = x_ref[...]
    hb = _mod_norm(x, g_ref[...], sc_ref[...], sh_ref[...]).astype(BF16)
    merged = None
    branches = ((vconv_ref, wc_ref), (onsa_ref, wn_ref), (ylru_ref, wl_ref))
    for k, (y_ref, w_ref) in enumerate(branches):
        gm = _sigmoid(_dot(hb, wm_ref[:, k * d:(k + 1) * d]) + bm_ref[:, k * d:(k + 1) * d])
        term = gm * _dot(y_ref[...], w_ref[...])
        merged = term if merged is None else merged + term
    o_ref[...] = x + gate_ref[...] * _dot(merged.astype(BF16), wo_ref[...])


def _merge(x, mod, layer, norm_g, vconv, onsa, ylru, p):
    bsz, s, d = x.shape
    tm = TOKEN_TILE
    c = CONV_DIM
    row = lambda k: pl.BlockSpec((None, None, None, 1, d), lambda b, i: (layer, b, k, 0, 0))
    full = lambda shape: pl.BlockSpec(shape, lambda b, i: (0,) * len(shape))
    tok = lambda w: pl.BlockSpec((None, tm, w), lambda b, i: (b, i, 0))
    return pl.pallas_call(
        _merge_kernel,
        grid=(bsz, s // tm),
        in_specs=[tok(d), full((1, d)), row(1), row(0), row(2), tok(c), tok(c), tok(c),
                  full((d, N_BRANCH * d)), full((1, N_BRANCH * d)),
                  full((c, d)), full((c, d)), full((c, d)), full((d, d))],
        out_specs=tok(d),
        out_shape=jax.ShapeDtypeStruct((bsz, s, d), F32),
        compiler_params=_params(2),
        name="merge",
    )(x, norm_g, mod, mod, mod, vconv, onsa, ylru, p["w_merge"], p["b_merge"],
      p["conv_w_out"], p["nsa_w_out"], p["lru_w_out"], p["w_out"])


def _ffn_kernel(x_ref, g_ref, sc_ref, sh_ref, gate_ref, wa_ref, wb_ref, wo_ref, fin_ref, o_ref,
                *, chunk, final):
    x = x_ref[...]
    hb = _mod_norm(x, g_ref[...], sc_ref[...], sh_ref[...]).astype(BF16)
    ffn = wa_ref.shape[1]
    acc = None
    for c0 in range(0, ffn, chunk):
        a = _dot(hb, wa_ref[:, c0:c0 + chunk])
        b = _dot(hb, wb_ref[:, c0:c0 + chunk])
        part = _dot((_silu(a) * b).astype(BF16), wo_ref[c0:c0 + chunk, :])
        acc = part if acc is None else acc + part
    y = x + gate_ref[...] * acc
    if final:
        y = (y * lax.rsqrt(jnp.mean(y * y, axis=-1, keepdims=True) + EPS)) * fin_ref[...]
    o_ref[...] = y


def _ffn(x, mod, layer, norm_g, p, final_norm, final):
    bsz, s, d = x.shape
    tm = TOKEN_TILE
    ffn = p["w_ffn_a"].shape[1]
    chunk = 256
    assert ffn % chunk == 0
    row = lambda k: pl.BlockSpec((None, None, None, 1, d), lambda b, i: (layer, b, k, 0, 0))
    full = lambda shape: pl.BlockSpec(shape, lambda b, i: (0,) * len(shape))
    tok = pl.BlockSpec((None, tm, d), lambda b, i: (b, i, 0))
    return pl.pallas_call(
        functools.partial(_ffn_kernel, chunk=chunk, final=final),
        grid=(bsz, s // tm),
        in_specs=[tok, full((1, d)), row(4), row(3), row(5),
                  full((d, ffn)), full((d, ffn)), full((ffn, d)), full((1, d))],
        out_specs=tok,
        out_shape=jax.ShapeDtypeStruct((bsz, s, d), F32),
        compiler_params=_params(2),
        name="ffn",
    )(x, norm_g, mod, mod, mod, p["w_ffn_a"], p["w_ffn_b"], p["w_ffn_out"], final_norm)


def _pack_input_projection(w_in):
    d = w_in.shape[0]
    q0, kv0, gate0, lru0 = 1024, 1536, 2304, 2328

    def head_slots(w, n):
        w = w.reshape(d, n, HEAD_DIM)
        return jnp.pad(w, ((0, 0), (0, 0), (0, LANES - HEAD_DIM))).reshape(d, n * LANES)

    kv = lambda i: w_in[:, kv0 + i * 128:kv0 + (i + 1) * 128]
    gates = w_in[:, gate0:gate0 + N_KV * GROUP * 3].reshape(d, N_KV, GROUP, 3)
    gates = jnp.pad(gates.transpose(0, 1, 3, 2).reshape(d, N_KV, 3 * GROUP), ((0, 0), (0, 0), (0, 4)))
    gates = jnp.pad(gates.reshape(d, N_KV * 16), ((0, 0), (0, LANES - N_KV * 16)))
    return jnp.concatenate([
        w_in[:, 0:q0],
        head_slots(w_in[:, q0:kv0] * (HEAD_DIM ** -0.5 * LOG2E), N_HEADS),
        kv(0), kv(1),
        head_slots(kv(2), N_KV), head_slots(kv(4), N_KV),
        kv(3), kv(5), gates,
        w_in[:, lru0:lru0 + 2 * LRU_DIM]], axis=1)


def _bf16_pieces(x):
    pieces = []
    for _ in range(SLOPE_PIECES):
        piece = float(np.asarray(x, np.float32).astype(BF16).astype(np.float64))
        pieces.append(piece)
        x = x - piece
    return pieces


def _q_slope_row():
    row = np.zeros((1, N_HEADS * LANES), np.float32)
    for h in range(N_HEADS):
        slope = 2.0 ** (-8.0 * (h + 1) / N_HEADS) * LOG2E
        base = h * LANES + HEAD_DIM
        row[0, base:base + SLOPE_PIECES] = _bf16_pieces(slope * 128.0)
        row[0, base + SLOPE_PIECES:base + 2 * SLOPE_PIECES] = _bf16_pieces(slope)
    return row


def _overlap_matrix(s):
    nc = s // CMP_STRIDE
    n_sel = s // SEL_BLOCK
    cs = np.arange(nc)[:, None] * CMP_STRIDE
    ss = np.arange(n_sel)[None, :] * SEL_BLOCK
    ov = np.clip(np.minimum(cs + CMP_LEN, ss + SEL_BLOCK) - np.maximum(cs, ss), 0, None) / CMP_LEN
    ovt = ov.T.astype(np.float32).reshape(n_sel, nc // CMP_CHUNK, CMP_CHUNK)
    return np.ascontiguousarray(ovt.transpose(1, 0, 2))


def _block_diag(w):
    heads, n, _ = w.shape
    out = jnp.zeros((heads * n, heads * n), w.dtype)
    for h in range(heads):
        out = out.at[h * n:(h + 1) * n, h * n:(h + 1) * n].set(w[h])
    return out


def _layer_params(l, a):
    w_pack = _pack_input_projection(a["w_in"][l])
    assert w_pack.shape[1] == COL_END
    row = lambda v: v[l].reshape(1, -1)

    def chunk_weights(half):
        wk = a["nsa_w_ck1"][l].reshape(CMP_LEN, HEAD_DIM, CMP_HIDDEN)[half * CMP_STRIDE:(half + 1) * CMP_STRIDE]
        wv = a["nsa_w_cv1"][l].reshape(CMP_LEN, HEAD_DIM, CMP_HIDDEN)[half * CMP_STRIDE:(half + 1) * CMP_STRIDE]
        w = jnp.zeros((CMP_STRIDE, 4, HEAD_DIM, 4, CMP_HIDDEN), F32)
        for slot, src_w in enumerate((wk, wk, wv, wv)):
            w = w.at[:, slot, :, slot, :].set(src_w)
        return w.reshape(CMP_STRIDE * 256, 4 * CMP_HIDDEN).astype(BF16)

    w2 = jnp.zeros((4 * CMP_HIDDEN, 384), F32)
    w2 = w2.at[0:128, 0:64].set(a["nsa_w_ck2"][l]).at[128:256, 128:192].set(a["nsa_w_ck2"][l])
    w2 = w2.at[256:384, 256:320].set(a["nsa_w_cv2"][l]).at[384:512, 320:384].set(a["nsa_w_cv2"][l])
    pe_rows = lambda pe: jnp.zeros((8, CMP_LEN * HEAD_DIM), F32).at[0].set(pe.reshape(-1))
    ffn = a["w_ffn_out"].shape[1]
    return {
        "w_pack": w_pack.astype(BF16),
        "conv_w_dw": a["conv_w_dw"][l], "conv_b_dw": row(a["conv_b_dw"]),
        "conv_ln_g": row(a["conv_ln_g"]), "conv_ln_b": row(a["conv_ln_b"]),
        "lru_w_conv": a["lru_w_conv"][l], "lru_b_conv": row(a["lru_b_conv"]),
        "lru_wa_bd": _block_diag(a["lru_w_a"][l]).astype(BF16), "lru_b_a": row(a["lru_b_a"]),
        "lru_wx_bd": _block_diag(a["lru_w_x"][l]).astype(BF16), "lru_b_x": row(a["lru_b_x"]),
        "lru_lam": row(a["lru_lam"]),
        "cmp_w_lo": chunk_weights(0), "cmp_w_hi": chunk_weights(1),
        "cmp_pe_k": pe_rows(a["nsa_pe_k"][l]), "cmp_pe_v": pe_rows(a["nsa_pe_v"][l]),
        "nsa_w_ck1": a["nsa_w_ck1"][l], "nsa_w_cv1": a["nsa_w_cv1"][l],
        "cmp_w2": w2.astype(BF16),
        "w_merge": a["w_merge"][l].astype(BF16), "b_merge": row(a["b_merge"]),
        "conv_w_out": a["conv_w_out"][l].astype(BF16), "nsa_w_out": a["nsa_w_out"][l].astype(BF16),
        "lru_w_out": a["lru_w_out"][l].astype(BF16), "w_out": a["w_out"][l].astype(BF16),
        "w_ffn_a": a["w_ffn_in"][l][:, :ffn].astype(BF16), "w_ffn_b": a["w_ffn_in"][l][:, ffn:].astype(BF16),
        "w_ffn_out": a["w_ffn_out"][l].astype(BF16),
    }


def kernel(x, c, w_mod, b_mod, norm_mix, norm_ffn, w_in, conv_w_dw, conv_b_dw, conv_ln_g, conv_ln_b,
           conv_w_out, nsa_pe_k, nsa_w_ck1, nsa_w_ck2, nsa_pe_v, nsa_w_cv1, nsa_w_cv2, nsa_w_out,
           lru_w_conv, lru_b_conv, lru_w_a, lru_b_a, lru_w_x, lru_b_x, lru_lam, lru_w_out,
           w_merge, b_merge, w_out, w_ffn_in, w_ffn_out, final_norm):
    a = dict(w_in=w_in, conv_w_dw=conv_w_dw, conv_b_dw=conv_b_dw, conv_ln_g=conv_ln_g, conv_ln_b=conv_ln_b,
             conv_w_out=conv_w_out, nsa_pe_k=nsa_pe_k, nsa_w_ck1=nsa_w_ck1, nsa_w_ck2=nsa_w_ck2,
             nsa_pe_v=nsa_pe_v, nsa_w_cv1=nsa_w_cv1, nsa_w_cv2=nsa_w_cv2, nsa_w_out=nsa_w_out,
             lru_w_conv=lru_w_conv, lru_b_conv=lru_b_conv, lru_w_a=lru_w_a, lru_b_a=lru_b_a,
             lru_w_x=lru_w_x, lru_b_x=lru_b_x, lru_lam=lru_lam, lru_w_out=lru_w_out,
             w_merge=w_merge, b_merge=b_merge, w_out=w_out, w_ffn_in=w_ffn_in, w_ffn_out=w_ffn_out)
    depth = w_in.shape[0]
    s = x.shape[1]
    mod = _modulation(c, w_mod, b_mod)
    qaug = jnp.asarray(_q_slope_row())
    ovt = jnp.asarray(_overlap_matrix(s)).astype(BF16)
    fin = final_norm.reshape(1, -1)
    for l in range(depth):
        p = _layer_params(l, a)
        g_mix = norm_mix[l].reshape(1, -1)
        g_ffn = norm_ffn[l].reshape(1, -1)
        uconv, ulru, q, kvc, ks, kw, vst, vwt, gt = _premix(x, mod, l, g_mix, p["w_pack"], qaug)
        vconv, ylru = _branches(uconv, ulru, p)
        kc, vct = _compress(kvc, p)
        onsa = _attention(q, kc, vct, ovt, ks, vst, kw, vwt, gt)
        x = _merge(x, mod, l, g_mix, vconv, onsa, ylru, p)
        x = _ffn(x, mod, l, g_ffn, p, fin, final=(l == depth - 1))
    return x
```

```python
import functools

import numpy as np
import jax
import jax.numpy as jnp
from jax import lax
from jax.experimental import pallas as pl
from jax.experimental.pallas import tpu as pltpu

F32 = jnp.float32
BF16 = jnp.bfloat16

EPS = 1e-6
CONV_DIM = 512
CONV_WIDTH = 31
N_HEADS = 8
N_KV = 2
GROUP = N_HEADS // N_KV
HEAD_DIM = 64
CMP_LEN = 32
CMP_STRIDE = 16
CMP_HIDDEN = 128
SEL_BLOCK = 64
SEL_TOPK = 16
WINDOW = 512
Q_BLOCK = 256
FORCE_SCORE = 1e4
LRU_DIM = 512
LRU_HEADS = 8
LRU_CONV_WIDTH = 4
LRU_C = 8.0
N_BRANCH = 3

LANES = 128
NEG = -1e30
LOG2E = 1.4426950408889634
SLOPE_PIECES = 3
KEY_TILE = 512
WIN_TILE = 128
CMP_CHUNK = 256
TOKEN_TILE = 512
BRANCH_TILE = 256
CONV_HALO = 32
LRU_HALO = 8
VMEM_LIMIT = 56 * 1024 * 1024

COL_CONV = 0
COL_Q = 1024
COL_KVC = 2048
COL_K = 2304
COL_V = 2816
COL_LRU = 3200
COL_END = 4224


def _params(n_grid):
    return pltpu.CompilerParams(dimension_semantics=("arbitrary",) * n_grid,
                                vmem_limit_bytes=VMEM_LIMIT)


def _dot(a, b):
    return jnp.dot(a, b, preferred_element_type=F32)


def _dot_nt(a, b):
    return lax.dot_general(a, b, (((1,), (1,)), ((), ())), preferred_element_type=F32)


def _split_bf16(a):
    hi = a.astype(BF16)
    lo = (a - hi.astype(F32)).astype(BF16)
    return hi, lo


def _dot3(a, b):
    a_hi, a_lo = _split_bf16(a)
    b_hi, b_lo = _split_bf16(b)
    return _dot(a_hi, b_hi) + (_dot(a_hi, b_lo) + _dot(a_lo, b_hi))


def _gelu(x):
    return 0.5 * x * (1.0 + jnp.tanh(0.7978845608028654 * (x + 0.044715 * (x * x * x))))


def _sigmoid(x):
    return 0.5 * jnp.tanh(0.5 * x) + 0.5


def _silu(x):
    return x * _sigmoid(x)


def _fold8(x, op):
    groups = [x[i:i + 8, :] for i in range(0, x.shape[0], 8)]
    while len(groups) > 1:
        groups = [op(groups[i], groups[i + 1]) if i + 1 < len(groups) else groups[i]
                  for i in range(0, len(groups), 2)]
    return groups[0]


def _position_lanes(pos, lane):
    hi = (pos >> 7).astype(F32)
    lo = (pos & 127).astype(F32)
    off = lane - HEAD_DIM
    return jnp.where((off >= 0) & (off < SLOPE_PIECES), hi,
                     jnp.where((off >= SLOPE_PIECES) & (off < 2 * SLOPE_PIECES), lo, 0.0))


def _mod_norm(x, g, sc, sh):
    y = x * lax.rsqrt(jnp.mean(x * x, axis=-1, keepdims=True) + EPS)
    return (y * g) * (1.0 + sc) + sh


def _mod_kernel(c_ref, w_ref, b_ref, o_ref):
    o_ref[...] = _dot3(c_ref[...], w_ref[...]) + b_ref[...]


def _modulation(c, w_mod, b_mod):
    depth, d, n = w_mod.shape
    bsz = c.shape[0]
    rows = 8
    c_pad = jnp.zeros((rows, d), F32).at[:bsz].set(c)
    out = pl.pallas_call(
        _mod_kernel,
        grid=(depth, n // d),
        in_specs=[pl.BlockSpec((rows, d), lambda l, j: (0, 0)),
                  pl.BlockSpec((None, d, d), lambda l, j: (l, 0, j)),
                  pl.BlockSpec((None, 1, d), lambda l, j: (l, 0, j))],
        out_specs=pl.BlockSpec((None, rows, d), lambda l, j: (l, 0, j)),
        out_shape=jax.ShapeDtypeStruct((depth, rows, n), F32),
        compiler_params=_params(2),
        name="modulation",
    )(c_pad, w_mod, b_mod.reshape(depth, 1, n))
    return out[:, :bsz].reshape(depth, bsz, n // d, 1, d)


def _premix_kernel(x_ref, g_ref, sc_ref, sh_ref, w_ref, qaug_ref,
                   uconv_ref, ulru_ref, q_ref, kvc_ref, ks_ref, kw_ref, vst_ref, vwt_ref, gt_ref,
                   kvc_tmp_ref):
    i = pl.program_id(1)
    tm = x_ref.shape[0]
    hb = _mod_norm(x_ref[...], g_ref[...], sc_ref[...], sh_ref[...]).astype(BF16)

    def proj(a, b):
        return _dot(hb, w_ref[:, a:b])

    uconv_ref[...] = proj(COL_CONV, COL_Q)
    ulru_ref[...] = proj(COL_LRU, COL_END)
    q_ref[...] = (proj(COL_Q, COL_KVC) + qaug_ref[...]).astype(BF16)
    ukv = proj(COL_KVC, COL_K)
    for half in range(2):
        kvc_tmp_ref[half] = ukv[:, half * LANES:(half + 1) * LANES]
    for t in range(CMP_STRIDE):
        for half in range(2):
            col = t * 256 + half * LANES
            kvc_ref[:, col:col + LANES] = (
                kvc_tmp_ref[half, pl.ds(t, tm // CMP_STRIDE, stride=CMP_STRIDE), :].astype(BF16))
    pos = i * tm + lax.broadcasted_iota(jnp.int32, (tm, LANES), 0)
    lane = lax.broadcasted_iota(jnp.int32, (tm, LANES), 1)
    kaug = _position_lanes(pos, lane)
    uk = proj(COL_K, COL_V)
    ks_ref[0] = (uk[:, 0:128] + kaug).astype(BF16)
    ks_ref[1] = (uk[:, 128:256] + kaug).astype(BF16)
    kw_ref[0] = (uk[:, 256:384] + kaug).astype(BF16)
    kw_ref[1] = (uk[:, 384:512] + kaug).astype(BF16)
    uv = proj(COL_V, COL_LRU)
    vst_ref[...] = uv[:, 0:128].T.astype(BF16)
    vwt = uv[:, 128:256].T.astype(BF16)
    for t in range(tm // WIN_TILE):
        vwt_ref[t] = vwt[:, t * WIN_TILE:(t + 1) * WIN_TILE]
    gt = _sigmoid(uv[:, 256:384]).T
    gt_ref[...] = gt[0:32, :]


def _premix(x, mod, layer, norm_g, w_pack, qaug):
    bsz, s, d = x.shape
    tm = TOKEN_TILE
    assert tm == KEY_TILE and s % tm == 0
    n_t = s // tm
    row = lambda k: pl.BlockSpec((None, None, None, 1, d), lambda b, i: (layer, b, k, 0, 0))
    out_shapes = (
        jax.ShapeDtypeStruct((bsz, s, 1024), F32),
        jax.ShapeDtypeStruct((bsz, s, 1024), F32),
        jax.ShapeDtypeStruct((bsz, s, 1024), BF16),
        jax.ShapeDtypeStruct((bsz, s // CMP_STRIDE, CMP_STRIDE * 256), BF16),
        jax.ShapeDtypeStruct((bsz, N_KV, s, LANES), BF16),
        jax.ShapeDtypeStruct((bsz, N_KV, s, LANES), BF16),
        jax.ShapeDtypeStruct((bsz, n_t, LANES, KEY_TILE), BF16),
        jax.ShapeDtypeStruct((bsz, s // WIN_TILE, LANES, WIN_TILE), BF16),
        jax.ShapeDtypeStruct((bsz, 32, s), F32),
    )
    out_specs = (
        pl.BlockSpec((None, tm, 1024), lambda b, i: (b, i, 0)),
        pl.BlockSpec((None, tm, 1024), lambda b, i: (b, i, 0)),
        pl.BlockSpec((None, tm, 1024), lambda b, i: (b, i, 0)),
        pl.BlockSpec((None, tm // CMP_STRIDE, CMP_STRIDE * 256), lambda b, i: (b, i, 0)),
        pl.BlockSpec((None, N_KV, tm, LANES), lambda b, i: (b, 0, i, 0)),
        pl.BlockSpec((None, N_KV, tm, LANES), lambda b, i: (b, 0, i, 0)),
        pl.BlockSpec((None, None, LANES, KEY_TILE), lambda b, i: (b, i, 0, 0)),
        pl.BlockSpec((None, tm // WIN_TILE, LANES, WIN_TILE), lambda b, i: (b, i, 0, 0)),
        pl.BlockSpec((None, 32, tm), lambda b, i: (b, 0, i)),
    )
    return pl.pallas_call(
        _premix_kernel,
        grid=(bsz, n_t),
        in_specs=[pl.BlockSpec((None, tm, d), lambda b, i: (b, i, 0)),
                  pl.BlockSpec((1, d), lambda b, i: (0, 0)),
                  row(1), row(0),
                  pl.BlockSpec((d, COL_END), lambda b, i: (0, 0)),
                  pl.BlockSpec((1, 1024), lambda b, i: (0, 0))],
        out_specs=out_specs,
        out_shape=out_shapes,
        scratch_shapes=[pltpu.VMEM((2, tm, LANES), F32)],
        compiler_params=_params(2),
        name="premix",
    )(x, norm_g, mod, mod, w_pack, qaug)


def _branch_rows(r0, ts, uconv_ref, ulru_ref, wdw_ref, bdw_ref, lng_ref, lnb_ref,
                 wc4_ref, bc4_ref, wa_ref, ba_ref, wx_ref, bx_ref, lam_ref,
                 vext_ref, vsh_ref, xext_ref, hcar_ref):
    c = CONV_DIM
    v = uconv_ref[r0:r0 + ts, 0:c] * _sigmoid(uconv_ref[r0:r0 + ts, c:2 * c])
    vext_ref[CONV_HALO:CONV_HALO + ts, :] = v
    span = ts + CONV_HALO - 8
    for phase in range(1, 8):
        vsh_ref[phase - 1] = vext_ref[pl.ds(phase, span), :]
    acc = jnp.broadcast_to(bdw_ref[...], (ts, c))
    first = CONV_HALO - (CONV_WIDTH - 1)
    for j in range(CONV_WIDTH):
        base, phase = (first + j) // 8 * 8, (first + j) % 8
        rows = vext_ref[base:base + ts, :] if phase == 0 else vsh_ref[phase - 1, base:base + ts, :]
        acc = acc + wdw_ref[j:j + 1, :] * rows
    vext_ref[0:CONV_HALO, :] = v[ts - CONV_HALO:ts, :]
    mu = jnp.mean(acc, axis=-1, keepdims=True)
    cen = acc - mu
    var = jnp.mean(cen * cen, axis=-1, keepdims=True)
    ln = (cen * lax.rsqrt(var + EPS)) * lng_ref[...] + lnb_ref[...]
    vconv = _silu(ln).astype(BF16)

    ux = ulru_ref[r0:r0 + ts, 0:c]
    xext_ref[LRU_HALO:LRU_HALO + ts, :] = ux
    xr = jnp.broadcast_to(bc4_ref[...], (ts, c))
    first = LRU_HALO - (LRU_CONV_WIDTH - 1)
    for j in range(LRU_CONV_WIDTH):
        xr = xr + wc4_ref[j:j + 1, :] * xext_ref[pl.ds(first + j, ts), :]
    xext_ref[0:LRU_HALO, :] = ux[ts - LRU_HALO:ts, :]
    xb = xr.astype(BF16)
    r = _sigmoid(_dot(xb, wa_ref[...]) + ba_ref[...])
    gate_i = _sigmoid(_dot(xb, wx_ref[...]) + bx_ref[...])
    z = -lam_ref[...]
    softplus = jnp.maximum(z, 0.0) + jnp.log(1.0 + jnp.exp(-jnp.abs(z)))
    log_a = (-LRU_C * r) * softplus
    a = jnp.exp(log_a)
    b = jnp.sqrt(1.0 - a * a) * (gate_i * xr)
    rows = lax.broadcasted_iota(jnp.int32, (ts, c), 0)
    shift = 1
    while shift < ts:
        a_prev = pltpu.roll(a, shift, axis=0)
        b_prev = pltpu.roll(b, shift, axis=0)
        live = rows >= shift
        b = jnp.where(live, a * b_prev + b, b)
        a = jnp.where(live, a * a_prev, a)
        shift *= 2
    h = a * hcar_ref[0:1, :] + b
    hcar_ref[...] = jnp.broadcast_to(h[ts - 1:ts, :], hcar_ref.shape)
    ylru = (h * _gelu(ulru_ref[r0:r0 + ts, c:2 * c])).astype(BF16)
    return vconv, ylru


def _compress_kernel(x_ref, wlo_ref, whi_ref, pek_ref, pev_ref, w1k_ref, w1v_ref, w2_ref,
                     kc_ref, vct_ref):
    nc = x_ref.shape[0]
    x = x_ref[...]
    h_lo = _dot(x, wlo_ref[...])
    h_hi = _dot(x, whi_ref[...])
    h_next = pltpu.roll(h_hi, nc - 1, axis=0)
    bk = _dot3(pek_ref[...], w1k_ref[...])[0:1, :]
    bv = _dot3(pev_ref[...], w1v_ref[...])[0:1, :]
    bias = jnp.concatenate([bk, bk, bv, bv], axis=1)
    hid = _gelu(h_lo + h_next + bias).astype(BF16)
    out = _dot(hid, w2_ref[...])
    cend = lax.broadcasted_iota(jnp.int32, (nc, LANES), 0) * CMP_STRIDE + (CMP_LEN - 1)
    lane = lax.broadcasted_iota(jnp.int32, (nc, LANES), 1)
    aug = _position_lanes(cend, lane)
    kc_ref[0] = (out[:, 0:128] + aug).astype(BF16)
    kc_ref[1] = (out[:, 128:256] + aug).astype(BF16)
    vct = out[:, 256:384].T.astype(BF16)
    for t in range(nc // CMP_CHUNK):
        vct_ref[t] = vct[:, t * CMP_CHUNK:(t + 1) * CMP_CHUNK]


def _compress(kvc, p):
    bsz, nc, width = kvc.shape
    x = kvc
    full = lambda shape: pl.BlockSpec(shape, lambda b: (0,) * len(shape))
    return pl.pallas_call(
        _compress_kernel,
        grid=(bsz,),
        in_specs=[pl.BlockSpec((None, nc, width), lambda b: (b, 0, 0)),
                  full((width, 512)), full((width, 512)),
                  full((8, CMP_LEN * HEAD_DIM)), full((8, CMP_LEN * HEAD_DIM)),
                  full((CMP_LEN * HEAD_DIM, CMP_HIDDEN)), full((CMP_LEN * HEAD_DIM, CMP_HIDDEN)),
                  full((512, 384))],
        out_specs=(pl.BlockSpec((None, N_KV, nc, LANES), lambda b: (b, 0, 0, 0)),
                   pl.BlockSpec((None, nc // CMP_CHUNK, LANES, CMP_CHUNK), lambda b: (b, 0, 0, 0))),
        out_shape=(jax.ShapeDtypeStruct((bsz, N_KV, nc, LANES), BF16),
                   jax.ShapeDtypeStruct((bsz, nc // CMP_CHUNK, LANES, CMP_CHUNK), BF16)),
        compiler_params=_params(1),
        name="compress",
    )(x, p["cmp_w_lo"], p["cmp_w_hi"], p["cmp_pe_k"], p["cmp_pe_v"], p["nsa_w_ck1"], p["nsa_w_cv1"],
      p["cmp_w2"])


def _attn_kernel(q_ref, kc_ref, vct_ref, ovt_ref, ks_ref, vst_ref, kw_ref, vwt_ref, gt_ref,
                 o_ref, s0_ref, s1_ref, w_ref, selneg_ref, m_ref, l_ref, acc_ref, flag_ref, list_ref):
    qb = pl.program_id(2)
    q0 = qb * Q_BLOCK
    n_sel = ovt_ref.shape[1]
    n_rows = GROUP * Q_BLOCK
    qg = jnp.concatenate([q_ref[:, r * LANES:(r + 1) * LANES] for r in range(GROUP)], axis=0)
    qpos = q0 + (lax.broadcasted_iota(jnp.int32, (1, n_rows), 1) & (Q_BLOCK - 1))

    n_chunks = ((q0 + Q_BLOCK - CMP_LEN) // CMP_STRIDE + CMP_CHUNK) // CMP_CHUNK
    cend = lax.broadcasted_iota(jnp.int32, (CMP_CHUNK, n_rows), 0) * CMP_STRIDE + (CMP_LEN - 1)

    def cmp_branch(n_ch):
        def fn():
            s_all = _dot_nt(kc_ref[0:n_ch * CMP_CHUNK, :], qg)
            scores = []
            m_c = None
            for ci in range(n_ch):
                s = s_all[ci * CMP_CHUNK:(ci + 1) * CMP_CHUNK, :]
                if ci >= n_ch - 2:
                    s = jnp.where(cend + ci * CMP_CHUNK * CMP_STRIDE <= qpos, s, NEG)
                scores.append(s)
                mi = jnp.max(s, axis=0, keepdims=True)
                m_c = mi if m_c is None else jnp.maximum(m_c, mi)
            m_c = jnp.where(m_c > 0.5 * NEG, m_c, 0.0)
            probs = [jnp.exp2(s - m_c) for s in scores]
            den = probs[0].sum(axis=0, keepdims=True)
            for p in probs[1:]:
                den = den + jnp.sum(p, axis=0, keepdims=True)
            inv_den = 1.0 / jnp.where(den > 0, den, 1.0)
            o_c = None
            imp = None
            for ci, p in enumerate(probs):
                p = p * inv_den
                part = _dot(vct_ref[ci], p.astype(BF16))
                o_c = part if o_c is None else o_c + part
                p_sum = p[:, 0:Q_BLOCK]
                for r in range(1, GROUP):
                    p_sum = p_sum + p[:, r * Q_BLOCK:(r + 1) * Q_BLOCK]
                p_hi, p_lo = _split_bf16(p_sum)
                part = _dot(ovt_ref[ci], p_hi) + _dot(ovt_ref[ci], p_lo)
                imp = part if imp is None else imp + part
            return o_c, imp
        return fn

    o_c, imp = lax.switch(n_chunks - 1, [cmp_branch(n) for n in range(1, kc_ref.shape[0] // CMP_CHUNK + 1)])

    blk = lax.broadcasted_iota(jnp.int32, (n_sel, LANES), 0)
    works = []
    for h in range(Q_BLOCK // LANES):
        qp = q0 + h * LANES + lax.broadcasted_iota(jnp.int32, (n_sel, LANES), 1)
        qid = qp // SEL_BLOCK
        forced = (blk == 0) | (blk == qid) | (blk == qid - 1)
        valid = blk * SEL_BLOCK <= qp
        works.append(jnp.where(forced, -jnp.inf, jnp.where(valid, imp[:, h * LANES:(h + 1) * LANES], -1.0)))
    for _ in range(min(SEL_TOPK, n_sel) - 3):
        for h, work_h in enumerate(works):
            best = jnp.max(work_h, axis=0, keepdims=True)
            first = jnp.min(jnp.where(work_h == best, blk, n_sel), axis=0, keepdims=True)
            works[h] = jnp.where(blk == first, -jnp.inf, work_h)
    work = jnp.concatenate(works, axis=1)

    def scores_into(buf_ref, kt):
        k0 = pl.multiple_of(kt * KEY_TILE, KEY_TILE)
        buf_ref[...] = _dot_nt(ks_ref[pl.ds(k0, KEY_TILE), :], qg)

    scores_into(s0_ref, 0)

    n_win = (WINDOW + Q_BLOCK) // WIN_TILE
    t0 = jnp.maximum(q0 - WINDOW, 0) // WIN_TILE
    kstart = pl.multiple_of(t0 * WIN_TILE, WIN_TILE)
    w_ref[...] = _dot_nt(kw_ref[pl.ds(kstart, n_win * WIN_TILE), :], qg)
    wrow = lax.broadcasted_iota(jnp.int32, (WIN_TILE, n_rows), 0)
    qrel = qpos - kstart

    def win_scores(t):
        dist = qrel - (t * WIN_TILE + wrow)
        visible = (dist >= 0) & (dist < WINDOW) if t * WIN_TILE < Q_BLOCK else dist >= 0
        return jnp.where(visible, w_ref[t * WIN_TILE:(t + 1) * WIN_TILE, :], NEG)

    top = None
    for t in range(n_win):
        mt = _fold8(win_scores(t), jnp.maximum)
        top = mt if top is None else jnp.maximum(top, mt)
    m_w = jnp.max(top, axis=0, keepdims=True)
    total = None
    o_w = None
    for t in range(n_win):
        pt = jnp.exp2(win_scores(t) - m_w)
        lt = _fold8(pt, jnp.add)
        total = lt if total is None else total + lt
        part = _dot(vwt_ref[t0 + t], pt.astype(BF16))
        o_w = part if o_w is None else o_w + part
    o_w = o_w / jnp.sum(total, axis=0, keepdims=True)

    chosen = jnp.where(work == -jnp.inf, 1.0, 0.0)
    selneg = jnp.where(work == -jnp.inf, 0.0, NEG)
    selneg_ref[...] = jnp.concatenate([selneg] * GROUP, axis=1)
    blocks_per_tile = KEY_TILE // SEL_BLOCK
    any_q = jnp.max(chosen, axis=1, keepdims=True)
    for t in range(n_sel // blocks_per_tile):
        tile_any = jnp.max(any_q[t * blocks_per_tile:(t + 1) * blocks_per_tile, :])
        flag_ref[t] = (tile_any > 0).astype(jnp.int32)

    n_tiles = q0 // KEY_TILE + 1
    krow = lax.broadcasted_iota(jnp.int32, (SEL_BLOCK, n_rows), 0)
    m_ref[...] = jnp.full((1, n_rows), NEG, F32)
    l_ref[...] = jnp.zeros((1, n_rows), F32)
    acc_ref[...] = jnp.zeros((HEAD_DIM, n_rows), F32)

    def compact(kt, n):
        @pl.when(flag_ref[kt] > 0)
        def _():
            list_ref[n] = kt
        return n + flag_ref[kt]

    n_act = lax.fori_loop(0, n_tiles - 1, compact, 0)
    list_ref[n_act] = n_tiles - 1

    def sel_tile(buf_ref, kt, causal):
        sel_rows = selneg_ref[pl.ds(pl.multiple_of(kt * blocks_per_tile, blocks_per_tile),
                                    blocks_per_tile), :]

        def block_scores(j):
            sj = buf_ref[j * SEL_BLOCK:(j + 1) * SEL_BLOCK, :]
            if causal:
                sj = jnp.where(kt * KEY_TILE + j * SEL_BLOCK + krow <= qpos, sj, NEG)
            return sj

        m_old = m_ref[...]
        top = None
        for j in range(blocks_per_tile):
            mj = _fold8(block_scores(j), jnp.maximum) + sel_rows[j:j + 1, :]
            top = mj if top is None else jnp.maximum(top, mj)
        m_new = jnp.maximum(m_old, jnp.max(top, axis=0, keepdims=True))
        alpha = jnp.exp2(m_old - m_new)
        total = None
        probs = []
        for j in range(blocks_per_tile):
            pj = jnp.exp2(block_scores(j) - (m_new - sel_rows[j:j + 1, :]))
            lj = _fold8(pj, jnp.add)
            total = lj if total is None else total + lj
            probs.append(pj.astype(BF16))
        m_ref[...] = m_new
        l_ref[...] = alpha * l_ref[...] + jnp.sum(total, axis=0, keepdims=True)
        acc_ref[...] = alpha * acc_ref[...] + _dot(vst_ref[kt], jnp.concatenate(probs, axis=0))

    def pipelined(i, carry):
        scores_into(s1_ref, list_ref[2 * i + 1])
        sel_tile(s0_ref, list_ref[2 * i], False)

        @pl.when(2 * i + 1 < n_act)
        def _():
            scores_into(s0_ref, list_ref[2 * i + 2])
            sel_tile(s1_ref, list_ref[2 * i + 1], False)
        return carry

    lax.fori_loop(0, (n_act + 1) // 2, pipelined, 0)

    @pl.when((n_act & 1) == 0)
    def _():
        sel_tile(s0_ref, n_tiles - 1, True)

    @pl.when((n_act & 1) == 1)
    def _():
        sel_tile(s1_ref, n_tiles - 1, True)

    o_s = acc_ref[...] / l_ref[...]

    def gate(branch):
        return jnp.concatenate([gt_ref[branch * GROUP + r:branch * GROUP + r + 1, :] for r in range(GROUP)],
                               axis=1)
    o = gate(0) * o_c + gate(1) * o_s + gate(2) * o_w
    o_ref[...] = jnp.concatenate([o[:, r * Q_BLOCK:(r + 1) * Q_BLOCK].T for r in range(GROUP)],
                                 axis=1).astype(BF16)


def _attention(q, kc, vct, ovt, ks, vst, kw, vwt, gt):
    bsz, s, _ = q.shape
    nc = kc.shape[2]
    n_sel = s // SEL_BLOCK
    n_qb = s // Q_BLOCK
    n_kt = s // KEY_TILE
    n_wt = s // WIN_TILE
    n_ch = nc // CMP_CHUNK
    n_rows = GROUP * Q_BLOCK
    assert s % KEY_TILE == 0 and s >= (WINDOW + Q_BLOCK) and n_sel % 8 == 0 and nc % CMP_CHUNK == 0
    assert KEY_TILE % Q_BLOCK == 0 and Q_BLOCK % WIN_TILE == 0
    return pl.pallas_call(
        _attn_kernel,
        grid=(bsz, N_KV, n_qb),
        in_specs=[pl.BlockSpec((None, Q_BLOCK, GROUP * LANES), lambda b, g, i: (b, i, g)),
                  pl.BlockSpec((None, None, nc, LANES), lambda b, g, i: (b, g, 0, 0)),
                  pl.BlockSpec((None, n_ch, HEAD_DIM, CMP_CHUNK), lambda b, g, i: (b, 0, g, 0)),
                  pl.BlockSpec((n_ch, n_sel, CMP_CHUNK), lambda b, g, i: (0, 0, 0)),
                  pl.BlockSpec((None, None, s, LANES), lambda b, g, i: (b, g, 0, 0)),
                  pl.BlockSpec((None, n_kt, HEAD_DIM, KEY_TILE), lambda b, g, i: (b, 0, g, 0)),
                  pl.BlockSpec((None, None, s, LANES), lambda b, g, i: (b, g, 0, 0)),
                  pl.BlockSpec((None, n_wt, HEAD_DIM, WIN_TILE), lambda b, g, i: (b, 0, g, 0)),
                  pl.BlockSpec((None, 16, Q_BLOCK), lambda b, g, i: (b, g, i))],
        out_specs=pl.BlockSpec((None, Q_BLOCK, GROUP * HEAD_DIM), lambda b, g, i: (b, i, g)),
        out_shape=jax.ShapeDtypeStruct((bsz, s, N_HEADS * HEAD_DIM), BF16),
        scratch_shapes=[pltpu.VMEM((KEY_TILE, n_rows), F32),
                        pltpu.VMEM((KEY_TILE, n_rows), F32),
                        pltpu.VMEM((WINDOW + Q_BLOCK, n_rows), F32),
                        pltpu.VMEM((n_sel, n_rows), F32),
                        pltpu.VMEM((1, n_rows), F32),
                        pltpu.VMEM((1, n_rows), F32),
                        pltpu.VMEM((HEAD_DIM, n_rows), F32),
                        pltpu.SMEM((n_kt,), jnp.int32),
                        pltpu.SMEM((n_kt + 1,), jnp.int32)],
        compiler_params=_params(3),
        name="nsa_attention",
    )(q, kc, vct, ovt, ks, vst, kw, vwt, gt)


def _merge_kernel(x_ref, g_ref, sc_ref, sh_ref, gate_ref, uconv_ref, ulru_ref, onsa_ref,
                  wdw_ref, bdw_ref, lng_ref, lnb_ref, wc4_ref, bc4_ref, wa_ref, ba_ref, wx_ref, bx_ref,
                  lam_ref, wm_ref, bm_ref, wc_ref, wn_ref, wl_ref, wo_ref, o_ref,
                  vext_ref, vsh_ref, xext_ref, hcar_ref):
    tm, d = x_ref.shape
    c = CONV_DIM

    @pl.when(pl.program_id(1) == 0)
    def _():
        vext_ref[0:CONV_HALO, :] = jnp.zeros((CONV_HALO, c), F32)
        xext_ref[0:LRU_HALO, :] = jnp.zeros((LRU_HALO, c), F32)
        hcar_ref[...] = jnp.zeros(hcar_ref.shape, F32)

    x = x_ref[...]
    hb = _mod_norm(x, g_ref[...], sc_ref[...], sh_ref[...]).astype(BF16)

    def merge_gate(k):
        return _sigmoid(_dot(hb, wm_ref[:, k * d:(k + 1) * d]) + bm_ref[:, k * d:(k + 1) * d])

    halves = []
    early = []
    for r0 in range(0, tm, BRANCH_TILE):
        halves.append(_branch_rows(r0, BRANCH_TILE, uconv_ref, ulru_ref, wdw_ref, bdw_ref, lng_ref, lnb_ref,
                                   wc4_ref, bc4_ref, wa_ref, ba_ref, wx_ref, bx_ref, lam_ref,
                                   vext_ref, vsh_ref, xext_ref, hcar_ref))
        if not early:
            early.append(merge_gate(1) * _dot(onsa_ref[...], wn_ref[...]))
        else:
            early.append(merge_gate(0))
    merged = early[0] + early[1] * _dot(jnp.concatenate([h[0] for h in halves], axis=0), wc_ref[...])
    merged = merged + merge_gate(2) * _dot(jnp.concatenate([h[1] for h in halves], axis=0), wl_ref[...])
    o_ref[...] = x + gate_ref[...] * _dot(merged.astype(BF16), wo_ref[...])


def _merge(x, mod, layer, norm_g, uconv, ulru, onsa, p):
    bsz, s, d = x.shape
    tm = TOKEN_TILE
    ts = BRANCH_TILE
    c = CONV_DIM
    assert tm % ts == 0
    row = lambda k: pl.BlockSpec((None, None, None, 1, d), lambda b, i: (layer, b, k, 0, 0))
    full = lambda shape: pl.BlockSpec(shape, lambda b, i: (0,) * len(shape))
    tok = lambda w: pl.BlockSpec((None, tm, w), lambda b, i: (b, i, 0))
    return pl.pallas_call(
        _merge_kernel,
        grid=(bsz, s // tm),
        in_specs=[tok(d), full((1, d)), row(1), row(0), row(2), tok(2 * c), tok(2 * c), tok(c),
                  full((CONV_WIDTH, c)), full((1, c)), full((1, c)), full((1, c)),
                  full((LRU_CONV_WIDTH, c)), full((1, c)),
                  full((c, c)), full((1, c)), full((c, c)), full((1, c)), full((1, c)),
                  full((d, N_BRANCH * d)), full((1, N_BRANCH * d)),
                  full((c, d)), full((c, d)), full((c, d)), full((d, d))],
        out_specs=tok(d),
        out_shape=jax.ShapeDtypeStruct((bsz, s, d), F32),
        scratch_shapes=[pltpu.VMEM((ts + CONV_HALO, c), F32),
                        pltpu.VMEM((7, ts + CONV_HALO - 8, c), F32),
                        pltpu.VMEM((ts + LRU_HALO, c), F32),
                        pltpu.VMEM((8, c), F32)],
        compiler_params=_params(2),
        name="merge",
    )(x, norm_g, mod, mod, mod, uconv, ulru, onsa,
      p["conv_w_dw"], p["conv_b_dw"], p["conv_ln_g"], p["conv_ln_b"],
      p["lru_w_conv"], p["lru_b_conv"], p["lru_wa_bd"], p["lru_b_a"], p["lru_wx_bd"], p["lru_b_x"],
      p["lru_lam"], p["w_merge"], p["b_merge"],
      p["conv_w_out"], p["nsa_w_out"], p["lru_w_out"], p["w_out"])


def _ffn_kernel(x_ref, g_ref, sc_ref, sh_ref, gate_ref, wa_ref, wb_ref, wo_ref, fin_ref, o_ref,
                *, chunk, final):
    x = x_ref[...]
    hb = _mod_norm(x, g_ref[...], sc_ref[...], sh_ref[...]).astype(BF16)
    ffn = wa_ref.shape[1]
    acc = None
    for c0 in range(0, ffn, chunk):
        a = _dot(hb, wa_ref[:, c0:c0 + chunk])
        b = _dot(hb, wb_ref[:, c0:c0 + chunk])
        part = _dot((_silu(a) * b).astype(BF16), wo_ref[c0:c0 + chunk, :])
        acc = part if acc is None else acc + part
    y = x + gate_ref[...] * acc
    if final:
        y = (y * lax.rsqrt(jnp.mean(y * y, axis=-1, keepdims=True) + EPS)) * fin_ref[...]
    o_ref[...] = y


def _ffn(x, mod, layer, norm_g, p, final_norm, final):
    bsz, s, d = x.shape
    tm = TOKEN_TILE
    ffn = p["w_ffn_a"].shape[1]
    chunk = 256
    assert ffn % chunk == 0
    row = lambda k: pl.BlockSpec((None, None, None, 1, d), lambda b, i: (layer, b, k, 0, 0))
    full = lambda shape: pl.BlockSpec(shape, lambda b, i: (0,) * len(shape))
    tok = pl.BlockSpec((None, tm, d), lambda b, i: (b, i, 0))
    return pl.pallas_call(
        functools.partial(_ffn_kernel, chunk=chunk, final=final),
        grid=(bsz, s // tm),
        in_specs=[tok, full((1, d)), row(4), row(3), row(5),
                  full((d, ffn)), full((d, ffn)), full((ffn, d)), full((1, d))],
        out_specs=tok,
        out_shape=jax.ShapeDtypeStruct((bsz, s, d), F32),
        compiler_params=_params(2),
        name="ffn",
    )(x, norm_g, mod, mod, mod, p["w_ffn_a"], p["w_ffn_b"], p["w_ffn_out"], final_norm)


def _pack_input_projection(w_in):
    d = w_in.shape[0]
    q0, kv0, gate0, lru0 = 1024, 1536, 2304, 2328

    def head_slots(w, n):
        w = w.reshape(d, n, HEAD_DIM)
        return jnp.pad(w, ((0, 0), (0, 0), (0, LANES - HEAD_DIM))).reshape(d, n * LANES)

    kv = lambda i: w_in[:, kv0 + i * 128:kv0 + (i + 1) * 128]
    gates = w_in[:, gate0:gate0 + N_KV * GROUP * 3].reshape(d, N_KV, GROUP, 3)
    gates = jnp.pad(gates.transpose(0, 1, 3, 2).reshape(d, N_KV, 3 * GROUP), ((0, 0), (0, 0), (0, 4)))
    gates = jnp.pad(gates.reshape(d, N_KV * 16), ((0, 0), (0, LANES - N_KV * 16)))
    return jnp.concatenate([
        w_in[:, 0:q0],
        head_slots(w_in[:, q0:kv0] * (HEAD_DIM ** -0.5 * LOG2E), N_HEADS),
        kv(0), kv(1),
        head_slots(kv(2), N_KV), head_slots(kv(4), N_KV),
        kv(3), kv(5), gates,
        w_in[:, lru0:lru0 + 2 * LRU_DIM]], axis=1)


def _bf16_pieces(x):
    pieces = []
    for _ in range(SLOPE_PIECES):
        piece = float(np.asarray(x, np.float32).astype(BF16).astype(np.float64))
        pieces.append(piece)
        x = x - piece
    return pieces


def _q_slope_row():
    row = np.zeros((1, N_HEADS * LANES), np.float32)
    for h in range(N_HEADS):
        slope = 2.0 ** (-8.0 * (h + 1) / N_HEADS) * LOG2E
        base = h * LANES + HEAD_DIM
        row[0, base:base + SLOPE_PIECES] = _bf16_pieces(slope * 128.0)
        row[0, base + SLOPE_PIECES:base + 2 * SLOPE_PIECES] = _bf16_pieces(slope)
    return row


def _overlap_matrix(s):
    nc = s // CMP_STRIDE
    n_sel = s // SEL_BLOCK
    cs = np.arange(nc)[:, None] * CMP_STRIDE
    ss = np.arange(n_sel)[None, :] * SEL_BLOCK
    ov = np.clip(np.minimum(cs + CMP_LEN, ss + SEL_BLOCK) - np.maximum(cs, ss), 0, None) / CMP_LEN
    ovt = ov.T.astype(np.float32).reshape(n_sel, nc // CMP_CHUNK, CMP_CHUNK)
    return np.ascontiguousarray(ovt.transpose(1, 0, 2))


def _block_diag(w):
    heads, n, _ = w.shape
    eye = jnp.asarray(np.eye(heads, dtype=np.float32))
    return (w[:, :, None, :] * eye[:, None, :, None]).reshape(heads * n, heads * n)


def _layer_params(l, a):
    w_pack = _pack_input_projection(a["w_in"][l])
    assert w_pack.shape[1] == COL_END
    row = lambda v: v[l].reshape(1, -1)

    def chunk_weights(half):
        wk = a["nsa_w_ck1"][l].reshape(CMP_LEN, HEAD_DIM, CMP_HIDDEN)[half * CMP_STRIDE:(half + 1) * CMP_STRIDE]
        wv = a["nsa_w_cv1"][l].reshape(CMP_LEN, HEAD_DIM, CMP_HIDDEN)[half * CMP_STRIDE:(half + 1) * CMP_STRIDE]
        k_slots = jnp.asarray(np.diag([1.0, 1.0, 0.0, 0.0]).astype(np.float32))
        v_slots = jnp.asarray(np.diag([0.0, 0.0, 1.0, 1.0]).astype(np.float32))
        w = (wk[:, None, :, None, :] * k_slots[None, :, None, :, None]
             + wv[:, None, :, None, :] * v_slots[None, :, None, :, None])
        return w.reshape(CMP_STRIDE * 256, 4 * CMP_HIDDEN).astype(BF16)

    eye2 = jnp.asarray(np.eye(N_KV, dtype=np.float32))
    ck2 = jnp.pad(a["nsa_w_ck2"][l], ((0, 0), (0, LANES - HEAD_DIM)))
    block2 = lambda w: (w[None, :, None, :] * eye2[:, None, :, None]).reshape(N_KV * w.shape[0], N_KV * w.shape[1])
    w2_k = block2(ck2)
    w2_v = block2(a["nsa_w_cv2"][l])
    w2 = jnp.concatenate([jnp.pad(w2_k, ((0, 0), (0, N_KV * HEAD_DIM))),
                          jnp.pad(w2_v, ((0, 0), (N_KV * LANES, 0)))], axis=0)
    pe_rows = lambda pe: jnp.pad(pe.reshape(1, -1), ((0, 7), (0, 0)))
    ffn = a["w_ffn_out"].shape[1]
    return {
        "w_pack": w_pack.astype(BF16),
        "conv_w_dw": a["conv_w_dw"][l], "conv_b_dw": row(a["conv_b_dw"]),
        "conv_ln_g": row(a["conv_ln_g"]), "conv_ln_b": row(a["conv_ln_b"]),
        "lru_w_conv": a["lru_w_conv"][l], "lru_b_conv": row(a["lru_b_conv"]),
        "lru_wa_bd": _block_diag(a["lru_w_a"][l]).astype(BF16), "lru_b_a": row(a["lru_b_a"]),
        "lru_wx_bd": _block_diag(a["lru_w_x"][l]).astype(BF16), "lru_b_x": row(a["lru_b_x"]),
        "lru_lam": row(a["lru_lam"]),
        "cmp_w_lo": chunk_weights(0), "cmp_w_hi": chunk_weights(1),
        "cmp_pe_k": pe_rows(a["nsa_pe_k"][l]), "cmp_pe_v": pe_rows(a["nsa_pe_v"][l]),
        "nsa_w_ck1": a["nsa_w_ck1"][l], "nsa_w_cv1": a["nsa_w_cv1"][l],
        "cmp_w2": w2.astype(BF16),
        "w_merge": a["w_merge"][l].astype(BF16), "b_merge": row(a["b_merge"]),
        "conv_w_out": a["conv_w_out"][l].astype(BF16), "nsa_w_out": a["nsa_w_out"][l].astype(BF16),
        "lru_w_out": a["lru_w_out"][l].astype(BF16), "w_out": a["w_out"][l].astype(BF16),
        "w_ffn_a": a["w_ffn_in"][l][:, :ffn].astype(BF16), "w_ffn_b": a["w_ffn_in"][l][:, ffn:].astype(BF16),
        "w_ffn_out": a["w_ffn_out"][l].astype(BF16),
    }


def kernel(x, c, w_mod, b_mod, norm_mix, norm_ffn, w_in, conv_w_dw, conv_b_dw, conv_ln_g, conv_ln_b,
           conv_w_out, nsa_pe_k, nsa_w_ck1, nsa_w_ck2, nsa_pe_v, nsa_w_cv1, nsa_w_cv2, nsa_w_out,
           lru_w_conv, lru_b_conv, lru_w_a, lru_b_a, lru_w_x, lru_b_x, lru_lam, lru_w_out,
           w_merge, b_merge, w_out, w_ffn_in, w_ffn_out, final_norm):
    a = dict(w_in=w_in, conv_w_dw=conv_w_dw, conv_b_dw=conv_b_dw, conv_ln_g=conv_ln_g, conv_ln_b=conv_ln_b,
             conv_w_out=conv_w_out, nsa_pe_k=nsa_pe_k, nsa_w_ck1=nsa_w_ck1, nsa_w_ck2=nsa_w_ck2,
             nsa_pe_v=nsa_pe_v, nsa_w_cv1=nsa_w_cv1, nsa_w_cv2=nsa_w_cv2, nsa_w_out=nsa_w_out,
             lru_w_conv=lru_w_conv, lru_b_conv=lru_b_conv, lru_w_a=lru_w_a, lru_b_a=lru_b_a,
             lru_w_x=lru_w_x, lru_b_x=lru_b_x, lru_lam=lru_lam, lru_w_out=lru_w_out,
             w_merge=w_merge, b_merge=b_merge, w_out=w_out, w_ffn_in=w_ffn_in, w_ffn_out=w_ffn_out)
    depth = w_in.shape[0]
    s = x.shape[1]
    mod = _modulation(c, w_mod, b_mod)
    qaug = jnp.asarray(_q_slope_row())
    ovt = jnp.asarray(_overlap_matrix(s)).astype(BF16)
    fin = final_norm.reshape(1, -1)
    for l in range(depth):
        p = _layer_params(l, a)
        g_mix = norm_mix[l].reshape(1, -1)
        g_ffn = norm_ffn[l].reshape(1, -1)
        uconv, ulru, q, kvc, ks, kw, vst, vwt, gt = _premix(x, mod, l, g_mix, p["w_pack"], qaug)
        kc, vct = _compress(kvc, p)
        onsa = _attention(q, kc, vct, ovt, ks, vst, kw, vwt, gt)
        x = _merge(x, mod, l, g_mix, uconv, ulru, onsa, p)
        x = _ffn(x, mod, l, g_ffn, p, fin, final=(l == depth - 1))
    return x
```

```python
import functools

import numpy as np
import jax
import jax.numpy as jnp
from jax import lax
from jax.experimental import pallas as pl
from jax.experimental.pallas import tpu as pltpu

F32 = jnp.float32
BF16 = jnp.bfloat16

EPS = 1e-6
CONV_DIM = 512
CONV_WIDTH = 31
N_HEADS = 8
N_KV = 2
GROUP = N_HEADS // N_KV
HEAD_DIM = 64
CMP_LEN = 32
CMP_STRIDE = 16
CMP_HIDDEN = 128
SEL_BLOCK = 64
SEL_TOPK = 16
WINDOW = 512
Q_BLOCK = 256
FORCE_SCORE = 1e4
LRU_DIM = 512
LRU_HEADS = 8
LRU_CONV_WIDTH = 4
LRU_C = 8.0
N_BRANCH = 3

LANES = 128
NEG = -1e30
LOG2E = 1.4426950408889634
SLOPE_PIECES = 3
KEY_TILE = 512
WIN_TILE = 128
CMP_CHUNK = 256
TOKEN_TILE = 512
BRANCH_TILE = 256
CONV_HALO = 32
LRU_HALO = 8
VMEM_LIMIT = 56 * 1024 * 1024

COL_CONV = 0
COL_Q = 1024
COL_KVC = 2048
COL_K = 2304
COL_V = 2816
COL_LRU = 3200
COL_END = 4224


def _params(n_grid):
    return pltpu.CompilerParams(dimension_semantics=("arbitrary",) * n_grid,
                                vmem_limit_bytes=VMEM_LIMIT)


def _dot(a, b):
    return jnp.dot(a, b, preferred_element_type=F32)


def _dot_nt(a, b):
    return lax.dot_general(a, b, (((1,), (1,)), ((), ())), preferred_element_type=F32)


def _split_bf16(a):
    hi = a.astype(BF16)
    lo = (a - hi.astype(F32)).astype(BF16)
    return hi, lo


def _dot3(a, b):
    a_hi, a_lo = _split_bf16(a)
    b_hi, b_lo = _split_bf16(b)
    return _dot(a_hi, b_hi) + (_dot(a_hi, b_lo) + _dot(a_lo, b_hi))


def _gelu(x):
    return 0.5 * x * (1.0 + jnp.tanh(0.7978845608028654 * (x + 0.044715 * (x * x * x))))


def _sigmoid(x):
    return 0.5 * jnp.tanh(0.5 * x) + 0.5


def _silu(x):
    return x * _sigmoid(x)


def _fold8(x, op):
    groups = [x[i:i + 8, :] for i in range(0, x.shape[0], 8)]
    while len(groups) > 1:
        groups = [op(groups[i], groups[i + 1]) if i + 1 < len(groups) else groups[i]
                  for i in range(0, len(groups), 2)]
    return groups[0]


def _position_lanes(pos, lane):
    hi = (pos >> 7).astype(F32)
    lo = (pos & 127).astype(F32)
    off = lane - HEAD_DIM
    return jnp.where((off >= 0) & (off < SLOPE_PIECES), hi,
                     jnp.where((off >= SLOPE_PIECES) & (off < 2 * SLOPE_PIECES), lo, 0.0))


def _mod_norm(x, g, sc, sh):
    y = x * lax.rsqrt(jnp.mean(x * x, axis=-1, keepdims=True) + EPS)
    return (y * g) * (1.0 + sc) + sh


def _mod_kernel(c_ref, w_ref, b_ref, o_ref):
    o_ref[...] = _dot3(c_ref[...], w_ref[...]) + b_ref[...]


def _modulation(c, w_mod, b_mod):
    depth, d, n = w_mod.shape
    bsz = c.shape[0]
    rows = 8
    c_pad = jnp.zeros((rows, d), F32).at[:bsz].set(c)
    out = pl.pallas_call(
        _mod_kernel,
        grid=(depth, n // d),
        in_specs=[pl.BlockSpec((rows, d), lambda l, j: (0, 0)),
                  pl.BlockSpec((None, d, d), lambda l, j: (l, 0, j)),
                  pl.BlockSpec((None, 1, d), lambda l, j: (l, 0, j))],
        out_specs=pl.BlockSpec((None, rows, d), lambda l, j: (l, 0, j)),
        out_shape=jax.ShapeDtypeStruct((depth, rows, n), F32),
        compiler_params=_params(2),
        name="modulation",
    )(c_pad, w_mod, b_mod.reshape(depth, 1, n))
    return out[:, :bsz].reshape(depth, bsz, n // d, 1, d)


def _premix_kernel(x_ref, g_ref, sc_ref, sh_ref, w_ref, qaug_ref,
                   uconv_ref, ulru_ref, q_ref, kvc_ref, ks_ref, kw_ref, vst_ref, vwt_ref, gt_ref,
                   kvc_tmp_ref):
    i = pl.program_id(1)
    tm = x_ref.shape[0]
    hb = _mod_norm(x_ref[...], g_ref[...], sc_ref[...], sh_ref[...]).astype(BF16)

    def proj(a, b):
        return _dot(hb, w_ref[:, a:b])

    uconv_ref[...] = proj(COL_CONV, COL_Q)
    ulru_ref[...] = proj(COL_LRU, COL_END)
    q_ref[...] = (proj(COL_Q, COL_KVC) + qaug_ref[...]).astype(BF16)
    ukv = proj(COL_KVC, COL_K)
    for half in range(2):
        kvc_tmp_ref[half] = ukv[:, half * LANES:(half + 1) * LANES]
    for t in range(CMP_STRIDE):
        for half in range(2):
            col = t * 256 + half * LANES
            kvc_ref[:, col:col + LANES] = (
                kvc_tmp_ref[half, pl.ds(t, tm // CMP_STRIDE, stride=CMP_STRIDE), :].astype(BF16))
    pos = i * tm + lax.broadcasted_iota(jnp.int32, (tm, LANES), 0)
    lane = lax.broadcasted_iota(jnp.int32, (tm, LANES), 1)
    kaug = _position_lanes(pos, lane)
    uk = proj(COL_K, COL_V)
    ks_ref[0] = (uk[:, 0:128] + kaug).astype(BF16)
    ks_ref[1] = (uk[:, 128:256] + kaug).astype(BF16)
    kw_ref[0] = (uk[:, 256:384] + kaug).astype(BF16)
    kw_ref[1] = (uk[:, 384:512] + kaug).astype(BF16)
    uv = proj(COL_V, COL_LRU)
    vst_ref[...] = uv[:, 0:128].T.astype(BF16)
    vwt = uv[:, 128:256].T.astype(BF16)
    for t in range(tm // WIN_TILE):
        vwt_ref[t] = vwt[:, t * WIN_TILE:(t + 1) * WIN_TILE]
    gt = _sigmoid(uv[:, 256:384]).T
    gt_ref[...] = gt[0:32, :]


def _premix(x, mod, layer, norm_g, w_pack, qaug):
    bsz, s, d = x.shape
    tm = TOKEN_TILE
    assert tm == KEY_TILE and s % tm == 0
    n_t = s // tm
    row = lambda k: pl.BlockSpec((None, None, None, 1, d), lambda b, i: (layer, b, k, 0, 0))
    out_shapes = (
        jax.ShapeDtypeStruct((bsz, s, 1024), F32),
        jax.ShapeDtypeStruct((bsz, s, 1024), F32),
        jax.ShapeDtypeStruct((bsz, s, 1024), BF16),
        jax.ShapeDtypeStruct((bsz, s // CMP_STRIDE, CMP_STRIDE * 256), BF16),
        jax.ShapeDtypeStruct((bsz, N_KV, s, LANES), BF16),
        jax.ShapeDtypeStruct((bsz, N_KV, s, LANES), BF16),
        jax.ShapeDtypeStruct((bsz, n_t, LANES, KEY_TILE), BF16),
        jax.ShapeDtypeStruct((bsz, s // WIN_TILE, LANES, WIN_TILE), BF16),
        jax.ShapeDtypeStruct((bsz, 32, s), F32),
    )
    out_specs = (
        pl.BlockSpec((None, tm, 1024), lambda b, i: (b, i, 0)),
        pl.BlockSpec((None, tm, 1024), lambda b, i: (b, i, 0)),
        pl.BlockSpec((None, tm, 1024), lambda b, i: (b, i, 0)),
        pl.BlockSpec((None, tm // CMP_STRIDE, CMP_STRIDE * 256), lambda b, i: (b, i, 0)),
        pl.BlockSpec((None, N_KV, tm, LANES), lambda b, i: (b, 0, i, 0)),
        pl.BlockSpec((None, N_KV, tm, LANES), lambda b, i: (b, 0, i, 0)),
        pl.BlockSpec((None, None, LANES, KEY_TILE), lambda b, i: (b, i, 0, 0)),
        pl.BlockSpec((None, tm // WIN_TILE, LANES, WIN_TILE), lambda b, i: (b, i, 0, 0)),
        pl.BlockSpec((None, 32, tm), lambda b, i: (b, 0, i)),
    )
    return pl.pallas_call(
        _premix_kernel,
        grid=(bsz, n_t),
        in_specs=[pl.BlockSpec((None, tm, d), lambda b, i: (b, i, 0)),
                  pl.BlockSpec((1, d), lambda b, i: (0, 0)),
                  row(1), row(0),
                  pl.BlockSpec((d, COL_END), lambda b, i: (0, 0)),
                  pl.BlockSpec((1, 1024), lambda b, i: (0, 0))],
        out_specs=out_specs,
        out_shape=out_shapes,
        scratch_shapes=[pltpu.VMEM((2, tm, LANES), F32)],
        compiler_params=_params(2),
        name="premix",
    )(x, norm_g, mod, mod, w_pack, qaug)


def _branch_rows(r0, ts, uconv_ref, ulru_ref, wdw_ref, bdw_ref, lng_ref, lnb_ref,
                 wc4_ref, bc4_ref, wa_ref, ba_ref, wx_ref, bx_ref, lam_ref,
                 vext_ref, vsh_ref, xext_ref, hcar_ref):
    c = CONV_DIM
    v = uconv_ref[r0:r0 + ts, 0:c] * _sigmoid(uconv_ref[r0:r0 + ts, c:2 * c])
    vext_ref[CONV_HALO:CONV_HALO + ts, :] = v
    span = ts + CONV_HALO - 8
    for phase in range(1, 8):
        vsh_ref[phase - 1] = vext_ref[pl.ds(phase, span), :]
    acc = jnp.broadcast_to(bdw_ref[...], (ts, c))
    first = CONV_HALO - (CONV_WIDTH - 1)
    for j in range(CONV_WIDTH):
        base, phase = (first + j) // 8 * 8, (first + j) % 8
        rows = vext_ref[base:base + ts, :] if phase == 0 else vsh_ref[phase - 1, base:base + ts, :]
        acc = acc + wdw_ref[j:j + 1, :] * rows
    vext_ref[0:CONV_HALO, :] = v[ts - CONV_HALO:ts, :]
    mu = jnp.mean(acc, axis=-1, keepdims=True)
    cen = acc - mu
    var = jnp.mean(cen * cen, axis=-1, keepdims=True)
    ln = (cen * lax.rsqrt(var + EPS)) * lng_ref[...] + lnb_ref[...]
    vconv = _silu(ln).astype(BF16)

    ux = ulru_ref[r0:r0 + ts, 0:c]
    xext_ref[LRU_HALO:LRU_HALO + ts, :] = ux
    xr = jnp.broadcast_to(bc4_ref[...], (ts, c))
    first = LRU_HALO - (LRU_CONV_WIDTH - 1)
    for j in range(LRU_CONV_WIDTH):
        xr = xr + wc4_ref[j:j + 1, :] * xext_ref[pl.ds(first + j, ts), :]
    xext_ref[0:LRU_HALO, :] = ux[ts - LRU_HALO:ts, :]
    xb = xr.astype(BF16)
    r = _sigmoid(_dot(xb, wa_ref[...]) + ba_ref[...])
    gate_i = _sigmoid(_dot(xb, wx_ref[...]) + bx_ref[...])
    z = -lam_ref[...]
    softplus = jnp.maximum(z, 0.0) + jnp.log(1.0 + jnp.exp(-jnp.abs(z)))
    log_a = (-LRU_C * r) * softplus
    a = jnp.exp(log_a)
    b = jnp.sqrt(1.0 - a * a) * (gate_i * xr)
    rows = lax.broadcasted_iota(jnp.int32, (ts, c), 0)
    shift = 1
    while shift < ts:
        a_prev = pltpu.roll(a, shift, axis=0)
        b_prev = pltpu.roll(b, shift, axis=0)
        live = rows >= shift
        b = jnp.where(live, a * b_prev + b, b)
        a = jnp.where(live, a * a_prev, a)
        shift *= 2
    h = a * hcar_ref[0:1, :] + b
    hcar_ref[...] = jnp.broadcast_to(h[ts - 1:ts, :], hcar_ref.shape)
    ylru = (h * _gelu(ulru_ref[r0:r0 + ts, c:2 * c])).astype(BF16)
    return vconv, ylru


def _compress_kernel(x_ref, wlo_ref, whi_ref, pek_ref, pev_ref, w1k_ref, w1v_ref, w2_ref,
                     kc_ref, vct_ref):
    nc = x_ref.shape[0]
    x = x_ref[...]
    h_lo = _dot(x, wlo_ref[...])
    h_hi = _dot(x, whi_ref[...])
    h_next = pltpu.roll(h_hi, nc - 1, axis=0)
    bk = _dot3(pek_ref[...], w1k_ref[...])[0:1, :]
    bv = _dot3(pev_ref[...], w1v_ref[...])[0:1, :]
    bias = jnp.concatenate([bk, bk, bv, bv], axis=1)
    hid = _gelu(h_lo + h_next + bias).astype(BF16)
    out = _dot(hid, w2_ref[...])
    cend = lax.broadcasted_iota(jnp.int32, (nc, LANES), 0) * CMP_STRIDE + (CMP_LEN - 1)
    lane = lax.broadcasted_iota(jnp.int32, (nc, LANES), 1)
    aug = _position_lanes(cend, lane)
    kc_ref[0] = (out[:, 0:128] + aug).astype(BF16)
    kc_ref[1] = (out[:, 128:256] + aug).astype(BF16)
    vct = out[:, 256:384].T.astype(BF16)
    for t in range(nc // CMP_CHUNK):
        vct_ref[t] = vct[:, t * CMP_CHUNK:(t + 1) * CMP_CHUNK]


def _compress(kvc, p):
    bsz, nc, width = kvc.shape
    x = kvc
    full = lambda shape: pl.BlockSpec(shape, lambda b: (0,) * len(shape))
    return pl.pallas_call(
        _compress_kernel,
        grid=(bsz,),
        in_specs=[pl.BlockSpec((None, nc, width), lambda b: (b, 0, 0)),
                  full((width, 512)), full((width, 512)),
                  full((8, CMP_LEN * HEAD_DIM)), full((8, CMP_LEN * HEAD_DIM)),
                  full((CMP_LEN * HEAD_DIM, CMP_HIDDEN)), full((CMP_LEN * HEAD_DIM, CMP_HIDDEN)),
                  full((512, 384))],
        out_specs=(pl.BlockSpec((None, N_KV, nc, LANES), lambda b: (b, 0, 0, 0)),
                   pl.BlockSpec((None, nc // CMP_CHUNK, LANES, CMP_CHUNK), lambda b: (b, 0, 0, 0))),
        out_shape=(jax.ShapeDtypeStruct((bsz, N_KV, nc, LANES), BF16),
                   jax.ShapeDtypeStruct((bsz, nc // CMP_CHUNK, LANES, CMP_CHUNK), BF16)),
        compiler_params=_params(1),
        name="compress",
    )(x, p["cmp_w_lo"], p["cmp_w_hi"], p["cmp_pe_k"], p["cmp_pe_v"], p["nsa_w_ck1"], p["nsa_w_cv1"],
      p["cmp_w2"])


def _attn_kernel(q_ref, kc_ref, vct_ref, ovt_ref, ks_ref, vst_ref, kw_ref, vwt_ref, gt_ref,
                 o_ref, s0_ref, s1_ref, w_ref, selneg_ref, m_ref, l_ref, acc_ref, flag_ref, list_ref):
    qb = pl.program_id(2)
    q0 = qb * Q_BLOCK
    n_sel = ovt_ref.shape[1]
    n_rows = GROUP * Q_BLOCK
    qg = jnp.concatenate([q_ref[:, r * LANES:(r + 1) * LANES] for r in range(GROUP)], axis=0)
    qpos = q0 + (lax.broadcasted_iota(jnp.int32, (1, n_rows), 1) & (Q_BLOCK - 1))

    n_chunks = ((q0 + Q_BLOCK - CMP_LEN) // CMP_STRIDE + CMP_CHUNK) // CMP_CHUNK
    cend = lax.broadcasted_iota(jnp.int32, (CMP_CHUNK, n_rows), 0) * CMP_STRIDE + (CMP_LEN - 1)

    def cmp_branch(n_ch):
        def fn():
            s_all = _dot_nt(kc_ref[0:n_ch * CMP_CHUNK, :], qg)
            scores = []
            m_c = None
            for ci in range(n_ch):
                s = s_all[ci * CMP_CHUNK:(ci + 1) * CMP_CHUNK, :]
                if ci >= n_ch - 2:
                    s = jnp.where(cend + ci * CMP_CHUNK * CMP_STRIDE <= qpos, s, NEG)
                scores.append(s)
                mi = jnp.max(s, axis=0, keepdims=True)
                m_c = mi if m_c is None else jnp.maximum(m_c, mi)
            m_c = jnp.where(m_c > 0.5 * NEG, m_c, 0.0)
            probs = [jnp.exp2(s - m_c) for s in scores]
            den = probs[0].sum(axis=0, keepdims=True)
            for p in probs[1:]:
                den = den + jnp.sum(p, axis=0, keepdims=True)
            inv_den = 1.0 / jnp.where(den > 0, den, 1.0)
            o_c = None
            imp = None
            for ci, p in enumerate(probs):
                p = p * inv_den
                part = _dot(vct_ref[ci], p.astype(BF16))
                o_c = part if o_c is None else o_c + part
                p_sum = p[:, 0:Q_BLOCK]
                for r in range(1, GROUP):
                    p_sum = p_sum + p[:, r * Q_BLOCK:(r + 1) * Q_BLOCK]
                p_hi, p_lo = _split_bf16(p_sum)
                part = _dot(ovt_ref[ci], p_hi) + _dot(ovt_ref[ci], p_lo)
                imp = part if imp is None else imp + part
            return o_c, imp
        return fn

    o_c, imp = lax.switch(n_chunks - 1, [cmp_branch(n) for n in range(1, kc_ref.shape[0] // CMP_CHUNK + 1)])

    blk = lax.broadcasted_iota(jnp.int32, (n_sel, LANES), 0)
    works = []
    for h in range(Q_BLOCK // LANES):
        qp = q0 + h * LANES + lax.broadcasted_iota(jnp.int32, (n_sel, LANES), 1)
        qid = qp // SEL_BLOCK
        forced = (blk == 0) | (blk == qid) | (blk == qid - 1)
        valid = blk * SEL_BLOCK <= qp
        works.append(jnp.where(forced, -jnp.inf, jnp.where(valid, imp[:, h * LANES:(h + 1) * LANES], -1.0)))
    for _ in range(min(SEL_TOPK, n_sel) - 3):
        for h, work_h in enumerate(works):
            best = jnp.max(work_h, axis=0, keepdims=True)
            first = jnp.min(jnp.where(work_h == best, blk, n_sel), axis=0, keepdims=True)
            works[h] = jnp.where(blk == first, -jnp.inf, work_h)
    work = jnp.concatenate(works, axis=1)

    n_tiles = q0 // KEY_TILE + 1
    krow = lax.broadcasted_iota(jnp.int32, (SEL_BLOCK, n_rows), 0)

    def scores_into(buf_ref, kt):
        k0 = pl.multiple_of(kt * KEY_TILE, KEY_TILE)
        buf_ref[...] = _dot_nt(ks_ref[pl.ds(k0, KEY_TILE), :], qg)

    k_last = pl.multiple_of((n_tiles - 1) * KEY_TILE, KEY_TILE)
    s_both = _dot_nt(jnp.concatenate([ks_ref[pl.ds(k_last, KEY_TILE), :], ks_ref[0:LANES, :]], axis=0), qg)
    s0_ref[...] = s_both[0:KEY_TILE, :]
    hrow = lax.broadcasted_iota(jnp.int32, (LANES, n_rows), 0)
    s_first = jnp.where((hrow < SEL_BLOCK) & (hrow <= qpos), s_both[KEY_TILE:KEY_TILE + LANES, :], NEG)
    m_first = jnp.max(s_first, axis=0, keepdims=True)
    p_first = jnp.exp2(s_first - m_first)
    m_ref[...] = m_first
    l_ref[...] = jnp.sum(p_first, axis=0, keepdims=True)
    acc_ref[...] = _dot(vst_ref[0, :, 0:LANES], p_first.astype(BF16))

    n_win = (WINDOW + Q_BLOCK) // WIN_TILE
    t0 = jnp.maximum(q0 - WINDOW, 0) // WIN_TILE
    kstart = pl.multiple_of(t0 * WIN_TILE, WIN_TILE)
    w_ref[...] = _dot_nt(kw_ref[pl.ds(kstart, n_win * WIN_TILE), :], qg)
    wrow = lax.broadcasted_iota(jnp.int32, (WIN_TILE, n_rows), 0)
    qrel = qpos - kstart

    def win_scores(t):
        dist = qrel - (t * WIN_TILE + wrow)
        visible = (dist >= 0) & (dist < WINDOW) if t * WIN_TILE < Q_BLOCK else dist >= 0
        return jnp.where(visible, w_ref[t * WIN_TILE:(t + 1) * WIN_TILE, :], NEG)

    top = None
    for t in range(n_win):
        mt = _fold8(win_scores(t), jnp.maximum)
        top = mt if top is None else jnp.maximum(top, mt)
    m_w = jnp.max(top, axis=0, keepdims=True)
    total = None
    o_w = None
    for t in range(n_win):
        pt = jnp.exp2(win_scores(t) - m_w)
        lt = _fold8(pt, jnp.add)
        total = lt if total is None else total + lt
        part = _dot(vwt_ref[t0 + t], pt.astype(BF16))
        o_w = part if o_w is None else o_w + part
    o_w = o_w / jnp.sum(total, axis=0, keepdims=True)

    taken = (work == -jnp.inf) & (lax.broadcasted_iota(jnp.int32, (n_sel, Q_BLOCK), 0) > 0)
    chosen = jnp.where(taken, 1.0, 0.0)
    selneg = jnp.where(taken, 0.0, NEG)
    selneg_ref[...] = jnp.concatenate([selneg] * GROUP, axis=1)
    blocks_per_tile = KEY_TILE // SEL_BLOCK
    any_q = jnp.max(chosen, axis=1, keepdims=True)
    for t in range(n_sel // blocks_per_tile):
        tile_any = jnp.max(any_q[t * blocks_per_tile:(t + 1) * blocks_per_tile, :])
        flag_ref[t] = (tile_any > 0).astype(jnp.int32)

    def compact(kt, n):
        @pl.when(flag_ref[kt] > 0)
        def _():
            list_ref[n] = kt
        return n + flag_ref[kt]

    n_act = lax.fori_loop(0, n_tiles - 1, compact, 0)
    list_ref[n_act] = 0

    def sel_tile(buf_ref, kt, causal, n_blocks=KEY_TILE // SEL_BLOCK):
        sel_rows = selneg_ref[pl.ds(pl.multiple_of(kt * blocks_per_tile, blocks_per_tile),
                                    blocks_per_tile), :]

        def block_scores(j):
            sj = buf_ref[j * SEL_BLOCK:(j + 1) * SEL_BLOCK, :]
            if causal:
                sj = jnp.where(kt * KEY_TILE + j * SEL_BLOCK + krow <= qpos, sj, NEG)
            return sj

        m_old = m_ref[...]
        top = None
        for j in range(n_blocks):
            mj = _fold8(block_scores(j), jnp.maximum) + sel_rows[j:j + 1, :]
            top = mj if top is None else jnp.maximum(top, mj)
        m_new = jnp.maximum(m_old, jnp.max(top, axis=0, keepdims=True))
        alpha = jnp.exp2(m_old - m_new)
        total = None
        probs = []
        for j in range(n_blocks):
            pj = jnp.exp2(block_scores(j) - (m_new - sel_rows[j:j + 1, :]))
            lj = _fold8(pj, jnp.add)
            total = lj if total is None else total + lj
            probs.append(pj.astype(BF16))
        m_ref[...] = m_new
        l_ref[...] = alpha * l_ref[...] + jnp.sum(total, axis=0, keepdims=True)
        acc_ref[...] = alpha * acc_ref[...] + _dot(vst_ref[kt, :, 0:n_blocks * SEL_BLOCK],
                                                   jnp.concatenate(probs, axis=0))

    half_tile = (q0 % KEY_TILE) + Q_BLOCK <= KEY_TILE // 2

    @pl.when(half_tile)
    def _():
        scores_into(s1_ref, list_ref[0])
        sel_tile(s0_ref, n_tiles - 1, True, n_blocks=KEY_TILE // SEL_BLOCK // 2)

    @pl.when(jnp.logical_not(half_tile))
    def _():
        scores_into(s1_ref, list_ref[0])
        sel_tile(s0_ref, n_tiles - 1, True)

    def pipelined(i, carry):
        scores_into(s0_ref, list_ref[2 * i + 1])
        sel_tile(s1_ref, list_ref[2 * i], False)

        @pl.when(2 * i + 1 < n_act)
        def _():
            scores_into(s1_ref, list_ref[2 * i + 2])
            sel_tile(s0_ref, list_ref[2 * i + 1], False)
        return carry

    lax.fori_loop(0, (n_act + 1) // 2, pipelined, 0)
    o_s = acc_ref[...] / l_ref[...]

    def gate(branch):
        return jnp.concatenate([gt_ref[branch * GROUP + r:branch * GROUP + r + 1, :] for r in range(GROUP)],
                               axis=1)
    o = gate(0) * o_c + gate(1) * o_s + gate(2) * o_w
    o_ref[...] = jnp.concatenate([o[:, r * Q_BLOCK:(r + 1) * Q_BLOCK].T for r in range(GROUP)],
                                 axis=1).astype(BF16)


def _attention(q, kc, vct, ovt, ks, vst, kw, vwt, gt):
    bsz, s, _ = q.shape
    nc = kc.shape[2]
    n_sel = s // SEL_BLOCK
    n_qb = s // Q_BLOCK
    n_kt = s // KEY_TILE
    n_wt = s // WIN_TILE
    n_ch = nc // CMP_CHUNK
    n_rows = GROUP * Q_BLOCK
    assert s % KEY_TILE == 0 and s >= (WINDOW + Q_BLOCK) and n_sel % 8 == 0 and nc % CMP_CHUNK == 0
    assert KEY_TILE % Q_BLOCK == 0 and Q_BLOCK % WIN_TILE == 0
    return pl.pallas_call(
        _attn_kernel,
        grid=(bsz, N_KV, n_qb),
        in_specs=[pl.BlockSpec((None, Q_BLOCK, GROUP * LANES), lambda b, g, i: (b, i, g)),
                  pl.BlockSpec((None, None, nc, LANES), lambda b, g, i: (b, g, 0, 0)),
                  pl.BlockSpec((None, n_ch, HEAD_DIM, CMP_CHUNK), lambda b, g, i: (b, 0, g, 0)),
                  pl.BlockSpec((n_ch, n_sel, CMP_CHUNK), lambda b, g, i: (0, 0, 0)),
                  pl.BlockSpec((None, None, s, LANES), lambda b, g, i: (b, g, 0, 0)),
                  pl.BlockSpec((None, n_kt, HEAD_DIM, KEY_TILE), lambda b, g, i: (b, 0, g, 0)),
                  pl.BlockSpec((None, None, s, LANES), lambda b, g, i: (b, g, 0, 0)),
                  pl.BlockSpec((None, n_wt, HEAD_DIM, WIN_TILE), lambda b, g, i: (b, 0, g, 0)),
                  pl.BlockSpec((None, 16, Q_BLOCK), lambda b, g, i: (b, g, i))],
        out_specs=pl.BlockSpec((None, Q_BLOCK, GROUP * HEAD_DIM), lambda b, g, i: (b, i, g)),
        out_shape=jax.ShapeDtypeStruct((bsz, s, N_HEADS * HEAD_DIM), BF16),
        scratch_shapes=[pltpu.VMEM((KEY_TILE, n_rows), F32),
                        pltpu.VMEM((KEY_TILE, n_rows), F32),
                        pltpu.VMEM((WINDOW + Q_BLOCK, n_rows), F32),
                        pltpu.VMEM((n_sel, n_rows), F32),
                        pltpu.VMEM((1, n_rows), F32),
                        pltpu.VMEM((1, n_rows), F32),
                        pltpu.VMEM((HEAD_DIM, n_rows), F32),
                        pltpu.SMEM((n_kt,), jnp.int32),
                        pltpu.SMEM((n_kt + 1,), jnp.int32)],
        compiler_params=_params(3),
        name="nsa_attention",
    )(q, kc, vct, ovt, ks, vst, kw, vwt, gt)


def _merge_kernel(x_ref, g_ref, sc_ref, sh_ref, gate_ref, uconv_ref, ulru_ref, onsa_ref,
                  wdw_ref, bdw_ref, lng_ref, lnb_ref, wc4_ref, bc4_ref, wa_ref, ba_ref, wx_ref, bx_ref,
                  lam_ref, wm_ref, bm_ref, wc_ref, wn_ref, wl_ref, wo_ref, o_ref,
                  vext_ref, vsh_ref, xext_ref, hcar_ref):
    tm, d = x_ref.shape
    c = CONV_DIM

    @pl.when(pl.program_id(1) == 0)
    def _():
        vext_ref[0:CONV_HALO, :] = jnp.zeros((CONV_HALO, c), F32)
        xext_ref[0:LRU_HALO, :] = jnp.zeros((LRU_HALO, c), F32)
        hcar_ref[...] = jnp.zeros(hcar_ref.shape, F32)

    x = x_ref[...]
    hb = _mod_norm(x, g_ref[...], sc_ref[...], sh_ref[...]).astype(BF16)

    def merge_gate(k):
        return _sigmoid(_dot(hb, wm_ref[:, k * d:(k + 1) * d]) + bm_ref[:, k * d:(k + 1) * d])

    halves = []
    early = []
    for r0 in range(0, tm, BRANCH_TILE):
        halves.append(_branch_rows(r0, BRANCH_TILE, uconv_ref, ulru_ref, wdw_ref, bdw_ref, lng_ref, lnb_ref,
                                   wc4_ref, bc4_ref, wa_ref, ba_ref, wx_ref, bx_ref, lam_ref,
                                   vext_ref, vsh_ref, xext_ref, hcar_ref))
        if not early:
            early.append(merge_gate(1) * _dot(onsa_ref[...], wn_ref[...]))
        else:
            early.append(merge_gate(0))
    merged = early[0] + early[1] * _dot(jnp.concatenate([h[0] for h in halves], axis=0), wc_ref[...])
    merged = merged + merge_gate(2) * _dot(jnp.concatenate([h[1] for h in halves], axis=0), wl_ref[...])
    o_ref[...] = x + gate_ref[...] * _dot(merged.astype(BF16), wo_ref[...])


def _merge(x, mod, layer, norm_g, uconv, ulru, onsa, p):
    bsz, s, d = x.shape
    tm = TOKEN_TILE
    ts = BRANCH_TILE
    c = CONV_DIM
    assert tm % ts == 0
    row = lambda k: pl.BlockSpec((None, None, None, 1, d), lambda b, i: (layer, b, k, 0, 0))
    full = lambda shape: pl.BlockSpec(shape, lambda b, i: (0,) * len(shape))
    tok = lambda w: pl.BlockSpec((None, tm, w), lambda b, i: (b, i, 0))
    return pl.pallas_call(
        _merge_kernel,
        grid=(bsz, s // tm),
        in_specs=[tok(d), full((1, d)), row(1), row(0), row(2), tok(2 * c), tok(2 * c), tok(c),
                  full((CONV_WIDTH, c)), full((1, c)), full((1, c)), full((1, c)),
                  full((LRU_CONV_WIDTH, c)), full((1, c)),
                  full((c, c)), full((1, c)), full((c, c)), full((1, c)), full((1, c)),
                  full((d, N_BRANCH * d)), full((1, N_BRANCH * d)),
                  full((c, d)), full((c, d)), full((c, d)), full((d, d))],
        out_specs=tok(d),
        out_shape=jax.ShapeDtypeStruct((bsz, s, d), F32),
        scratch_shapes=[pltpu.VMEM((ts + CONV_HALO, c), F32),
                        pltpu.VMEM((7, ts + CONV_HALO - 8, c), F32),
                        pltpu.VMEM((ts + LRU_HALO, c), F32),
                        pltpu.VMEM((8, c), F32)],
        compiler_params=_params(2),
        name="merge",
    )(x, norm_g, mod, mod, mod, uconv, ulru, onsa,
      p["conv_w_dw"], p["conv_b_dw"], p["conv_ln_g"], p["conv_ln_b"],
      p["lru_w_conv"], p["lru_b_conv"], p["lru_wa_bd"], p["lru_b_a"], p["lru_wx_bd"], p["lru_b_x"],
      p["lru_lam"], p["w_merge"], p["b_merge"],
      p["conv_w_out"], p["nsa_w_out"], p["lru_w_out"], p["w_out"])


def _ffn_kernel(x_ref, g_ref, sc_ref, sh_ref, gate_ref, wa_ref, wb_ref, wo_ref, fin_ref, o_ref,
                *, chunk, final):
    x = x_ref[...]
    hb = _mod_norm(x, g_ref[...], sc_ref[...], sh_ref[...]).astype(BF16)
    ffn = wa_ref.shape[1]
    acc = None
    for c0 in range(0, ffn, chunk):
        a = _dot(hb, wa_ref[:, c0:c0 + chunk])
        b = _dot(hb, wb_ref[:, c0:c0 + chunk])
        part = _dot((_silu(a) * b).astype(BF16), wo_ref[c0:c0 + chunk, :])
        acc = part if acc is None else acc + part
    y = x + gate_ref[...] * acc
    if final:
        y = (y * lax.rsqrt(jnp.mean(y * y, axis=-1, keepdims=True) + EPS)) * fin_ref[...]
    o_ref[...] = y


def _ffn(x, mod, layer, norm_g, p, final_norm, final):
    bsz, s, d = x.shape
    tm = TOKEN_TILE
    ffn = p["w_ffn_a"].shape[1]
    chunk = 256
    assert ffn % chunk == 0
    row = lambda k: pl.BlockSpec((None, None, None, 1, d), lambda b, i: (layer, b, k, 0, 0))
    full = lambda shape: pl.BlockSpec(shape, lambda b, i: (0,) * len(shape))
    tok = pl.BlockSpec((None, tm, d), lambda b, i: (b, i, 0))
    return pl.pallas_call(
        functools.partial(_ffn_kernel, chunk=chunk, final=final),
        grid=(bsz, s // tm),
        in_specs=[tok, full((1, d)), row(4), row(3), row(5),
                  full((d, ffn)), full((d, ffn)), full((ffn, d)), full((1, d))],
        out_specs=tok,
        out_shape=jax.ShapeDtypeStruct((bsz, s, d), F32),
        compiler_params=_params(2),
        name="ffn",
    )(x, norm_g, mod, mod, mod, p["w_ffn_a"], p["w_ffn_b"], p["w_ffn_out"], final_norm)


def _pack_input_projection(w_in):
    d = w_in.shape[0]
    q0, kv0, gate0, lru0 = 1024, 1536, 2304, 2328

    def head_slots(w, n):
        w = w.reshape(d, n, HEAD_DIM)
        return jnp.pad(w, ((0, 0), (0, 0), (0, LANES - HEAD_DIM))).reshape(d, n * LANES)

    kv = lambda i: w_in[:, kv0 + i * 128:kv0 + (i + 1) * 128]
    gates = w_in[:, gate0:gate0 + N_KV * GROUP * 3].reshape(d, N_KV, GROUP, 3)
    gates = jnp.pad(gates.transpose(0, 1, 3, 2).reshape(d, N_KV, 3 * GROUP), ((0, 0), (0, 0), (0, 4)))
    gates = jnp.pad(gates.reshape(d, N_KV * 16), ((0, 0), (0, LANES - N_KV * 16)))
    return jnp.concatenate([
        w_in[:, 0:q0],
        head_slots(w_in[:, q0:kv0] * (HEAD_DIM ** -0.5 * LOG2E), N_HEADS),
        kv(0), kv(1),
        head_slots(kv(2), N_KV), head_slots(kv(4), N_KV),
        kv(3), kv(5), gates,
        w_in[:, lru0:lru0 + 2 * LRU_DIM]], axis=1)


def _bf16_pieces(x):
    pieces = []
    for _ in range(SLOPE_PIECES):
        piece = float(np.asarray(x, np.float32).astype(BF16).astype(np.float64))
        pieces.append(piece)
        x = x - piece
    return pieces


def _q_slope_row():
    row = np.zeros((1, N_HEADS * LANES), np.float32)
    for h in range(N_HEADS):
        slope = 2.0 ** (-8.0 * (h + 1) / N_HEADS) * LOG2E
        base = h * LANES + HEAD_DIM
        row[0, base:base + SLOPE_PIECES] = _bf16_pieces(slope * 128.0)
        row[0, base + SLOPE_PIECES:base + 2 * SLOPE_PIECES] = _bf16_pieces(slope)
    return row


def _overlap_matrix(s):
    nc = s // CMP_STRIDE
    n_sel = s // SEL_BLOCK
    cs = np.arange(nc)[:, None] * CMP_STRIDE
    ss = np.arange(n_sel)[None, :] * SEL_BLOCK
    ov = np.clip(np.minimum(cs + CMP_LEN, ss + SEL_BLOCK) - np.maximum(cs, ss), 0, None) / CMP_LEN
    ovt = ov.T.astype(np.float32).reshape(n_sel, nc // CMP_CHUNK, CMP_CHUNK)
    return np.ascontiguousarray(ovt.transpose(1, 0, 2))


def _block_diag(w):
    heads, n, _ = w.shape
    eye = jnp.asarray(np.eye(heads, dtype=np.float32))
    return (w[:, :, None, :] * eye[:, None, :, None]).reshape(heads * n, heads * n)


def _layer_params(l, a):
    w_pack = _pack_input_projection(a["w_in"][l])
    assert w_pack.shape[1] == COL_END
    row = lambda v: v[l].reshape(1, -1)

    def chunk_weights(half):
        wk = a["nsa_w_ck1"][l].reshape(CMP_LEN, HEAD_DIM, CMP_HIDDEN)[half * CMP_STRIDE:(half + 1) * CMP_STRIDE]
        wv = a["nsa_w_cv1"][l].reshape(CMP_LEN, HEAD_DIM, CMP_HIDDEN)[half * CMP_STRIDE:(half + 1) * CMP_STRIDE]
        k_slots = jnp.asarray(np.diag([1.0, 1.0, 0.0, 0.0]).astype(np.float32))
        v_slots = jnp.asarray(np.diag([0.0, 0.0, 1.0, 1.0]).astype(np.float32))
        w = (wk[:, None, :, None, :] * k_slots[None, :, None, :, None]
             + wv[:, None, :, None, :] * v_slots[None, :, None, :, None])
        return w.reshape(CMP_STRIDE * 256, 4 * CMP_HIDDEN).astype(BF16)

    eye2 = jnp.asarray(np.eye(N_KV, dtype=np.float32))
    ck2 = jnp.pad(a["nsa_w_ck2"][l], ((0, 0), (0, LANES - HEAD_DIM)))
    block2 = lambda w: (w[None, :, None, :] * eye2[:, None, :, None]).reshape(N_KV * w.shape[0], N_KV * w.shape[1])
    w2_k = block2(ck2)
    w2_v = block2(a["nsa_w_cv2"][l])
    w2 = jnp.concatenate([jnp.pad(w2_k, ((0, 0), (0, N_KV * HEAD_DIM))),
                          jnp.pad(w2_v, ((0, 0), (N_KV * LANES, 0)))], axis=0)
    pe_rows = lambda pe: jnp.pad(pe.reshape(1, -1), ((0, 7), (0, 0)))
    ffn = a["w_ffn_out"].shape[1]
    return {
        "w_pack": w_pack.astype(BF16),
        "conv_w_dw": a["conv_w_dw"][l], "conv_b_dw": row(a["conv_b_dw"]),
        "conv_ln_g": row(a["conv_ln_g"]), "conv_ln_b": row(a["conv_ln_b"]),
        "lru_w_conv": a["lru_w_conv"][l], "lru_b_conv": row(a["lru_b_conv"]),
        "lru_wa_bd": _block_diag(a["lru_w_a"][l]).astype(BF16), "lru_b_a": row(a["lru_b_a"]),
        "lru_wx_bd": _block_diag(a["lru_w_x"][l]).astype(BF16), "lru_b_x": row(a["lru_b_x"]),
        "lru_lam": row(a["lru_lam"]),
        "cmp_w_lo": chunk_weights(0), "cmp_w_hi": chunk_weights(1),
        "cmp_pe_k": pe_rows(a["nsa_pe_k"][l]), "cmp_pe_v": pe_rows(a["nsa_pe_v"][l]),
        "nsa_w_ck1": a["nsa_w_ck1"][l], "nsa_w_cv1": a["nsa_w_cv1"][l],
        "cmp_w2": w2.astype(BF16),
        "w_merge": a["w_merge"][l].astype(BF16), "b_merge": row(a["b_merge"]),
        "conv_w_out": a["conv_w_out"][l].astype(BF16), "nsa_w_out": a["nsa_w_out"][l].astype(BF16),
        "lru_w_out": a["lru_w_out"][l].astype(BF16), "w_out": a["w_out"][l].astype(BF16),
        "w_ffn_a": a["w_ffn_in"][l][:, :ffn].astype(BF16), "w_ffn_b": a["w_ffn_in"][l][:, ffn:].astype(BF16),
        "w_ffn_out": a["w_ffn_out"][l].astype(BF16),
    }


def kernel(x, c, w_mod, b_mod, norm_mix, norm_ffn, w_in, conv_w_dw, conv_b_dw, conv_ln_g, conv_ln_b,
           conv_w_out, nsa_pe_k, nsa_w_ck1, nsa_w_ck2, nsa_pe_v, nsa_w_cv1, nsa_w_cv2, nsa_w_out,
           lru_w_conv, lru_b_conv, lru_w_a, lru_b_a, lru_w_x, lru_b_x, lru_lam, lru_w_out,
           w_merge, b_merge, w_out, w_ffn_in, w_ffn_out, final_norm):
    a = dict(w_in=w_in, conv_w_dw=conv_w_dw, conv_b_dw=conv_b_dw, conv_ln_g=conv_ln_g, conv_ln_b=conv_ln_b,
             conv_w_out=conv_w_out, nsa_pe_k=nsa_pe_k, nsa_w_ck1=nsa_w_ck1, nsa_w_ck2=nsa_w_ck2,
             nsa_pe_v=nsa_pe_v, nsa_w_cv1=nsa_w_cv1, nsa_w_cv2=nsa_w_cv2, nsa_w_out=nsa_w_out,
             lru_w_conv=lru_w_conv, lru_b_conv=lru_b_conv, lru_w_a=lru_w_a, lru_b_a=lru_b_a,
             lru_w_x=lru_w_x, lru_b_x=lru_b_x, lru_lam=lru_lam, lru_w_out=lru_w_out,
             w_merge=w_merge, b_merge=b_merge, w_out=w_out, w_ffn_in=w_ffn_in, w_ffn_out=w_ffn_out)
    depth = w_in.shape[0]
    s = x.shape[1]
    mod = _modulation(c, w_mod, b_mod)
    qaug = jnp.asarray(_q_slope_row())
    ovt = jnp.asarray(_overlap_matrix(s)).astype(BF16)
    fin = final_norm.reshape(1, -1)
    for l in range(depth):
        p = _layer_params(l, a)
        g_mix = norm_mix[l].reshape(1, -1)
        g_ffn = norm_ffn[l].reshape(1, -1)
        uconv, ulru, q, kvc, ks, kw, vst, vwt, gt = _premix(x, mod, l, g_mix, p["w_pack"], qaug)
        kc, vct = _compress(kvc, p)
        onsa = _attention(q, kc, vct, ovt, ks, vst, kw, vwt, gt)
        x = _merge(x, mod, l, g_mix, uconv, ulru, onsa, p)
        x = _ffn(x, mod, l, g_ffn, p, fin, final=(l == depth - 1))
    return x
```

```python
import functools

import numpy as np
import jax
import jax.numpy as jnp
from jax import lax
from jax.experimental import pallas as pl
from jax.experimental.pallas import tpu as pltpu

F32 = jnp.float32
BF16 = jnp.bfloat16

EPS = 1e-6
CONV_DIM = 512
CONV_WIDTH = 31
N_HEADS = 8
N_KV = 2
GROUP = N_HEADS // N_KV
HEAD_DIM = 64
CMP_LEN = 32
CMP_STRIDE = 16
CMP_HIDDEN = 128
SEL_BLOCK = 64
SEL_TOPK = 16
WINDOW = 512
Q_BLOCK = 256
FORCE_SCORE = 1e4
LRU_DIM = 512
LRU_HEADS = 8
LRU_CONV_WIDTH = 4
LRU_C = 8.0
N_BRANCH = 3

LANES = 128
NEG = -1e30
LOG2E = 1.4426950408889634
SLOPE_PIECES = 3
KEY_TILE = 512
WIN_TILE = 128
CMP_CHUNK = 256
TOKEN_TILE = 512
BRANCH_TILE = 256
CONV_HALO = 32
LRU_HALO = 8
VMEM_LIMIT = 56 * 1024 * 1024

COL_CONV = 0
COL_Q = 1024
COL_KVC = 2048
COL_K = 2304
COL_V = 2816
COL_LRU = 3200
COL_END = 4224


def _params(n_grid):
    return pltpu.CompilerParams(dimension_semantics=("arbitrary",) * n_grid,
                                vmem_limit_bytes=VMEM_LIMIT)


def _dot(a, b):
    return jnp.dot(a, b, preferred_element_type=F32)


def _dot_nt(a, b):
    return lax.dot_general(a, b, (((1,), (1,)), ((), ())), preferred_element_type=F32)


def _split_bf16(a):
    hi = a.astype(BF16)
    lo = (a - hi.astype(F32)).astype(BF16)
    return hi, lo


def _dot3(a, b):
    a_hi, a_lo = _split_bf16(a)
    b_hi, b_lo = _split_bf16(b)
    return _dot(a_hi, b_hi) + (_dot(a_hi, b_lo) + _dot(a_lo, b_hi))


def _gelu(x):
    return 0.5 * x * (1.0 + jnp.tanh(0.7978845608028654 * (x + 0.044715 * (x * x * x))))


def _sigmoid(x):
    return 0.5 * jnp.tanh(0.5 * x) + 0.5


def _silu(x):
    return x * _sigmoid(x)


def _fold8(x, op):
    groups = [x[i:i + 8, :] for i in range(0, x.shape[0], 8)]
    while len(groups) > 1:
        groups = [op(groups[i], groups[i + 1]) if i + 1 < len(groups) else groups[i]
                  for i in range(0, len(groups), 2)]
    return groups[0]


def _position_lanes(pos, lane):
    hi = (pos >> 7).astype(F32)
    lo = (pos & 127).astype(F32)
    off = lane - HEAD_DIM
    return jnp.where((off >= 0) & (off < SLOPE_PIECES), hi,
                     jnp.where((off >= SLOPE_PIECES) & (off < 2 * SLOPE_PIECES), lo, 0.0))


def _mod_norm(x, g, sc, sh):
    y = x * lax.rsqrt(jnp.mean(x * x, axis=-1, keepdims=True) + EPS)
    return (y * g) * (1.0 + sc) + sh


def _mod_kernel(c_ref, w_ref, b_ref, o_ref):
    o_ref[...] = _dot3(c_ref[...], w_ref[...]) + b_ref[...]


def _modulation(c, w_mod, b_mod):
    depth, d, n = w_mod.shape
    bsz = c.shape[0]
    rows = 8
    c_pad = jnp.zeros((rows, d), F32).at[:bsz].set(c)
    out = pl.pallas_call(
        _mod_kernel,
        grid=(depth, n // d),
        in_specs=[pl.BlockSpec((rows, d), lambda l, j: (0, 0)),
                  pl.BlockSpec((None, d, d), lambda l, j: (l, 0, j)),
                  pl.BlockSpec((None, 1, d), lambda l, j: (l, 0, j))],
        out_specs=pl.BlockSpec((None, rows, d), lambda l, j: (l, 0, j)),
        out_shape=jax.ShapeDtypeStruct((depth, rows, n), F32),
        compiler_params=_params(2),
        name="modulation",
    )(c_pad, w_mod, b_mod.reshape(depth, 1, n))
    return out[:, :bsz].reshape(depth, bsz, n // d, 1, d)


def _premix_kernel(x_ref, g_ref, sc_ref, sh_ref, w_ref, qaug_ref,
                   wc4_ref, bc4_ref, wa_ref, ba_ref, wx_ref, bx_ref, lam_ref,
                   uconv_ref, ylru_ref, q_ref, kvc_ref, ks_ref, kw_ref, vst_ref, vwt_ref, gt_ref,
                   kvc_tmp_ref, xext_ref, hcar_ref):
    i = pl.program_id(1)
    tm = x_ref.shape[0]

    @pl.when(i == 0)
    def _():
        xext_ref[0:LRU_HALO, :] = jnp.zeros((LRU_HALO, LRU_DIM), F32)
        hcar_ref[...] = jnp.zeros(hcar_ref.shape, F32)

    hb = _mod_norm(x_ref[...], g_ref[...], sc_ref[...], sh_ref[...]).astype(BF16)

    def proj(a, b):
        return _dot(hb, w_ref[:, a:b])

    u_conv = proj(COL_CONV, COL_Q)
    uconv_ref[...] = u_conv[:, 0:CONV_DIM] * _sigmoid(u_conv[:, CONV_DIM:2 * CONV_DIM])
    u_lru = proj(COL_LRU, COL_END)
    for r0 in range(0, tm, BRANCH_TILE):
        ylru_ref[r0:r0 + BRANCH_TILE, :] = _lru_rows(
            u_lru[r0:r0 + BRANCH_TILE, 0:LRU_DIM], _gelu(u_lru[r0:r0 + BRANCH_TILE, LRU_DIM:2 * LRU_DIM]),
            wc4_ref, bc4_ref, wa_ref, ba_ref, wx_ref, bx_ref, lam_ref, xext_ref, hcar_ref)
    q_ref[...] = (proj(COL_Q, COL_KVC) + qaug_ref[...]).astype(BF16)
    ukv = proj(COL_KVC, COL_K)
    for half in range(2):
        kvc_tmp_ref[half] = ukv[:, half * LANES:(half + 1) * LANES]
    for t in range(CMP_STRIDE):
        for half in range(2):
            col = t * 256 + half * LANES
            kvc_ref[:, col:col + LANES] = (
                kvc_tmp_ref[half, pl.ds(t, tm // CMP_STRIDE, stride=CMP_STRIDE), :].astype(BF16))
    pos = i * tm + lax.broadcasted_iota(jnp.int32, (tm, LANES), 0)
    lane = lax.broadcasted_iota(jnp.int32, (tm, LANES), 1)
    kaug = _position_lanes(pos, lane)
    uk = proj(COL_K, COL_V)
    ks_ref[0] = (uk[:, 0:128] + kaug).astype(BF16)
    ks_ref[1] = (uk[:, 128:256] + kaug).astype(BF16)
    kw_ref[0] = (uk[:, 256:384] + kaug).astype(BF16)
    kw_ref[1] = (uk[:, 384:512] + kaug).astype(BF16)
    uv = proj(COL_V, COL_LRU)
    vst_ref[...] = uv[:, 0:128].T.astype(BF16)
    vwt = uv[:, 128:256].T.astype(BF16)
    for t in range(tm // WIN_TILE):
        vwt_ref[t] = vwt[:, t * WIN_TILE:(t + 1) * WIN_TILE]
    gt = _sigmoid(uv[:, 256:384]).T
    gt_ref[...] = gt[0:32, :]


def _premix(x, mod, layer, qaug, p):
    bsz, s, d = x.shape
    tm = TOKEN_TILE
    c = LRU_DIM
    assert tm == KEY_TILE and s % tm == 0 and tm % BRANCH_TILE == 0
    n_t = s // tm
    row = lambda k: pl.BlockSpec((None, None, None, 1, d), lambda b, i: (layer, b, k, 0, 0))
    full = lambda shape: pl.BlockSpec((None,) + shape, lambda b, i: (layer,) + (0,) * len(shape))
    out_shapes = (
        jax.ShapeDtypeStruct((bsz, s, CONV_DIM), F32),
        jax.ShapeDtypeStruct((bsz, s, LRU_DIM), BF16),
        jax.ShapeDtypeStruct((bsz, s, 1024), BF16),
        jax.ShapeDtypeStruct((bsz, s // CMP_STRIDE, CMP_STRIDE * 256), BF16),
        jax.ShapeDtypeStruct((bsz, N_KV, s, LANES), BF16),
        jax.ShapeDtypeStruct((bsz, N_KV, s, LANES), BF16),
        jax.ShapeDtypeStruct((bsz, n_t, LANES, KEY_TILE), BF16),
        jax.ShapeDtypeStruct((bsz, s // WIN_TILE, LANES, WIN_TILE), BF16),
        jax.ShapeDtypeStruct((bsz, 32, s), F32),
    )
    out_specs = (
        pl.BlockSpec((None, tm, CONV_DIM), lambda b, i: (b, i, 0)),
        pl.BlockSpec((None, tm, LRU_DIM), lambda b, i: (b, i, 0)),
        pl.BlockSpec((None, tm, 1024), lambda b, i: (b, i, 0)),
        pl.BlockSpec((None, tm // CMP_STRIDE, CMP_STRIDE * 256), lambda b, i: (b, i, 0)),
        pl.BlockSpec((None, N_KV, tm, LANES), lambda b, i: (b, 0, i, 0)),
        pl.BlockSpec((None, N_KV, tm, LANES), lambda b, i: (b, 0, i, 0)),
        pl.BlockSpec((None, None, LANES, KEY_TILE), lambda b, i: (b, i, 0, 0)),
        pl.BlockSpec((None, tm // WIN_TILE, LANES, WIN_TILE), lambda b, i: (b, i, 0, 0)),
        pl.BlockSpec((None, 32, tm), lambda b, i: (b, 0, i)),
    )
    return pl.pallas_call(
        _premix_kernel,
        grid=(bsz, n_t),
        in_specs=[pl.BlockSpec((None, tm, d), lambda b, i: (b, i, 0)),
                  full((1, d)), row(1), row(0), full((d, COL_END)),
                  pl.BlockSpec((1, 1024), lambda b, i: (0, 0)),
                  full((LRU_CONV_WIDTH, c)), full((1, c)),
                  full((c, c)), full((1, c)), full((c, c)), full((1, c)), full((1, c))],
        out_specs=out_specs,
        out_shape=out_shapes,
        scratch_shapes=[pltpu.VMEM((2, tm, LANES), F32),
                        pltpu.VMEM((BRANCH_TILE + LRU_HALO, c), F32),
                        pltpu.VMEM((8, c), F32)],
        compiler_params=_params(2),
        name="premix",
    )(x, p["norm_mix"], mod, mod, p["w_pack"], qaug,
      p["lru_w_conv"], p["lru_b_conv"], p["lru_wa_bd"], p["lru_b_a"], p["lru_wx_bd"], p["lru_b_x"],
      p["lru_lam"])


def _conv_rows(v, wdw_ref, bdw_ref, lng_ref, lnb_ref, vext_ref, vsh_ref):
    ts, c = v.shape
    vext_ref[CONV_HALO:CONV_HALO + ts, :] = v
    span = ts + CONV_HALO - 8
    for phase in range(1, 8):
        vsh_ref[phase - 1] = vext_ref[pl.ds(phase, span), :]
    acc = jnp.broadcast_to(bdw_ref[...], (ts, c))
    first = CONV_HALO - (CONV_WIDTH - 1)
    for j in range(CONV_WIDTH):
        base, phase = (first + j) // 8 * 8, (first + j) % 8
        rows = vext_ref[base:base + ts, :] if phase == 0 else vsh_ref[phase - 1, base:base + ts, :]
        acc = acc + wdw_ref[j:j + 1, :] * rows
    vext_ref[0:CONV_HALO, :] = v[ts - CONV_HALO:ts, :]
    mu = jnp.mean(acc, axis=-1, keepdims=True)
    cen = acc - mu
    var = jnp.mean(cen * cen, axis=-1, keepdims=True)
    ln = (cen * lax.rsqrt(var + EPS)) * lng_ref[...] + lnb_ref[...]
    return _silu(ln).astype(BF16)


def _lru_rows(ux, gate, wc4_ref, bc4_ref, wa_ref, ba_ref, wx_ref, bx_ref, lam_ref, xext_ref, hcar_ref):
    ts, c = ux.shape
    xext_ref[LRU_HALO:LRU_HALO + ts, :] = ux
    xr = jnp.broadcast_to(bc4_ref[...], (ts, c))
    first = LRU_HALO - (LRU_CONV_WIDTH - 1)
    for j in range(LRU_CONV_WIDTH):
        xr = xr + wc4_ref[j:j + 1, :] * xext_ref[pl.ds(first + j, ts), :]
    xext_ref[0:LRU_HALO, :] = ux[ts - LRU_HALO:ts, :]
    xb = xr.astype(BF16)
    r = _sigmoid(_dot(xb, wa_ref[...]) + ba_ref[...])
    gate_i = _sigmoid(_dot(xb, wx_ref[...]) + bx_ref[...])
    z = -lam_ref[...]
    softplus = jnp.maximum(z, 0.0) + jnp.log(1.0 + jnp.exp(-jnp.abs(z)))
    log_a = (-LRU_C * r) * softplus
    a = jnp.exp(log_a)
    b = jnp.sqrt(1.0 - a * a) * (gate_i * xr)
    rows = lax.broadcasted_iota(jnp.int32, (ts, c), 0)
    shift = 1
    while shift < ts:
        a_prev = pltpu.roll(a, shift, axis=0)
        b_prev = pltpu.roll(b, shift, axis=0)
        live = rows >= shift
        b = jnp.where(live, a * b_prev + b, b)
        a = jnp.where(live, a * a_prev, a)
        shift *= 2
    h = a * hcar_ref[0:1, :] + b
    hcar_ref[...] = jnp.broadcast_to(h[ts - 1:ts, :], hcar_ref.shape)
    return (h * gate).astype(BF16)


def _compress_kernel(x_ref, wlo_ref, whi_ref, pek_ref, pev_ref, w1k_ref, w1v_ref, w2_ref,
                     kc_ref, vct_ref):
    nc = x_ref.shape[0]
    x = x_ref[...]
    h_lo = _dot(x, wlo_ref[...])
    h_hi = _dot(x, whi_ref[...])
    h_next = pltpu.roll(h_hi, nc - 1, axis=0)
    bk = _dot3(pek_ref[...], w1k_ref[...])[0:1, :]
    bv = _dot3(pev_ref[...], w1v_ref[...])[0:1, :]
    bias = jnp.concatenate([bk, bk, bv, bv], axis=1)
    hid = _gelu(h_lo + h_next + bias).astype(BF16)
    out = _dot(hid, w2_ref[...])
    cend = lax.broadcasted_iota(jnp.int32, (nc, LANES), 0) * CMP_STRIDE + (CMP_LEN - 1)
    lane = lax.broadcasted_iota(jnp.int32, (nc, LANES), 1)
    aug = _position_lanes(cend, lane)
    kc_ref[0] = (out[:, 0:128] + aug).astype(BF16)
    kc_ref[1] = (out[:, 128:256] + aug).astype(BF16)
    vct = out[:, 256:384].T.astype(BF16)
    for t in range(nc // CMP_CHUNK):
        vct_ref[t] = vct[:, t * CMP_CHUNK:(t + 1) * CMP_CHUNK]


def _compress(kvc, layer, p):
    bsz, nc, width = kvc.shape
    x = kvc
    full = lambda shape: pl.BlockSpec((None,) + shape, lambda b: (layer,) + (0,) * len(shape))
    return pl.pallas_call(
        _compress_kernel,
        grid=(bsz,),
        in_specs=[pl.BlockSpec((None, nc, width), lambda b: (b, 0, 0)),
                  full((width, 512)), full((width, 512)),
                  full((8, CMP_LEN * HEAD_DIM)), full((8, CMP_LEN * HEAD_DIM)),
                  full((CMP_LEN * HEAD_DIM, CMP_HIDDEN)), full((CMP_LEN * HEAD_DIM, CMP_HIDDEN)),
                  full((512, 384))],
        out_specs=(pl.BlockSpec((None, N_KV, nc, LANES), lambda b: (b, 0, 0, 0)),
                   pl.BlockSpec((None, nc // CMP_CHUNK, LANES, CMP_CHUNK), lambda b: (b, 0, 0, 0))),
        out_shape=(jax.ShapeDtypeStruct((bsz, N_KV, nc, LANES), BF16),
                   jax.ShapeDtypeStruct((bsz, nc // CMP_CHUNK, LANES, CMP_CHUNK), BF16)),
        compiler_params=_params(1),
        name="compress",
    )(x, p["cmp_w_lo"], p["cmp_w_hi"], p["cmp_pe_k"], p["cmp_pe_v"], p["nsa_w_ck1"], p["nsa_w_cv1"],
      p["cmp_w2"])


def _attn_kernel(q_ref, kc_ref, vct_ref, ovt_ref, ks_ref, vst_ref, kw_ref, vwt_ref, gt_ref,
                 o_ref, s0_ref, s1_ref, w_ref, selneg_ref, m_ref, l_ref, acc_ref, flag_ref, list_ref):
    qb = pl.program_id(2)
    q0 = qb * Q_BLOCK
    n_sel = ovt_ref.shape[1]
    n_rows = GROUP * Q_BLOCK
    qg = jnp.concatenate([q_ref[:, r * LANES:(r + 1) * LANES] for r in range(GROUP)], axis=0)
    qpos = q0 + (lax.broadcasted_iota(jnp.int32, (1, n_rows), 1) & (Q_BLOCK - 1))

    n_chunks = ((q0 + Q_BLOCK - CMP_LEN) // CMP_STRIDE + CMP_CHUNK) // CMP_CHUNK
    cend = lax.broadcasted_iota(jnp.int32, (CMP_CHUNK, n_rows), 0) * CMP_STRIDE + (CMP_LEN - 1)

    def cmp_branch(n_ch):
        def fn():
            s_all = _dot_nt(kc_ref[0:n_ch * CMP_CHUNK, :], qg)
            scores = []
            m_c = None
            for ci in range(n_ch):
                s = s_all[ci * CMP_CHUNK:(ci + 1) * CMP_CHUNK, :]
                if ci >= n_ch - 2:
                    s = jnp.where(cend + ci * CMP_CHUNK * CMP_STRIDE <= qpos, s, NEG)
                scores.append(s)
                mi = jnp.max(s, axis=0, keepdims=True)
                m_c = mi if m_c is None else jnp.maximum(m_c, mi)
            m_c = jnp.where(m_c > 0.5 * NEG, m_c, 0.0)
            probs = [jnp.exp2(s - m_c) for s in scores]
            den = probs[0].sum(axis=0, keepdims=True)
            for p in probs[1:]:
                den = den + jnp.sum(p, axis=0, keepdims=True)
            inv_den = 1.0 / jnp.where(den > 0, den, 1.0)
            o_c = None
            imp = None
            for ci, p in enumerate(probs):
                p = p * inv_den
                part = _dot(vct_ref[ci], p.astype(BF16))
                o_c = part if o_c is None else o_c + part
                p_sum = p[:, 0:Q_BLOCK]
                for r in range(1, GROUP):
                    p_sum = p_sum + p[:, r * Q_BLOCK:(r + 1) * Q_BLOCK]
                p_hi, p_lo = _split_bf16(p_sum)
                part = _dot(ovt_ref[ci], p_hi) + _dot(ovt_ref[ci], p_lo)
                imp = part if imp is None else imp + part
            return o_c, imp
        return fn

    o_c, imp = lax.switch(n_chunks - 1, [cmp_branch(n) for n in range(1, kc_ref.shape[0] // CMP_CHUNK + 1)])

    blk = lax.broadcasted_iota(jnp.int32, (n_sel, LANES), 0)
    works = []
    for h in range(Q_BLOCK // LANES):
        qp = q0 + h * LANES + lax.broadcasted_iota(jnp.int32, (n_sel, LANES), 1)
        qid = qp // SEL_BLOCK
        forced = (blk == 0) | (blk == qid) | (blk == qid - 1)
        valid = blk * SEL_BLOCK <= qp
        works.append(jnp.where(forced, -jnp.inf, jnp.where(valid, imp[:, h * LANES:(h + 1) * LANES], -1.0)))
    for _ in range(min(SEL_TOPK, n_sel) - 3):
        for h, work_h in enumerate(works):
            best = jnp.max(work_h, axis=0, keepdims=True)
            first = jnp.min(jnp.where(work_h == best, blk, n_sel), axis=0, keepdims=True)
            works[h] = jnp.where(blk == first, -jnp.inf, work_h)
    work = jnp.concatenate(works, axis=1)

    n_tiles = q0 // KEY_TILE + 1
    krow = lax.broadcasted_iota(jnp.int32, (SEL_BLOCK, n_rows), 0)

    def scores_into(buf_ref, kt):
        k0 = pl.multiple_of(kt * KEY_TILE, KEY_TILE)
        buf_ref[...] = _dot_nt(ks_ref[pl.ds(k0, KEY_TILE), :], qg)

    k_last = pl.multiple_of((n_tiles - 1) * KEY_TILE, KEY_TILE)
    s_both = _dot_nt(jnp.concatenate([ks_ref[pl.ds(k_last, KEY_TILE), :], ks_ref[0:LANES, :]], axis=0), qg)
    s0_ref[...] = s_both[0:KEY_TILE, :]
    hrow = lax.broadcasted_iota(jnp.int32, (LANES, n_rows), 0)
    s_first = jnp.where((hrow < SEL_BLOCK) & (hrow <= qpos), s_both[KEY_TILE:KEY_TILE + LANES, :], NEG)
    m_first = jnp.max(s_first, axis=0, keepdims=True)
    p_first = jnp.exp2(s_first - m_first)
    m_ref[...] = m_first
    l_ref[...] = jnp.sum(p_first, axis=0, keepdims=True)
    acc_ref[...] = _dot(vst_ref[0, :, 0:LANES], p_first.astype(BF16))

    n_win = (WINDOW + Q_BLOCK) // WIN_TILE
    t0 = jnp.maximum(q0 - WINDOW, 0) // WIN_TILE
    kstart = pl.multiple_of(t0 * WIN_TILE, WIN_TILE)
    w_ref[...] = _dot_nt(kw_ref[pl.ds(kstart, n_win * WIN_TILE), :], qg)
    wrow = lax.broadcasted_iota(jnp.int32, (WIN_TILE, n_rows), 0)
    qrel = qpos - kstart

    def win_scores(t):
        dist = qrel - (t * WIN_TILE + wrow)
        visible = (dist >= 0) & (dist < WINDOW) if t * WIN_TILE < Q_BLOCK else dist >= 0
        return jnp.where(visible, w_ref[t * WIN_TILE:(t + 1) * WIN_TILE, :], NEG)

    top = None
    for t in range(n_win):
        mt = _fold8(win_scores(t), jnp.maximum)
        top = mt if top is None else jnp.maximum(top, mt)
    m_w = jnp.max(top, axis=0, keepdims=True)
    total = None
    o_w = None
    for t in range(n_win):
        pt = jnp.exp2(win_scores(t) - m_w)
        lt = _fold8(pt, jnp.add)
        total = lt if total is None else total + lt
        part = _dot(vwt_ref[t0 + t], pt.astype(BF16))
        o_w = part if o_w is None else o_w + part
    o_w = o_w / jnp.sum(total, axis=0, keepdims=True)

    taken = (work == -jnp.inf) & (lax.broadcasted_iota(jnp.int32, (n_sel, Q_BLOCK), 0) > 0)
    chosen = jnp.where(taken, 1.0, 0.0)
    selneg = jnp.where(taken, 0.0, NEG)
    selneg_ref[...] = jnp.concatenate([selneg] * GROUP, axis=1)
    blocks_per_tile = KEY_TILE // SEL_BLOCK
    any_q = jnp.max(chosen, axis=1, keepdims=True)
    for t in range(n_sel // blocks_per_tile):
        tile_any = jnp.max(any_q[t * blocks_per_tile:(t + 1) * blocks_per_tile, :])
        flag_ref[t] = (tile_any > 0).astype(jnp.int32)

    def compact(kt, n):
        @pl.when(flag_ref[kt] > 0)
        def _():
            list_ref[n] = kt
        return n + flag_ref[kt]

    n_act = lax.fori_loop(0, n_tiles - 1, compact, 0)
    list_ref[n_act] = 0

    def sel_tile(buf_ref, kt, causal, n_blocks=KEY_TILE // SEL_BLOCK):
        sel_rows = selneg_ref[pl.ds(pl.multiple_of(kt * blocks_per_tile, blocks_per_tile),
                                    blocks_per_tile), :]

        def block_scores(j):
            sj = buf_ref[j * SEL_BLOCK:(j + 1) * SEL_BLOCK, :]
            if causal:
                sj = jnp.where(kt * KEY_TILE + j * SEL_BLOCK + krow <= qpos, sj, NEG)
            return sj

        m_old = m_ref[...]
        top = None
        for j in range(n_blocks):
            mj = _fold8(block_scores(j), jnp.maximum) + sel_rows[j:j + 1, :]
            top = mj if top is None else jnp.maximum(top, mj)
        m_new = jnp.maximum(m_old, jnp.max(top, axis=0, keepdims=True))
        alpha = jnp.exp2(m_old - m_new)
        total = None
        probs = []
        for j in range(n_blocks):
            pj = jnp.exp2(block_scores(j) - (m_new - sel_rows[j:j + 1, :]))
            lj = _fold8(pj, jnp.add)
            total = lj if total is None else total + lj
            probs.append(pj.astype(BF16))
        m_ref[...] = m_new
        l_ref[...] = alpha * l_ref[...] + jnp.sum(total, axis=0, keepdims=True)
        acc_ref[...] = alpha * acc_ref[...] + _dot(vst_ref[kt, :, 0:n_blocks * SEL_BLOCK],
                                                   jnp.concatenate(probs, axis=0))

    half_tile = (q0 % KEY_TILE) + Q_BLOCK <= KEY_TILE // 2

    @pl.when(half_tile)
    def _():
        scores_into(s1_ref, list_ref[0])
        sel_tile(s0_ref, n_tiles - 1, True, n_blocks=KEY_TILE // SEL_BLOCK // 2)

    @pl.when(jnp.logical_not(half_tile))
    def _():
        scores_into(s1_ref, list_ref[0])
        sel_tile(s0_ref, n_tiles - 1, True)

    def pipelined(i, carry):
        scores_into(s0_ref, list_ref[2 * i + 1])
        sel_tile(s1_ref, list_ref[2 * i], False)

        @pl.when(2 * i + 1 < n_act)
        def _():
            scores_into(s1_ref, list_ref[2 * i + 2])
            sel_tile(s0_ref, list_ref[2 * i + 1], False)
        return carry

    lax.fori_loop(0, (n_act + 1) // 2, pipelined, 0)
    o_s = acc_ref[...] / l_ref[...]

    def gate(branch):
        return jnp.concatenate([gt_ref[branch * GROUP + r:branch * GROUP + r + 1, :] for r in range(GROUP)],
                               axis=1)
    o = gate(0) * o_c + gate(1) * o_s + gate(2) * o_w
    o_ref[...] = jnp.concatenate([o[:, r * Q_BLOCK:(r + 1) * Q_BLOCK].T for r in range(GROUP)],
                                 axis=1).astype(BF16)


def _attention(q, kc, vct, ovt, ks, vst, kw, vwt, gt):
    bsz, s, _ = q.shape
    nc = kc.shape[2]
    n_sel = s // SEL_BLOCK
    n_qb = s // Q_BLOCK
    n_kt = s // KEY_TILE
    n_wt = s // WIN_TILE
    n_ch = nc // CMP_CHUNK
    n_rows = GROUP * Q_BLOCK
    assert s % KEY_TILE == 0 and s >= (WINDOW + Q_BLOCK) and n_sel % 8 == 0 and nc % CMP_CHUNK == 0
    assert KEY_TILE % Q_BLOCK == 0 and Q_BLOCK % WIN_TILE == 0
    return pl.pallas_call(
        _attn_kernel,
        grid=(bsz, N_KV, n_qb),
        in_specs=[pl.BlockSpec((None, Q_BLOCK, GROUP * LANES), lambda b, g, i: (b, i, g)),
                  pl.BlockSpec((None, None, nc, LANES), lambda b, g, i: (b, g, 0, 0)),
                  pl.BlockSpec((None, n_ch, HEAD_DIM, CMP_CHUNK), lambda b, g, i: (b, 0, g, 0)),
                  pl.BlockSpec((n_ch, n_sel, CMP_CHUNK), lambda b, g, i: (0, 0, 0)),
                  pl.BlockSpec((None, None, s, LANES), lambda b, g, i: (b, g, 0, 0)),
                  pl.BlockSpec((None, n_kt, HEAD_DIM, KEY_TILE), lambda b, g, i: (b, 0, g, 0)),
                  pl.BlockSpec((None, None, s, LANES), lambda b, g, i: (b, g, 0, 0)),
                  pl.BlockSpec((None, n_wt, HEAD_DIM, WIN_TILE), lambda b, g, i: (b, 0, g, 0)),
                  pl.BlockSpec((None, 16, Q_BLOCK), lambda b, g, i: (b, g, i))],
        out_specs=pl.BlockSpec((None, Q_BLOCK, GROUP * HEAD_DIM), lambda b, g, i: (b, i, g)),
        out_shape=jax.ShapeDtypeStruct((bsz, s, N_HEADS * HEAD_DIM), BF16),
        scratch_shapes=[pltpu.VMEM((KEY_TILE, n_rows), F32),
                        pltpu.VMEM((KEY_TILE, n_rows), F32),
                        pltpu.VMEM((WINDOW + Q_BLOCK, n_rows), F32),
                        pltpu.VMEM((n_sel, n_rows), F32),
                        pltpu.VMEM((1, n_rows), F32),
                        pltpu.VMEM((1, n_rows), F32),
                        pltpu.VMEM((HEAD_DIM, n_rows), F32),
                        pltpu.SMEM((n_kt,), jnp.int32),
                        pltpu.SMEM((n_kt + 1,), jnp.int32)],
        compiler_params=_params(3),
        name="nsa_attention",
    )(q, kc, vct, ovt, ks, vst, kw, vwt, gt)


def _merge_kernel(x_ref, g_ref, sc_ref, sh_ref, gate_ref, uconv_ref, ylru_ref, onsa_ref,
                  wdw_ref, bdw_ref, lng_ref, lnb_ref, wm_ref, bm_ref, wc_ref, wn_ref, wl_ref, wo_ref,
                  o_ref, vext_ref, vsh_ref):
    tm, d = x_ref.shape

    @pl.when(pl.program_id(1) == 0)
    def _():
        vext_ref[0:CONV_HALO, :] = jnp.zeros((CONV_HALO, CONV_DIM), F32)

    x = x_ref[...]
    hb = _mod_norm(x, g_ref[...], sc_ref[...], sh_ref[...]).astype(BF16)

    def merge_gate(k):
        return _sigmoid(_dot(hb, wm_ref[:, k * d:(k + 1) * d]) + bm_ref[:, k * d:(k + 1) * d])

    early = merge_gate(1) * _dot(onsa_ref[...], wn_ref[...]) + merge_gate(2) * _dot(ylru_ref[...], wl_ref[...])
    vconv = jnp.concatenate(
        [_conv_rows(uconv_ref[r0:r0 + BRANCH_TILE, :], wdw_ref, bdw_ref, lng_ref, lnb_ref, vext_ref, vsh_ref)
         for r0 in range(0, tm, BRANCH_TILE)], axis=0)
    merged = early + merge_gate(0) * _dot(vconv, wc_ref[...])
    o_ref[...] = x + gate_ref[...] * _dot(merged.astype(BF16), wo_ref[...])


def _merge(x, mod, layer, uconv, ylru, onsa, p):
    bsz, s, d = x.shape
    tm = TOKEN_TILE
    ts = BRANCH_TILE
    c = CONV_DIM
    assert tm % ts == 0
    row = lambda k: pl.BlockSpec((None, None, None, 1, d), lambda b, i: (layer, b, k, 0, 0))
    full = lambda shape: pl.BlockSpec((None,) + shape, lambda b, i: (layer,) + (0,) * len(shape))
    tok = lambda w: pl.BlockSpec((None, tm, w), lambda b, i: (b, i, 0))
    return pl.pallas_call(
        _merge_kernel,
        grid=(bsz, s // tm),
        in_specs=[tok(d), full((1, d)), row(1), row(0), row(2), tok(c), tok(c), tok(c),
                  full((CONV_WIDTH, c)), full((1, c)), full((1, c)), full((1, c)),
                  full((d, N_BRANCH * d)), full((1, N_BRANCH * d)),
                  full((c, d)), full((c, d)), full((c, d)), full((d, d))],
        out_specs=tok(d),
        out_shape=jax.ShapeDtypeStruct((bsz, s, d), F32),
        scratch_shapes=[pltpu.VMEM((ts + CONV_HALO, c), F32),
                        pltpu.VMEM((7, ts + CONV_HALO - 8, c), F32)],
        compiler_params=_params(2),
        name="merge",
    )(x, p["norm_mix"], mod, mod, mod, uconv, ylru, onsa,
      p["conv_w_dw"], p["conv_b_dw"], p["conv_ln_g"], p["conv_ln_b"],
      p["w_merge"], p["b_merge"], p["conv_w_out"], p["nsa_w_out"], p["lru_w_out"], p["w_out"])


def _ffn_kernel(x_ref, g_ref, sc_ref, sh_ref, gate_ref, wa_ref, wb_ref, wo_ref, fin_ref, o_ref,
                *, chunk, final):
    x = x_ref[...]
    hb = _mod_norm(x, g_ref[...], sc_ref[...], sh_ref[...]).astype(BF16)
    ffn = wa_ref.shape[1]
    acc = None
    for c0 in range(0, ffn, chunk):
        a = _dot(hb, wa_ref[:, c0:c0 + chunk])
        b = _dot(hb, wb_ref[:, c0:c0 + chunk])
        part = _dot((_silu(a) * b).astype(BF16), wo_ref[c0:c0 + chunk, :])
        acc = part if acc is None else acc + part
    y = x + gate_ref[...] * acc
    if final:
        y = (y * lax.rsqrt(jnp.mean(y * y, axis=-1, keepdims=True) + EPS)) * fin_ref[...]
    o_ref[...] = y


def _ffn(x, mod, layer, norm_g, p, final_norm, final):
    bsz, s, d = x.shape
    tm = TOKEN_TILE
    ffn = p["w_ffn_out"].shape[1]
    chunk = 256
    assert ffn % chunk == 0
    row = lambda k: pl.BlockSpec((None, None, None, 1, d), lambda b, i: (layer, b, k, 0, 0))
    full = lambda shape: pl.BlockSpec((None,) + shape, lambda b, i: (layer,) + (0,) * len(shape))
    half = lambda k: pl.BlockSpec((None, d, ffn), lambda b, i: (layer, 0, k))
    tok = pl.BlockSpec((None, tm, d), lambda b, i: (b, i, 0))
    return pl.pallas_call(
        functools.partial(_ffn_kernel, chunk=chunk, final=final),
        grid=(bsz, s // tm),
        in_specs=[tok, full((1, d)), row(4), row(3), row(5),
                  half(0), half(1), full((ffn, d)), pl.BlockSpec((1, d), lambda b, i: (0, 0))],
        out_specs=tok,
        out_shape=jax.ShapeDtypeStruct((bsz, s, d), F32),
        compiler_params=_params(2),
        name="ffn",
    )(x, norm_g, mod, mod, mod, p["w_ffn_in"], p["w_ffn_in"], p["w_ffn_out"], final_norm)


def _pack_input_projection(w_in):
    lead = w_in.shape[:-1]
    q0, kv0, gate0, lru0 = 1024, 1536, 2304, 2328
    keep = ((0, 0),) * len(lead)

    def head_slots(w, n):
        w = w.reshape(lead + (n, HEAD_DIM))
        return jnp.pad(w, keep + ((0, 0), (0, LANES - HEAD_DIM))).reshape(lead + (n * LANES,))

    kv = lambda i: w_in[..., kv0 + i * 128:kv0 + (i + 1) * 128]
    gates = w_in[..., gate0:gate0 + N_KV * GROUP * 3].reshape(lead + (N_KV, GROUP, 3))
    gates = jnp.swapaxes(gates, -1, -2).reshape(lead + (N_KV, 3 * GROUP))
    gates = jnp.pad(gates, keep + ((0, 0), (0, 4))).reshape(lead + (N_KV * 16,))
    gates = jnp.pad(gates, keep + ((0, LANES - N_KV * 16),))
    return jnp.concatenate([
        w_in[..., 0:q0],
        head_slots(w_in[..., q0:kv0] * (HEAD_DIM ** -0.5 * LOG2E), N_HEADS),
        kv(0), kv(1),
        head_slots(kv(2), N_KV), head_slots(kv(4), N_KV),
        kv(3), kv(5), gates,
        w_in[..., lru0:lru0 + 2 * LRU_DIM]], axis=-1)


def _bf16_pieces(x):
    pieces = []
    for _ in range(SLOPE_PIECES):
        piece = float(np.asarray(x, np.float32).astype(BF16).astype(np.float64))
        pieces.append(piece)
        x = x - piece
    return pieces


def _q_slope_row():
    row = np.zeros((1, N_HEADS * LANES), np.float32)
    for h in range(N_HEADS):
        slope = 2.0 ** (-8.0 * (h + 1) / N_HEADS) * LOG2E
        base = h * LANES + HEAD_DIM
        row[0, base:base + SLOPE_PIECES] = _bf16_pieces(slope * 128.0)
        row[0, base + SLOPE_PIECES:base + 2 * SLOPE_PIECES] = _bf16_pieces(slope)
    return row


def _overlap_matrix(s):
    nc = s // CMP_STRIDE
    n_sel = s // SEL_BLOCK
    cs = np.arange(nc)[:, None] * CMP_STRIDE
    ss = np.arange(n_sel)[None, :] * SEL_BLOCK
    ov = np.clip(np.minimum(cs + CMP_LEN, ss + SEL_BLOCK) - np.maximum(cs, ss), 0, None) / CMP_LEN
    ovt = ov.T.astype(np.float32).reshape(n_sel, nc // CMP_CHUNK, CMP_CHUNK)
    return np.ascontiguousarray(ovt.transpose(1, 0, 2))


def _block_diag(w):
    depth, heads, n, _ = w.shape
    eye = jnp.asarray(np.eye(heads, dtype=np.float32))
    return (w[:, :, :, None, :] * eye[None, :, None, :, None]).reshape(depth, heads * n, heads * n)


def _prepare_params(a):
    depth = a["w_in"].shape[0]
    w_pack = _pack_input_projection(a["w_in"])
    assert w_pack.shape[-1] == COL_END
    row = lambda v: v.reshape(depth, 1, -1)

    k_slots = jnp.asarray(np.diag([1.0, 1.0, 0.0, 0.0]).astype(np.float32))
    v_slots = jnp.asarray(np.diag([0.0, 0.0, 1.0, 1.0]).astype(np.float32))

    def chunk_weights(half):
        rows = slice(half * CMP_STRIDE, (half + 1) * CMP_STRIDE)
        wk = a["nsa_w_ck1"].reshape(depth, CMP_LEN, HEAD_DIM, CMP_HIDDEN)[:, rows]
        wv = a["nsa_w_cv1"].reshape(depth, CMP_LEN, HEAD_DIM, CMP_HIDDEN)[:, rows]
        w = (wk[:, :, None, :, None, :] * k_slots[None, None, :, None, :, None]
             + wv[:, :, None, :, None, :] * v_slots[None, None, :, None, :, None])
        return w.reshape(depth, CMP_STRIDE * 256, 4 * CMP_HIDDEN).astype(BF16)

    eye2 = jnp.asarray(np.eye(N_KV, dtype=np.float32))
    ck2 = jnp.pad(a["nsa_w_ck2"], ((0, 0), (0, 0), (0, LANES - HEAD_DIM)))
    block2 = lambda w: (w[:, None, :, None, :] * eye2[None, :, None, :, None]).reshape(
        depth, N_KV * w.shape[1], N_KV * w.shape[2])
    w2 = jnp.concatenate([jnp.pad(block2(ck2), ((0, 0), (0, 0), (0, N_KV * HEAD_DIM))),
                          jnp.pad(block2(a["nsa_w_cv2"]), ((0, 0), (0, 0), (N_KV * LANES, 0)))], axis=1)
    pe_rows = lambda pe: jnp.pad(pe.reshape(depth, 1, -1), ((0, 0), (0, 7), (0, 0)))
    return {
        "w_pack": w_pack.astype(BF16),
        "norm_mix": row(a["norm_mix"]), "norm_ffn": row(a["norm_ffn"]),
        "conv_w_dw": a["conv_w_dw"], "conv_b_dw": row(a["conv_b_dw"]),
        "conv_ln_g": row(a["conv_ln_g"]), "conv_ln_b": row(a["conv_ln_b"]),
        "lru_w_conv": a["lru_w_conv"], "lru_b_conv": row(a["lru_b_conv"]),
        "lru_wa_bd": _block_diag(a["lru_w_a"]).astype(BF16), "lru_b_a": row(a["lru_b_a"]),
        "lru_wx_bd": _block_diag(a["lru_w_x"]).astype(BF16), "lru_b_x": row(a["lru_b_x"]),
        "lru_lam": row(a["lru_lam"]),
        "cmp_w_lo": chunk_weights(0), "cmp_w_hi": chunk_weights(1),
        "cmp_pe_k": pe_rows(a["nsa_pe_k"]), "cmp_pe_v": pe_rows(a["nsa_pe_v"]),
        "nsa_w_ck1": a["nsa_w_ck1"], "nsa_w_cv1": a["nsa_w_cv1"],
        "cmp_w2": w2.astype(BF16),
        "w_merge": a["w_merge"].astype(BF16), "b_merge": row(a["b_merge"]),
        "conv_w_out": a["conv_w_out"].astype(BF16), "nsa_w_out": a["nsa_w_out"].astype(BF16),
        "lru_w_out": a["lru_w_out"].astype(BF16), "w_out": a["w_out"].astype(BF16),
        "w_ffn_in": a["w_ffn_in"].astype(BF16), "w_ffn_out": a["w_ffn_out"].astype(BF16),
    }


def kernel(x, c, w_mod, b_mod, norm_mix, norm_ffn, w_in, conv_w_dw, conv_b_dw, conv_ln_g, conv_ln_b,
           conv_w_out, nsa_pe_k, nsa_w_ck1, nsa_w_ck2, nsa_pe_v, nsa_w_cv1, nsa_w_cv2, nsa_w_out,
           lru_w_conv, lru_b_conv, lru_w_a, lru_b_a, lru_w_x, lru_b_x, lru_lam, lru_w_out,
           w_merge, b_merge, w_out, w_ffn_in, w_ffn_out, final_norm):
    a = dict(norm_mix=norm_mix, norm_ffn=norm_ffn, w_in=w_in, conv_w_dw=conv_w_dw, conv_b_dw=conv_b_dw, conv_ln_g=conv_ln_g, conv_ln_b=conv_ln_b,
             conv_w_out=conv_w_out, nsa_pe_k=nsa_pe_k, nsa_w_ck1=nsa_w_ck1, nsa_w_ck2=nsa_w_ck2,
             nsa_pe_v=nsa_pe_v, nsa_w_cv1=nsa_w_cv1, nsa_w_cv2=nsa_w_cv2, nsa_w_out=nsa_w_out,
             lru_w_conv=lru_w_conv, lru_b_conv=lru_b_conv, lru_w_a=lru_w_a, lru_b_a=lru_b_a,
             lru_w_x=lru_w_x, lru_b_x=lru_b_x, lru_lam=lru_lam, lru_w_out=lru_w_out,
             w_merge=w_merge, b_merge=b_merge, w_out=w_out, w_ffn_in=w_ffn_in, w_ffn_out=w_ffn_out)
    depth = w_in.shape[0]
    s = x.shape[1]
    mod = _modulation(c, w_mod, b_mod)
    qaug = jnp.asarray(_q_slope_row())
    ovt = jnp.asarray(_overlap_matrix(s)).astype(BF16)
    fin = final_norm.reshape(1, -1)
    p = _prepare_params(a)
    for l in range(depth):
        uconv, ylru, q, kvc, ks, kw, vst, vwt, gt = _premix(x, mod, l, qaug, p)
        kc, vct = _compress(kvc, l, p)
        onsa = _attention(q, kc, vct, ovt, ks, vst, kw, vwt, gt)
        x = _merge(x, mod, l, uconv, ylru, onsa, p)
        x = _ffn(x, mod, l, p["norm_ffn"], p, fin, final=(l == depth - 1))
    return x
```

```python
import functools

import numpy as np
import jax
import jax.numpy as jnp
from jax import lax
from jax.experimental import pallas as pl
from jax.experimental.pallas import tpu as pltpu

F32 = jnp.float32
BF16 = jnp.bfloat16

EPS = 1e-6
CONV_DIM = 512
CONV_WIDTH = 31
N_HEADS = 8
N_KV = 2
GROUP = N_HEADS // N_KV
HEAD_DIM = 64
CMP_LEN = 32
CMP_STRIDE = 16
CMP_HIDDEN = 128
SEL_BLOCK = 64
SEL_TOPK = 16
WINDOW = 512
Q_BLOCK = 256
FORCE_SCORE = 1e4
LRU_DIM = 512
LRU_HEADS = 8
LRU_CONV_WIDTH = 4
LRU_C = 8.0
N_BRANCH = 3

LANES = 128
NEG = -1e30
LOG2E = 1.4426950408889634
SLOPE_PIECES = 3
KEY_TILE = 512
WIN_TILE = 128
CMP_CHUNK = 256
TOKEN_TILE = 512
BRANCH_TILE = 256
CONV_HALO = 32
LRU_HALO = 8
VMEM_LIMIT = 56 * 1024 * 1024

COL_CONV = 0
COL_Q = 1024
COL_KVC = 2048
COL_K = 2304
COL_V = 2816
COL_LRU = 3200
COL_END = 4224


def _params(n_grid):
    return pltpu.CompilerParams(dimension_semantics=("arbitrary",) * n_grid,
                                vmem_limit_bytes=VMEM_LIMIT)


def _dot(a, b):
    return jnp.dot(a, b, preferred_element_type=F32)


def _dot_nt(a, b):
    return lax.dot_general(a, b, (((1,), (1,)), ((), ())), preferred_element_type=F32)


def _split_bf16(a):
    hi = a.astype(BF16)
    lo = (a - hi.astype(F32)).astype(BF16)
    return hi, lo


def _dot3(a, b):
    a_hi, a_lo = _split_bf16(a)
    b_hi, b_lo = _split_bf16(b)
    return _dot(a_hi, b_hi) + (_dot(a_hi, b_lo) + _dot(a_lo, b_hi))


def _gelu(x):
    return 0.5 * x * (1.0 + jnp.tanh(0.7978845608028654 * (x + 0.044715 * (x * x * x))))


def _sigmoid(x):
    return 0.5 * jnp.tanh(0.5 * x) + 0.5


def _silu(x):
    return x * _sigmoid(x)


def _fold8(x, op):
    groups = [x[i:i + 8, :] for i in range(0, x.shape[0], 8)]
    while len(groups) > 1:
        groups = [op(groups[i], groups[i + 1]) if i + 1 < len(groups) else groups[i]
                  for i in range(0, len(groups), 2)]
    return groups[0]


def _position_lanes(pos, lane):
    hi = (pos >> 7).astype(F32)
    lo = (pos & 127).astype(F32)
    off = lane - HEAD_DIM
    return jnp.where((off >= 0) & (off < SLOPE_PIECES), hi,
                     jnp.where((off >= SLOPE_PIECES) & (off < 2 * SLOPE_PIECES), lo, 0.0))


def _mod_norm(x, g, sc, sh):
    y = x * lax.rsqrt(jnp.mean(x * x, axis=-1, keepdims=True) + EPS)
    return (y * g) * (1.0 + sc) + sh


def _mod_kernel(c_ref, w_ref, b_ref, o_ref):
    o_ref[...] = _dot3(c_ref[...], w_ref[...]) + b_ref[...]


def _modulation(c, w_mod, b_mod):
    depth, d, n = w_mod.shape
    bsz = c.shape[0]
    rows = 8
    c_pad = jnp.zeros((rows, d), F32).at[:bsz].set(c)
    out = pl.pallas_call(
        _mod_kernel,
        grid=(depth, n // d),
        in_specs=[pl.BlockSpec((rows, d), lambda l, j: (0, 0)),
                  pl.BlockSpec((None, d, d), lambda l, j: (l, 0, j)),
                  pl.BlockSpec((None, 1, d), lambda l, j: (l, 0, j))],
        out_specs=pl.BlockSpec((None, rows, d), lambda l, j: (l, 0, j)),
        out_shape=jax.ShapeDtypeStruct((depth, rows, n), F32),
        compiler_params=_params(2),
        name="modulation",
    )(c_pad, w_mod, b_mod.reshape(depth, 1, n))
    return out[:, :bsz].reshape(depth, bsz, n // d, 1, d)


def _premix_kernel(x_ref, g_ref, sc_ref, sh_ref, w_ref, qaug_ref,
                   wc4_ref, bc4_ref, wa_ref, ba_ref, wx_ref, bx_ref, lam_ref,
                   uconv_ref, ylru_ref, q_ref, kvc_ref, ks_ref, kw_ref, vst_ref, vwt_ref, gt_ref,
                   kvc_tmp_ref, xext_ref, hcar_ref):
    i = pl.program_id(1)
    tm = x_ref.shape[0]

    @pl.when(i == 0)
    def _():
        xext_ref[0:LRU_HALO, :] = jnp.zeros((LRU_HALO, LRU_DIM), F32)
        hcar_ref[...] = jnp.zeros(hcar_ref.shape, F32)

    hb = _mod_norm(x_ref[...], g_ref[...], sc_ref[...], sh_ref[...]).astype(BF16)

    def proj(a, b):
        return _dot(hb, w_ref[:, a:b])

    u_conv = proj(COL_CONV, COL_Q)
    uconv_ref[...] = u_conv[:, 0:CONV_DIM] * _sigmoid(u_conv[:, CONV_DIM:2 * CONV_DIM])
    u_lru = proj(COL_LRU, COL_END)
    for r0 in range(0, tm, BRANCH_TILE):
        ylru_ref[r0:r0 + BRANCH_TILE, :] = _lru_rows(
            u_lru[r0:r0 + BRANCH_TILE, 0:LRU_DIM], _gelu(u_lru[r0:r0 + BRANCH_TILE, LRU_DIM:2 * LRU_DIM]),
            wc4_ref, bc4_ref, wa_ref, ba_ref, wx_ref, bx_ref, lam_ref, xext_ref, hcar_ref)
    q_ref[...] = (proj(COL_Q, COL_KVC) + qaug_ref[...]).astype(BF16)
    ukv = proj(COL_KVC, COL_K)
    for half in range(2):
        kvc_tmp_ref[half] = ukv[:, half * LANES:(half + 1) * LANES]
    for t in range(CMP_STRIDE):
        for half in range(2):
            col = t * 256 + half * LANES
            kvc_ref[:, col:col + LANES] = (
                kvc_tmp_ref[half, pl.ds(t, tm // CMP_STRIDE, stride=CMP_STRIDE), :].astype(BF16))
    pos = i * tm + lax.broadcasted_iota(jnp.int32, (tm, LANES), 0)
    lane = lax.broadcasted_iota(jnp.int32, (tm, LANES), 1)
    kaug = _position_lanes(pos, lane)
    uk = proj(COL_K, COL_V)
    ks_ref[0] = (uk[:, 0:128] + kaug).astype(BF16)
    ks_ref[1] = (uk[:, 128:256] + kaug).astype(BF16)
    kw_ref[0] = (uk[:, 256:384] + kaug).astype(BF16)
    kw_ref[1] = (uk[:, 384:512] + kaug).astype(BF16)
    uv = proj(COL_V, COL_LRU)
    vst_ref[...] = uv[:, 0:128].T.astype(BF16)
    vwt = uv[:, 128:256].T.astype(BF16)
    for t in range(tm // WIN_TILE):
        vwt_ref[t] = vwt[:, t * WIN_TILE:(t + 1) * WIN_TILE]
    gt = _sigmoid(uv[:, 256:384]).T
    gt_ref[...] = gt[0:32, :]


def _premix(x, mod, layer, qaug, p):
    bsz, s, d = x.shape
    tm = TOKEN_TILE
    c = LRU_DIM
    assert tm == KEY_TILE and s % tm == 0 and tm % BRANCH_TILE == 0
    n_t = s // tm
    row = lambda k: pl.BlockSpec((None, None, None, 1, d), lambda b, i: (layer, b, k, 0, 0))
    full = lambda shape: pl.BlockSpec((None,) + shape, lambda b, i: (layer,) + (0,) * len(shape))
    out_shapes = (
        jax.ShapeDtypeStruct((bsz, s, CONV_DIM), F32),
        jax.ShapeDtypeStruct((bsz, s, LRU_DIM), BF16),
        jax.ShapeDtypeStruct((bsz, s, 1024), BF16),
        jax.ShapeDtypeStruct((bsz, s // CMP_STRIDE, CMP_STRIDE * 256), BF16),
        jax.ShapeDtypeStruct((bsz, N_KV, s, LANES), BF16),
        jax.ShapeDtypeStruct((bsz, N_KV, s, LANES), BF16),
        jax.ShapeDtypeStruct((bsz, n_t, LANES, KEY_TILE), BF16),
        jax.ShapeDtypeStruct((bsz, s // WIN_TILE, LANES, WIN_TILE), BF16),
        jax.ShapeDtypeStruct((bsz, 32, s), F32),
    )
    out_specs = (
        pl.BlockSpec((None, tm, CONV_DIM), lambda b, i: (b, i, 0)),
        pl.BlockSpec((None, tm, LRU_DIM), lambda b, i: (b, i, 0)),
        pl.BlockSpec((None, tm, 1024), lambda b, i: (b, i, 0)),
        pl.BlockSpec((None, tm // CMP_STRIDE, CMP_STRIDE * 256), lambda b, i: (b, i, 0)),
        pl.BlockSpec((None, N_KV, tm, LANES), lambda b, i: (b, 0, i, 0)),
        pl.BlockSpec((None, N_KV, tm, LANES), lambda b, i: (b, 0, i, 0)),
        pl.BlockSpec((None, None, LANES, KEY_TILE), lambda b, i: (b, i, 0, 0)),
        pl.BlockSpec((None, tm // WIN_TILE, LANES, WIN_TILE), lambda b, i: (b, i, 0, 0)),
        pl.BlockSpec((None, 32, tm), lambda b, i: (b, 0, i)),
    )
    return pl.pallas_call(
        _premix_kernel,
        grid=(bsz, n_t),
        in_specs=[pl.BlockSpec((None, tm, d), lambda b, i: (b, i, 0)),
                  full((1, d)), row(1), row(0), full((d, COL_END)),
                  pl.BlockSpec((1, 1024), lambda b, i: (0, 0)),
                  full((LRU_CONV_WIDTH, c)), full((1, c)),
                  full((c, c)), full((1, c)), full((c, c)), full((1, c)), full((1, c))],
        out_specs=out_specs,
        out_shape=out_shapes,
        scratch_shapes=[pltpu.VMEM((2, tm, LANES), F32),
                        pltpu.VMEM((BRANCH_TILE + LRU_HALO, c), F32),
                        pltpu.VMEM((8, c), F32)],
        compiler_params=_params(2),
        name="premix",
    )(x, p["norm_mix"], mod, mod, p["w_pack"], qaug,
      p["lru_w_conv"], p["lru_b_conv"], p["lru_wa_bd"], p["lru_b_a"], p["lru_wx_bd"], p["lru_b_x"],
      p["lru_lam"])


def _conv_rows(v, wdw_ref, bdw_ref, lng_ref, lnb_ref, vext_ref, vsh_ref):
    ts, c = v.shape
    vext_ref[CONV_HALO:CONV_HALO + ts, :] = v
    span = ts + CONV_HALO - 8
    for phase in range(1, 8):
        vsh_ref[phase - 1] = vext_ref[pl.ds(phase, span), :]
    acc = jnp.broadcast_to(bdw_ref[...], (ts, c))
    first = CONV_HALO - (CONV_WIDTH - 1)
    for j in range(CONV_WIDTH):
        base, phase = (first + j) // 8 * 8, (first + j) % 8
        rows = vext_ref[base:base + ts, :] if phase == 0 else vsh_ref[phase - 1, base:base + ts, :]
        acc = acc + wdw_ref[j:j + 1, :] * rows
    vext_ref[0:CONV_HALO, :] = v[ts - CONV_HALO:ts, :]
    mu = jnp.mean(acc, axis=-1, keepdims=True)
    cen = acc - mu
    var = jnp.mean(cen * cen, axis=-1, keepdims=True)
    ln = (cen * lax.rsqrt(var + EPS)) * lng_ref[...] + lnb_ref[...]
    return _silu(ln).astype(BF16)


def _lru_rows(ux, gate, wc4_ref, bc4_ref, wa_ref, ba_ref, wx_ref, bx_ref, lam_ref, xext_ref, hcar_ref):
    ts, c = ux.shape
    xext_ref[LRU_HALO:LRU_HALO + ts, :] = ux
    xr = jnp.broadcast_to(bc4_ref[...], (ts, c))
    first = LRU_HALO - (LRU_CONV_WIDTH - 1)
    for j in range(LRU_CONV_WIDTH):
        xr = xr + wc4_ref[j:j + 1, :] * xext_ref[pl.ds(first + j, ts), :]
    xext_ref[0:LRU_HALO, :] = ux[ts - LRU_HALO:ts, :]
    xb = xr.astype(BF16)
    r = _sigmoid(_dot(xb, wa_ref[...]) + ba_ref[...])
    gate_i = _sigmoid(_dot(xb, wx_ref[...]) + bx_ref[...])
    z = -lam_ref[...]
    softplus = jnp.maximum(z, 0.0) + jnp.log(1.0 + jnp.exp(-jnp.abs(z)))
    log_a = (-LRU_C * r) * softplus
    a = jnp.exp(log_a)
    b = jnp.sqrt(1.0 - a * a) * (gate_i * xr)
    rows = lax.broadcasted_iota(jnp.int32, (ts, c), 0)
    shift = 1
    while shift < ts:
        a_prev = pltpu.roll(a, shift, axis=0)
        b_prev = pltpu.roll(b, shift, axis=0)
        live = rows >= shift
        b = jnp.where(live, a * b_prev + b, b)
        a = jnp.where(live, a * a_prev, a)
        shift *= 2
    h = a * hcar_ref[0:1, :] + b
    hcar_ref[...] = jnp.broadcast_to(h[ts - 1:ts, :], hcar_ref.shape)
    return (h * gate).astype(BF16)


def _compress_kernel(x_ref, wlo_ref, whi_ref, pek_ref, pev_ref, w1k_ref, w1v_ref, w2_ref,
                     kc_ref, vct_ref):
    nc = x_ref.shape[0]
    x = x_ref[...]
    h_lo = _dot(x, wlo_ref[...])
    h_hi = _dot(x, whi_ref[...])
    h_next = pltpu.roll(h_hi, nc - 1, axis=0)
    bk = _dot3(pek_ref[...], w1k_ref[...])[0:1, :]
    bv = _dot3(pev_ref[...], w1v_ref[...])[0:1, :]
    bias = jnp.concatenate([bk, bk, bv, bv], axis=1)
    hid = _gelu(h_lo + h_next + bias).astype(BF16)
    out = _dot(hid, w2_ref[...])
    cend = lax.broadcasted_iota(jnp.int32, (nc, LANES), 0) * CMP_STRIDE + (CMP_LEN - 1)
    lane = lax.broadcasted_iota(jnp.int32, (nc, LANES), 1)
    aug = _position_lanes(cend, lane)
    kc_ref[0] = (out[:, 0:128] + aug).astype(BF16)
    kc_ref[1] = (out[:, 128:256] + aug).astype(BF16)
    vct = out[:, 256:384].T.astype(BF16)
    for t in range(nc // CMP_CHUNK):
        vct_ref[t] = vct[:, t * CMP_CHUNK:(t + 1) * CMP_CHUNK]


def _compress(kvc, layer, p):
    bsz, nc, width = kvc.shape
    x = kvc
    full = lambda shape: pl.BlockSpec((None,) + shape, lambda b: (layer,) + (0,) * len(shape))
    return pl.pallas_call(
        _compress_kernel,
        grid=(bsz,),
        in_specs=[pl.BlockSpec((None, nc, width), lambda b: (b, 0, 0)),
                  full((width, 512)), full((width, 512)),
                  full((8, CMP_LEN * HEAD_DIM)), full((8, CMP_LEN * HEAD_DIM)),
                  full((CMP_LEN * HEAD_DIM, CMP_HIDDEN)), full((CMP_LEN * HEAD_DIM, CMP_HIDDEN)),
                  full((512, 384))],
        out_specs=(pl.BlockSpec((None, N_KV, nc, LANES), lambda b: (b, 0, 0, 0)),
                   pl.BlockSpec((None, nc // CMP_CHUNK, LANES, CMP_CHUNK), lambda b: (b, 0, 0, 0))),
        out_shape=(jax.ShapeDtypeStruct((bsz, N_KV, nc, LANES), BF16),
                   jax.ShapeDtypeStruct((bsz, nc // CMP_CHUNK, LANES, CMP_CHUNK), BF16)),
        compiler_params=_params(1),
        name="compress",
    )(x, p["cmp_w_lo"], p["cmp_w_hi"], p["cmp_pe_k"], p["cmp_pe_v"], p["nsa_w_ck1"], p["nsa_w_cv1"],
      p["cmp_w2"])


def _attn_kernel(q_ref, kc_ref, vct_ref, ovt_ref, ks_ref, vst_ref, kw_ref, vwt_ref, gt_ref,
                 o_ref, s0_ref, s1_ref, w_ref, selneg_ref, m_ref, l_ref, acc_ref, flag_ref, list_ref):
    qb = pl.program_id(2)
    q0 = qb * Q_BLOCK
    n_sel = ovt_ref.shape[1]
    n_rows = GROUP * Q_BLOCK
    qg = jnp.concatenate([q_ref[:, r * LANES:(r + 1) * LANES] for r in range(GROUP)], axis=0)
    qpos = q0 + (lax.broadcasted_iota(jnp.int32, (1, n_rows), 1) & (Q_BLOCK - 1))

    n_chunks = ((q0 + Q_BLOCK - CMP_LEN) // CMP_STRIDE + CMP_CHUNK) // CMP_CHUNK
    cend = lax.broadcasted_iota(jnp.int32, (CMP_CHUNK, n_rows), 0) * CMP_STRIDE + (CMP_LEN - 1)

    def cmp_branch(n_ch):
        def fn():
            s_all = _dot_nt(kc_ref[0:n_ch * CMP_CHUNK, :], qg)
            scores = []
            m_c = None
            for ci in range(n_ch):
                s = s_all[ci * CMP_CHUNK:(ci + 1) * CMP_CHUNK, :]
                if ci >= n_ch - 2:
                    s = jnp.where(cend + ci * CMP_CHUNK * CMP_STRIDE <= qpos, s, NEG)
                scores.append(s)
                mi = jnp.max(s, axis=0, keepdims=True)
                m_c = mi if m_c is None else jnp.maximum(m_c, mi)
            m_c = jnp.where(m_c > 0.5 * NEG, m_c, 0.0)
            probs = [jnp.exp2(s - m_c) for s in scores]
            den = probs[0].sum(axis=0, keepdims=True)
            for p in probs[1:]:
                den = den + jnp.sum(p, axis=0, keepdims=True)
            inv_den = 1.0 / jnp.where(den > 0, den, 1.0)
            o_c = None
            imp = None
            for ci, p in enumerate(probs):
                p = p * inv_den
                part = _dot(vct_ref[ci], p.astype(BF16))
                o_c = part if o_c is None else o_c + part
                p_sum = p[:, 0:Q_BLOCK]
                for r in range(1, GROUP):
                    p_sum = p_sum + p[:, r * Q_BLOCK:(r + 1) * Q_BLOCK]
                p_hi, p_lo = _split_bf16(p_sum)
                part = _dot(ovt_ref[ci], p_hi) + _dot(ovt_ref[ci], p_lo)
                imp = part if imp is None else imp + part
            return o_c, imp
        return fn

    o_c, imp = lax.switch(n_chunks - 1, [cmp_branch(n) for n in range(1, kc_ref.shape[0] // CMP_CHUNK + 1)])

    blk = lax.broadcasted_iota(jnp.int32, (n_sel, LANES), 0)
    blk_f = blk.astype(F32)
    works = []
    for h in range(Q_BLOCK // LANES):
        qp = q0 + h * LANES + lax.broadcasted_iota(jnp.int32, (n_sel, LANES), 1)
        qid = qp // SEL_BLOCK
        forced = (blk == 0) | (blk == qid) | (blk == qid - 1)
        valid = blk * SEL_BLOCK <= qp
        works.append(jnp.where(forced, -jnp.inf, jnp.where(valid, imp[:, h * LANES:(h + 1) * LANES], -1.0)))
    for _ in range(min(SEL_TOPK, n_sel) - 3):
        for h, work_h in enumerate(works):
            best = jnp.max(work_h, axis=0, keepdims=True)
            first = jnp.min(jnp.where(work_h == best, blk_f, float(n_sel)), axis=0, keepdims=True)
            works[h] = jnp.where(blk_f == first, -jnp.inf, work_h)
    work = jnp.concatenate(works, axis=1)

    n_tiles = q0 // KEY_TILE + 1
    krow = lax.broadcasted_iota(jnp.int32, (SEL_BLOCK, n_rows), 0)

    def scores_into(buf_ref, kt):
        k0 = pl.multiple_of(kt * KEY_TILE, KEY_TILE)
        buf_ref[...] = _dot_nt(ks_ref[pl.ds(k0, KEY_TILE), :], qg)

    k_last = pl.multiple_of((n_tiles - 1) * KEY_TILE, KEY_TILE)
    s_both = _dot_nt(jnp.concatenate([ks_ref[pl.ds(k_last, KEY_TILE), :], ks_ref[0:LANES, :]], axis=0), qg)
    s0_ref[...] = s_both[0:KEY_TILE, :]
    hrow = lax.broadcasted_iota(jnp.int32, (LANES, n_rows), 0)
    s_first = jnp.where((hrow < SEL_BLOCK) & (hrow <= qpos), s_both[KEY_TILE:KEY_TILE + LANES, :], NEG)
    m_first = jnp.max(s_first, axis=0, keepdims=True)
    p_first = jnp.exp2(s_first - m_first)
    m_ref[...] = m_first
    l_ref[...] = jnp.sum(p_first, axis=0, keepdims=True)
    acc_ref[...] = _dot(vst_ref[0, :, 0:LANES], p_first.astype(BF16))

    n_win = (WINDOW + Q_BLOCK) // WIN_TILE
    t0 = jnp.maximum(q0 - WINDOW, 0) // WIN_TILE
    kstart = pl.multiple_of(t0 * WIN_TILE, WIN_TILE)
    w_ref[...] = _dot_nt(kw_ref[pl.ds(kstart, n_win * WIN_TILE), :], qg)
    wrow = lax.broadcasted_iota(jnp.int32, (WIN_TILE, n_rows), 0)
    qrel = qpos - kstart

    top = None
    for t in range(n_win):
        newest = qrel - t * WIN_TILE
        visible = wrow <= newest
        if t * WIN_TILE < Q_BLOCK:
            visible = visible & (wrow > newest - WINDOW)
        st = jnp.where(visible, w_ref[t * WIN_TILE:(t + 1) * WIN_TILE, :], NEG)
        w_ref[t * WIN_TILE:(t + 1) * WIN_TILE, :] = st
        mt = _fold8(st, jnp.maximum)
        top = mt if top is None else jnp.maximum(top, mt)
    m_w = jnp.max(top, axis=0, keepdims=True)
    total = None
    o_w = None
    for t in range(n_win):
        pt = jnp.exp2(w_ref[t * WIN_TILE:(t + 1) * WIN_TILE, :] - m_w)
        lt = _fold8(pt, jnp.add)
        total = lt if total is None else total + lt
        part = _dot(vwt_ref[t0 + t], pt.astype(BF16))
        o_w = part if o_w is None else o_w + part
    o_w = o_w / jnp.sum(total, axis=0, keepdims=True)

    taken = (work == -jnp.inf) & (lax.broadcasted_iota(jnp.int32, (n_sel, Q_BLOCK), 0) > 0)
    chosen = jnp.where(taken, 1.0, 0.0)
    selneg = jnp.where(taken, 0.0, NEG)
    selneg_ref[...] = jnp.concatenate([selneg] * GROUP, axis=1)
    blocks_per_tile = KEY_TILE // SEL_BLOCK
    any_q = jnp.max(chosen, axis=1, keepdims=True)
    for t in range(n_sel // blocks_per_tile):
        tile_any = jnp.max(any_q[t * blocks_per_tile:(t + 1) * blocks_per_tile, :])
        flag_ref[t] = (tile_any > 0).astype(jnp.int32)

    def compact(kt, n):
        @pl.when(flag_ref[kt] > 0)
        def _():
            list_ref[n] = kt
        return n + flag_ref[kt]

    n_act = lax.fori_loop(0, n_tiles - 1, compact, 0)
    list_ref[n_act] = 0

    def sel_tile(buf_ref, kt, causal, n_blocks=KEY_TILE // SEL_BLOCK):
        sel_rows = selneg_ref[pl.ds(pl.multiple_of(kt * blocks_per_tile, blocks_per_tile),
                                    blocks_per_tile), :]

        def block_scores(j):
            return buf_ref[j * SEL_BLOCK:(j + 1) * SEL_BLOCK, :]

        m_old = m_ref[...]
        top = None
        for j in range(n_blocks):
            sj = block_scores(j)
            if causal:
                sj = jnp.where(krow <= qpos - (kt * KEY_TILE + j * SEL_BLOCK), sj, NEG)
                buf_ref[j * SEL_BLOCK:(j + 1) * SEL_BLOCK, :] = sj
            mj = _fold8(sj, jnp.maximum) + sel_rows[j:j + 1, :]
            top = mj if top is None else jnp.maximum(top, mj)
        m_new = jnp.maximum(m_old, jnp.max(top, axis=0, keepdims=True))
        alpha = jnp.exp2(m_old - m_new)
        total = None
        probs = []
        for j in range(n_blocks):
            pj = jnp.exp2(block_scores(j) - (m_new - sel_rows[j:j + 1, :]))
            lj = _fold8(pj, jnp.add)
            total = lj if total is None else total + lj
            probs.append(pj.astype(BF16))
        m_ref[...] = m_new
        l_ref[...] = alpha * l_ref[...] + jnp.sum(total, axis=0, keepdims=True)
        acc_ref[...] = alpha * acc_ref[...] + _dot(vst_ref[kt, :, 0:n_blocks * SEL_BLOCK],
                                                   jnp.concatenate(probs, axis=0))

    half_tile = (q0 % KEY_TILE) + Q_BLOCK <= KEY_TILE // 2

    @pl.when(half_tile)
    def _():
        scores_into(s1_ref, list_ref[0])
        sel_tile(s0_ref, n_tiles - 1, True, n_blocks=KEY_TILE // SEL_BLOCK // 2)

    @pl.when(jnp.logical_not(half_tile))
    def _():
        scores_into(s1_ref, list_ref[0])
        sel_tile(s0_ref, n_tiles - 1, True)

    def pipelined(i, carry):
        scores_into(s0_ref, list_ref[2 * i + 1])
        sel_tile(s1_ref, list_ref[2 * i], False)

        @pl.when(2 * i + 1 < n_act)
        def _():
            scores_into(s1_ref, list_ref[2 * i + 2])
            sel_tile(s0_ref, list_ref[2 * i + 1], False)
        return carry

    lax.fori_loop(0, (n_act + 1) // 2, pipelined, 0)
    o_s = acc_ref[...] / l_ref[...]

    def gate(branch):
        return jnp.concatenate([gt_ref[branch * GROUP + r:branch * GROUP + r + 1, :] for r in range(GROUP)],
                               axis=1)
    o = gate(0) * o_c + gate(1) * o_s + gate(2) * o_w
    o_ref[...] = jnp.concatenate([o[:, r * Q_BLOCK:(r + 1) * Q_BLOCK].T for r in range(GROUP)],
                                 axis=1).astype(BF16)


def _attention(q, kc, vct, ovt, ks, vst, kw, vwt, gt):
    bsz, s, _ = q.shape
    nc = kc.shape[2]
    n_sel = s // SEL_BLOCK
    n_qb = s // Q_BLOCK
    n_kt = s // KEY_TILE
    n_wt = s // WIN_TILE
    n_ch = nc // CMP_CHUNK
    n_rows = GROUP * Q_BLOCK
    assert s % KEY_TILE == 0 and s >= (WINDOW + Q_BLOCK) and n_sel % 8 == 0 and nc % CMP_CHUNK == 0
    assert KEY_TILE % Q_BLOCK == 0 and Q_BLOCK % WIN_TILE == 0
    return pl.pallas_call(
        _attn_kernel,
        grid=(bsz, N_KV, n_qb),
        in_specs=[pl.BlockSpec((None, Q_BLOCK, GROUP * LANES), lambda b, g, i: (b, i, g)),
                  pl.BlockSpec((None, None, nc, LANES), lambda b, g, i: (b, g, 0, 0)),
                  pl.BlockSpec((None, n_ch, HEAD_DIM, CMP_CHUNK), lambda b, g, i: (b, 0, g, 0)),
                  pl.BlockSpec((n_ch, n_sel, CMP_CHUNK), lambda b, g, i: (0, 0, 0)),
                  pl.BlockSpec((None, None, s, LANES), lambda b, g, i: (b, g, 0, 0)),
                  pl.BlockSpec((None, n_kt, HEAD_DIM, KEY_TILE), lambda b, g, i: (b, 0, g, 0)),
                  pl.BlockSpec((None, None, s, LANES), lambda b, g, i: (b, g, 0, 0)),
                  pl.BlockSpec((None, n_wt, HEAD_DIM, WIN_TILE), lambda b, g, i: (b, 0, g, 0)),
                  pl.BlockSpec((None, 16, Q_BLOCK), lambda b, g, i: (b, g, i))],
        out_specs=pl.BlockSpec((None, Q_BLOCK, GROUP * HEAD_DIM), lambda b, g, i: (b, i, g)),
        out_shape=jax.ShapeDtypeStruct((bsz, s, N_HEADS * HEAD_DIM), BF16),
        scratch_shapes=[pltpu.VMEM((KEY_TILE, n_rows), F32),
                        pltpu.VMEM((KEY_TILE, n_rows), F32),
                        pltpu.VMEM((WINDOW + Q_BLOCK, n_rows), F32),
                        pltpu.VMEM((n_sel, n_rows), F32),
                        pltpu.VMEM((1, n_rows), F32),
                        pltpu.VMEM((1, n_rows), F32),
                        pltpu.VMEM((HEAD_DIM, n_rows), F32),
                        pltpu.SMEM((n_kt,), jnp.int32),
                        pltpu.SMEM((n_kt + 1,), jnp.int32)],
        compiler_params=_params(3),
        name="nsa_attention",
    )(q, kc, vct, ovt, ks, vst, kw, vwt, gt)


def _merge_kernel(x_ref, g_ref, sc_ref, sh_ref, gate_ref, uconv_ref, ylru_ref, onsa_ref,
                  wdw_ref, bdw_ref, lng_ref, lnb_ref, wm_ref, bm_ref, wc_ref, wn_ref, wl_ref, wo_ref,
                  o_ref, vext_ref, vsh_ref):
    tm, d = x_ref.shape

    @pl.when(pl.program_id(1) == 0)
    def _():
        vext_ref[0:CONV_HALO, :] = jnp.zeros((CONV_HALO, CONV_DIM), F32)

    x = x_ref[...]
    hb = _mod_norm(x, g_ref[...], sc_ref[...], sh_ref[...]).astype(BF16)

    def merge_gate(k):
        return _sigmoid(_dot(hb, wm_ref[:, k * d:(k + 1) * d]) + bm_ref[:, k * d:(k + 1) * d])

    early = merge_gate(1) * _dot(onsa_ref[...], wn_ref[...]) + merge_gate(2) * _dot(ylru_ref[...], wl_ref[...])
    vconv = jnp.concatenate(
        [_conv_rows(uconv_ref[r0:r0 + BRANCH_TILE, :], wdw_ref, bdw_ref, lng_ref, lnb_ref, vext_ref, vsh_ref)
         for r0 in range(0, tm, BRANCH_TILE)], axis=0)
    merged = early + merge_gate(0) * _dot(vconv, wc_ref[...])
    o_ref[...] = x + gate_ref[...] * _dot(merged.astype(BF16), wo_ref[...])


def _merge(x, mod, layer, uconv, ylru, onsa, p):
    bsz, s, d = x.shape
    tm = TOKEN_TILE
    ts = BRANCH_TILE
    c = CONV_DIM
    assert tm % ts == 0
    row = lambda k: pl.BlockSpec((None, None, None, 1, d), lambda b, i: (layer, b, k, 0, 0))
    full = lambda shape: pl.BlockSpec((None,) + shape, lambda b, i: (layer,) + (0,) * len(shape))
    tok = lambda w: pl.BlockSpec((None, tm, w), lambda b, i: (b, i, 0))
    return pl.pallas_call(
        _merge_kernel,
        grid=(bsz, s // tm),
        in_specs=[tok(d), full((1, d)), row(1), row(0), row(2), tok(c), tok(c), tok(c),
                  full((CONV_WIDTH, c)), full((1, c)), full((1, c)), full((1, c)),
                  full((d, N_BRANCH * d)), full((1, N_BRANCH * d)),
                  full((c, d)), full((c, d)), full((c, d)), full((d, d))],
        out_specs=tok(d),
        out_shape=jax.ShapeDtypeStruct((bsz, s, d), F32),
        scratch_shapes=[pltpu.VMEM((ts + CONV_HALO, c), F32),
                        pltpu.VMEM((7, ts + CONV_HALO - 8, c), F32)],
        compiler_params=_params(2),
        name="merge",
    )(x, p["norm_mix"], mod, mod, mod, uconv, ylru, onsa,
      p["conv_w_dw"], p["conv_b_dw"], p["conv_ln_g"], p["conv_ln_b"],
      p["w_merge"], p["b_merge"], p["conv_w_out"], p["nsa_w_out"], p["lru_w_out"], p["w_out"])


def _ffn_kernel(x_ref, g_ref, sc_ref, sh_ref, gate_ref, wa_ref, wb_ref, wo_ref, fin_ref, o_ref,
                *, chunk, final):
    x = x_ref[...]
    hb = _mod_norm(x, g_ref[...], sc_ref[...], sh_ref[...]).astype(BF16)
    ffn = wa_ref.shape[1]
    acc = None
    for c0 in range(0, ffn, chunk):
        a = _dot(hb, wa_ref[:, c0:c0 + chunk])
        b = _dot(hb, wb_ref[:, c0:c0 + chunk])
        part = _dot((_silu(a) * b).astype(BF16), wo_ref[c0:c0 + chunk, :])
        acc = part if acc is None else acc + part
    y = x + gate_ref[...] * acc
    if final:
        y = (y * lax.rsqrt(jnp.mean(y * y, axis=-1, keepdims=True) + EPS)) * fin_ref[...]
    o_ref[...] = y


def _ffn(x, mod, layer, norm_g, p, final_norm, final):
    bsz, s, d = x.shape
    tm = TOKEN_TILE
    ffn = p["w_ffn_out"].shape[1]
    chunk = 256
    assert ffn % chunk == 0
    row = lambda k: pl.BlockSpec((None, None, None, 1, d), lambda b, i: (layer, b, k, 0, 0))
    full = lambda shape: pl.BlockSpec((None,) + shape, lambda b, i: (layer,) + (0,) * len(shape))
    half = lambda k: pl.BlockSpec((None, d, ffn), lambda b, i: (layer, 0, k))
    tok = pl.BlockSpec((None, tm, d), lambda b, i: (b, i, 0))
    return pl.pallas_call(
        functools.partial(_ffn_kernel, chunk=chunk, final=final),
        grid=(bsz, s // tm),
        in_specs=[tok, full((1, d)), row(4), row(3), row(5),
                  half(0), half(1), full((ffn, d)), pl.BlockSpec((1, d), lambda b, i: (0, 0))],
        out_specs=tok,
        out_shape=jax.ShapeDtypeStruct((bsz, s, d), F32),
        compiler_params=_params(2),
        name="ffn",
    )(x, norm_g, mod, mod, mod, p["w_ffn_in"], p["w_ffn_in"], p["w_ffn_out"], final_norm)


def _pack_input_projection(w_in):
    lead = w_in.shape[:-1]
    q0, kv0, gate0, lru0 = 1024, 1536, 2304, 2328
    keep = ((0, 0),) * len(lead)

    gap = jnp.zeros(lead + (LANES - HEAD_DIM,), w_in.dtype)

    def head_slots(w, n):
        pieces = []
        for h in range(n):
            pieces += [w[..., h * HEAD_DIM:(h + 1) * HEAD_DIM], gap]
        return pieces

    kv = lambda i: w_in[..., kv0 + i * 128:kv0 + (i + 1) * 128]
    gates = w_in[..., gate0:gate0 + N_KV * GROUP * 3].reshape(lead + (N_KV, GROUP, 3))
    gates = jnp.swapaxes(gates, -1, -2).reshape(lead + (N_KV, 3 * GROUP))
    gates = jnp.pad(gates, keep + ((0, 0), (0, 4))).reshape(lead + (N_KV * 16,))
    gates = jnp.pad(gates, keep + ((0, LANES - N_KV * 16),))
    return jnp.concatenate(
        [w_in[..., 0:q0]]
        + head_slots(w_in[..., q0:kv0] * (HEAD_DIM ** -0.5 * LOG2E), N_HEADS)
        + [kv(0), kv(1)]
        + head_slots(kv(2), N_KV) + head_slots(kv(4), N_KV)
        + [kv(3), kv(5), gates, w_in[..., lru0:lru0 + 2 * LRU_DIM]], axis=-1)


def _bf16_pieces(x):
    pieces = []
    for _ in range(SLOPE_PIECES):
        piece = float(np.asarray(x, np.float32).astype(BF16).astype(np.float64))
        pieces.append(piece)
        x = x - piece
    return pieces


def _q_slope_row():
    row = np.zeros((1, N_HEADS * LANES), np.float32)
    for h in range(N_HEADS):
        slope = 2.0 ** (-8.0 * (h + 1) / N_HEADS) * LOG2E
        base = h * LANES + HEAD_DIM
        row[0, base:base + SLOPE_PIECES] = _bf16_pieces(slope * 128.0)
        row[0, base + SLOPE_PIECES:base + 2 * SLOPE_PIECES] = _bf16_pieces(slope)
    return row


def _overlap_matrix(s):
    nc = s // CMP_STRIDE
    n_sel = s // SEL_BLOCK
    cs = np.arange(nc)[:, None] * CMP_STRIDE
    ss = np.arange(n_sel)[None, :] * SEL_BLOCK
    ov = np.clip(np.minimum(cs + CMP_LEN, ss + SEL_BLOCK) - np.maximum(cs, ss), 0, None) / CMP_LEN
    ovt = ov.T.astype(np.float32).reshape(n_sel, nc // CMP_CHUNK, CMP_CHUNK)
    return np.ascontiguousarray(ovt.transpose(1, 0, 2))


def _block_diag(w):
    depth, heads, n, _ = w.shape
    eye = jnp.asarray(np.eye(heads, dtype=np.float32))
    return (w[:, :, :, None, :] * eye[None, :, None, :, None]).reshape(depth, heads * n, heads * n)


def _prepare_params(a):
    depth = a["w_in"].shape[0]
    w_pack = _pack_input_projection(a["w_in"])
    assert w_pack.shape[-1] == COL_END
    row = lambda v: v.reshape(depth, 1, -1)

    def chunk_weights(half):
        rows = slice(half * CMP_STRIDE, (half + 1) * CMP_STRIDE)
        wk = a["nsa_w_ck1"].reshape(depth, CMP_LEN, HEAD_DIM, CMP_HIDDEN)[:, rows].astype(BF16)
        wv = a["nsa_w_cv1"].reshape(depth, CMP_LEN, HEAD_DIM, CMP_HIDDEN)[:, rows].astype(BF16)
        slots = [jnp.pad(w[:, :, :, None, :], ((0, 0), (0, 0), (0, 0), (k, 3 - k), (0, 0)))
                 for k, w in enumerate((wk, wk, wv, wv))]
        return jnp.stack(slots, axis=2).reshape(depth, CMP_STRIDE * 256, 4 * CMP_HIDDEN)

    eye2 = jnp.asarray(np.eye(N_KV, dtype=np.float32))
    ck2 = jnp.pad(a["nsa_w_ck2"], ((0, 0), (0, 0), (0, LANES - HEAD_DIM)))
    block2 = lambda w: (w[:, None, :, None, :] * eye2[None, :, None, :, None]).reshape(
        depth, N_KV * w.shape[1], N_KV * w.shape[2])
    w2 = jnp.concatenate([jnp.pad(block2(ck2), ((0, 0), (0, 0), (0, N_KV * HEAD_DIM))),
                          jnp.pad(block2(a["nsa_w_cv2"]), ((0, 0), (0, 0), (N_KV * LANES, 0)))], axis=1)
    pe_rows = lambda pe: jnp.pad(pe.reshape(depth, 1, -1), ((0, 0), (0, 7), (0, 0)))
    return {
        "w_pack": w_pack.astype(BF16),
        "norm_mix": row(a["norm_mix"]), "norm_ffn": row(a["norm_ffn"]),
        "conv_w_dw": a["conv_w_dw"], "conv_b_dw": row(a["conv_b_dw"]),
        "conv_ln_g": row(a["conv_ln_g"]), "conv_ln_b": row(a["conv_ln_b"]),
        "lru_w_conv": a["lru_w_conv"], "lru_b_conv": row(a["lru_b_conv"]),
        "lru_wa_bd": _block_diag(a["lru_w_a"]).astype(BF16), "lru_b_a": row(a["lru_b_a"]),
        "lru_wx_bd": _block_diag(a["lru_w_x"]).astype(BF16), "lru_b_x": row(a["lru_b_x"]),
        "lru_lam": row(a["lru_lam"]),
        "cmp_w_lo": chunk_weights(0), "cmp_w_hi": chunk_weights(1),
        "cmp_pe_k": pe_rows(a["nsa_pe_k"]), "cmp_pe_v": pe_rows(a["nsa_pe_v"]),
        "nsa_w_ck1": a["nsa_w_ck1"], "nsa_w_cv1": a["nsa_w_cv1"],
        "cmp_w2": w2.astype(BF16),
        "w_merge": a["w_merge"].astype(BF16), "b_merge": row(a["b_merge"]),
        "conv_w_out": a["conv_w_out"].astype(BF16), "nsa_w_out": a["nsa_w_out"].astype(BF16),
        "lru_w_out": a["lru_w_out"].astype(BF16), "w_out": a["w_out"].astype(BF16),
        "w_ffn_in": a["w_ffn_in"].astype(BF16), "w_ffn_out": a["w_ffn_out"].astype(BF16),
    }


def kernel(x, c, w_mod, b_mod, norm_mix, norm_ffn, w_in, conv_w_dw, conv_b_dw, conv_ln_g, conv_ln_b,
           conv_w_out, nsa_pe_k, nsa_w_ck1, nsa_w_ck2, nsa_pe_v, nsa_w_cv1, nsa_w_cv2, nsa_w_out,
           lru_w_conv, lru_b_conv, lru_w_a, lru_b_a, lru_w_x, lru_b_x, lru_lam, lru_w_out,
           w_merge, b_merge, w_out, w_ffn_in, w_ffn_out, final_norm):
    a = dict(norm_mix=norm_mix, norm_ffn=norm_ffn, w_in=w_in, conv_w_dw=conv_w_dw, conv_b_dw=conv_b_dw, conv_ln_g=conv_ln_g, conv_ln_b=conv_ln_b,
             conv_w_out=conv_w_out, nsa_pe_k=nsa_pe_k, nsa_w_ck1=nsa_w_ck1, nsa_w_ck2=nsa_w_ck2,
             nsa_pe_v=nsa_pe_v, nsa_w_cv1=nsa_w_cv1, nsa_w_cv2=nsa_w_cv2, nsa_w_out=nsa_w_out,
             lru_w_conv=lru_w_conv, lru_b_conv=lru_b_conv, lru_w_a=lru_w_a, lru_b_a=lru_b_a,
             lru_w_x=lru_w_x, lru_b_x=lru_b_x, lru_lam=lru_lam, lru_w_out=lru_w_out,
             w_merge=w_merge, b_merge=b_merge, w_out=w_out, w_ffn_in=w_ffn_in, w_ffn_out=w_ffn_out)
    depth = w_in.shape[0]
    s = x.shape[1]
    mod = _modulation(c, w_mod, b_mod)
    qaug = jnp.asarray(_q_slope_row())
    ovt = jnp.asarray(_overlap_matrix(s)).astype(BF16)
    fin = final_norm.reshape(1, -1)
    p = _prepare_params(a)
    for l in range(depth):
        uconv, ylru, q, kvc, ks, kw, vst, vwt, gt = _premix(x, mod, l, qaug, p)
        kc, vct = _compress(kvc, l, p)
        onsa = _attention(q, kc, vct, ovt, ks, vst, kw, vwt, gt)
        x = _merge(x, mod, l, uconv, ylru, onsa, p)
        x = _ffn(x, mod, l, p["norm_ffn"], p, fin, final=(l == depth - 1))
    return x
```

```python
import functools

import numpy as np
import jax
import jax.numpy as jnp
from jax import lax
from jax.experimental import pallas as pl
from jax.experimental.pallas import tpu as pltpu

F32 = jnp.float32
BF16 = jnp.bfloat16

EPS = 1e-6
CONV_DIM = 512
CONV_WIDTH = 31
N_HEADS = 8
N_KV = 2
GROUP = N_HEADS // N_KV
HEAD_DIM = 64
CMP_LEN = 32
CMP_STRIDE = 16
CMP_HIDDEN = 128
SEL_BLOCK = 64
SEL_TOPK = 16
WINDOW = 512
Q_BLOCK = 256
FORCE_SCORE = 1e4
LRU_DIM = 512
LRU_HEADS = 8
LRU_CONV_WIDTH = 4
LRU_C = 8.0
N_BRANCH = 3

LANES = 128
NEG = -1e30
LOG2E = 1.4426950408889634
SLOPE_PIECES = 3
KEY_TILE = 512
WIN_TILE = 128
CMP_CHUNK = 256
TOKEN_TILE = 512
BRANCH_TILE = 256
CONV_HALO = 32
LRU_HALO = 8
VMEM_LIMIT = 56 * 1024 * 1024

COL_CONV = 0
COL_Q = 1024
COL_KVC = 1536
COL_K = 1792
COL_V = 2304
COL_LRU = 2688
COL_END = 3712


def _params(n_grid):
    return pltpu.CompilerParams(dimension_semantics=("arbitrary",) * n_grid,
                                vmem_limit_bytes=VMEM_LIMIT)


def _dot(a, b):
    return jnp.dot(a, b, preferred_element_type=F32)


def _dot_nt(a, b):
    return lax.dot_general(a, b, (((1,), (1,)), ((), ())), preferred_element_type=F32)


def _split_bf16(a):
    hi = a.astype(BF16)
    lo = (a - hi.astype(F32)).astype(BF16)
    return hi, lo


def _dot3(a, b):
    a_hi, a_lo = _split_bf16(a)
    b_hi, b_lo = _split_bf16(b)
    return _dot(a_hi, b_hi) + (_dot(a_hi, b_lo) + _dot(a_lo, b_hi))


def _gelu(x):
    return 0.5 * x * (1.0 + jnp.tanh(0.7978845608028654 * (x + 0.044715 * (x * x * x))))


def _sigmoid(x):
    return 0.5 * jnp.tanh(0.5 * x) + 0.5


def _silu(x):
    return x * _sigmoid(x)


def _fold8(x, op):
    groups = [x[i:i + 8, :] for i in range(0, x.shape[0], 8)]
    while len(groups) > 1:
        groups = [op(groups[i], groups[i + 1]) if i + 1 < len(groups) else groups[i]
                  for i in range(0, len(groups), 2)]
    return groups[0]


def _position_lanes(pos, lane):
    hi = (pos >> 7).astype(F32)
    lo = (pos & 127).astype(F32)
    off = lane - HEAD_DIM
    return jnp.where((off >= 0) & (off < SLOPE_PIECES), hi,
                     jnp.where((off >= SLOPE_PIECES) & (off < 2 * SLOPE_PIECES), lo, 0.0))


def _mod_norm(x, g, sc, sh):
    y = x * lax.rsqrt(jnp.mean(x * x, axis=-1, keepdims=True) + EPS)
    return (y * g) * (1.0 + sc) + sh


def _mod_kernel(c_ref, w_ref, b_ref, o_ref):
    o_ref[...] = _dot3(c_ref[...], w_ref[...]) + b_ref[...]


def _modulation(c, w_mod, b_mod):
    depth, d, n = w_mod.shape
    bsz = c.shape[0]
    rows = 8
    c_pad = jnp.zeros((rows, d), F32).at[:bsz].set(c)
    out = pl.pallas_call(
        _mod_kernel,
        grid=(depth, n // d),
        in_specs=[pl.BlockSpec((rows, d), lambda l, j: (0, 0)),
                  pl.BlockSpec((None, d, d), lambda l, j: (l, 0, j)),
                  pl.BlockSpec((None, 1, d), lambda l, j: (l, 0, j))],
        out_specs=pl.BlockSpec((None, rows, d), lambda l, j: (l, 0, j)),
        out_shape=jax.ShapeDtypeStruct((depth, rows, n), F32),
        compiler_params=_params(2),
        name="modulation",
    )(c_pad, w_mod, b_mod.reshape(depth, 1, n))
    return out[:, :bsz].reshape(depth, bsz, n // d, 1, d)


def _premix_kernel(x_ref, g_ref, sc_ref, sh_ref, w_ref, qaug_ref,
                   wc4_ref, bc4_ref, wa_ref, ba_ref, wx_ref, bx_ref, lam_ref,
                   uconv_ref, ylru_ref, q_ref, kvc_ref, ks_ref, kw_ref, vst_ref, vwt_ref, gt_ref,
                   kvc_tmp_ref, xext_ref, hcar_ref):
    i = pl.program_id(1)
    tm = x_ref.shape[0]

    @pl.when(i == 0)
    def _():
        xext_ref[0:LRU_HALO, :] = jnp.zeros((LRU_HALO, LRU_DIM), F32)
        hcar_ref[...] = jnp.zeros(hcar_ref.shape, F32)

    hb = _mod_norm(x_ref[...], g_ref[...], sc_ref[...], sh_ref[...]).astype(BF16)

    def proj(a, b):
        return _dot(hb, w_ref[:, a:b])

    u_conv = proj(COL_CONV, COL_Q)
    uconv_ref[...] = u_conv[:, 0:CONV_DIM] * _sigmoid(u_conv[:, CONV_DIM:2 * CONV_DIM])
    u_lru = proj(COL_LRU, COL_END)
    for r0 in range(0, tm, BRANCH_TILE):
        ylru_ref[r0:r0 + BRANCH_TILE, :] = _lru_rows(
            u_lru[r0:r0 + BRANCH_TILE, 0:LRU_DIM], _gelu(u_lru[r0:r0 + BRANCH_TILE, LRU_DIM:2 * LRU_DIM]),
            wc4_ref, bc4_ref, wa_ref, ba_ref, wx_ref, bx_ref, lam_ref, xext_ref, hcar_ref)
    uq = proj(COL_Q, COL_KVC)
    low_half = lax.broadcasted_iota(jnp.int32, (tm, LANES), 1) < HEAD_DIM
    for h in range(N_HEADS):
        pair = uq[:, (h // 2) * LANES:(h // 2 + 1) * LANES]
        if h % 2:
            pair = pltpu.roll(pair, HEAD_DIM, axis=1)
        q_ref[:, h * LANES:(h + 1) * LANES] = jnp.where(low_half, pair, qaug_ref[h:h + 1, :]).astype(BF16)
    ukv = proj(COL_KVC, COL_K)
    for half in range(2):
        kvc_tmp_ref[half] = ukv[:, half * LANES:(half + 1) * LANES]
    for t in range(CMP_STRIDE):
        for half in range(2):
            col = t * 256 + half * LANES
            kvc_ref[:, col:col + LANES] = (
                kvc_tmp_ref[half, pl.ds(t, tm // CMP_STRIDE, stride=CMP_STRIDE), :].astype(BF16))
    pos = i * tm + lax.broadcasted_iota(jnp.int32, (tm, LANES), 0)
    lane = lax.broadcasted_iota(jnp.int32, (tm, LANES), 1)
    kaug = _position_lanes(pos, lane)
    uk = proj(COL_K, COL_V)
    ks_ref[0] = (uk[:, 0:128] + kaug).astype(BF16)
    ks_ref[1] = (uk[:, 128:256] + kaug).astype(BF16)
    kw_ref[0] = (uk[:, 256:384] + kaug).astype(BF16)
    kw_ref[1] = (uk[:, 384:512] + kaug).astype(BF16)
    uv = proj(COL_V, COL_LRU)
    vst_ref[...] = uv[:, 0:128].T.astype(BF16)
    vwt = uv[:, 128:256].T.astype(BF16)
    for t in range(tm // WIN_TILE):
        vwt_ref[t] = vwt[:, t * WIN_TILE:(t + 1) * WIN_TILE]
    gt = _sigmoid(uv[:, 256:384]).T
    gt_ref[...] = gt[0:32, :]


def _premix(x, mod, layer, qaug, p):
    bsz, s, d = x.shape
    tm = TOKEN_TILE
    c = LRU_DIM
    assert tm == KEY_TILE and s % tm == 0 and tm % BRANCH_TILE == 0
    n_t = s // tm
    row = lambda k: pl.BlockSpec((None, None, None, 1, d), lambda b, i: (layer, b, k, 0, 0))
    full = lambda shape: pl.BlockSpec((None,) + shape, lambda b, i: (layer,) + (0,) * len(shape))
    out_shapes = (
        jax.ShapeDtypeStruct((bsz, s, CONV_DIM), F32),
        jax.ShapeDtypeStruct((bsz, s, LRU_DIM), BF16),
        jax.ShapeDtypeStruct((bsz, s, N_HEADS * LANES), BF16),
        jax.ShapeDtypeStruct((bsz, s // CMP_STRIDE, CMP_STRIDE * 256), BF16),
        jax.ShapeDtypeStruct((bsz, N_KV, s, LANES), BF16),
        jax.ShapeDtypeStruct((bsz, N_KV, s, LANES), BF16),
        jax.ShapeDtypeStruct((bsz, n_t, LANES, KEY_TILE), BF16),
        jax.ShapeDtypeStruct((bsz, s // WIN_TILE, LANES, WIN_TILE), BF16),
        jax.ShapeDtypeStruct((bsz, 32, s), F32),
    )
    out_specs = (
        pl.BlockSpec((None, tm, CONV_DIM), lambda b, i: (b, i, 0)),
        pl.BlockSpec((None, tm, LRU_DIM), lambda b, i: (b, i, 0)),
        pl.BlockSpec((None, tm, N_HEADS * LANES), lambda b, i: (b, i, 0)),
        pl.BlockSpec((None, tm // CMP_STRIDE, CMP_STRIDE * 256), lambda b, i: (b, i, 0)),
        pl.BlockSpec((None, N_KV, tm, LANES), lambda b, i: (b, 0, i, 0)),
        pl.BlockSpec((None, N_KV, tm, LANES), lambda b, i: (b, 0, i, 0)),
        pl.BlockSpec((None, None, LANES, KEY_TILE), lambda b, i: (b, i, 0, 0)),
        pl.BlockSpec((None, tm // WIN_TILE, LANES, WIN_TILE), lambda b, i: (b, i, 0, 0)),
        pl.BlockSpec((None, 32, tm), lambda b, i: (b, 0, i)),
    )
    return pl.pallas_call(
        _premix_kernel,
        grid=(bsz, n_t),
        in_specs=[pl.BlockSpec((None, tm, d), lambda b, i: (b, i, 0)),
                  full((1, d)), row(1), row(0), full((d, COL_END)),
                  pl.BlockSpec((N_HEADS, LANES), lambda b, i: (0, 0)),
                  full((LRU_CONV_WIDTH, c)), full((1, c)),
                  full((c, c)), full((1, c)), full((c, c)), full((1, c)), full((1, c))],
        out_specs=out_specs,
        out_shape=out_shapes,
        scratch_shapes=[pltpu.VMEM((2, tm, LANES), F32),
                        pltpu.VMEM((BRANCH_TILE + LRU_HALO, c), F32),
                        pltpu.VMEM((8, c), F32)],
        compiler_params=_params(2),
        name="premix",
    )(x, p["norm_mix"], mod, mod, p["w_pack"], qaug,
      p["lru_w_conv"], p["lru_b_conv"], p["lru_wa_bd"], p["lru_b_a"], p["lru_wx_bd"], p["lru_b_x"],
      p["lru_lam"])


def _conv_rows(v, wdw_ref, bdw_ref, lng_ref, lnb_ref, vext_ref, vsh_ref):
    ts, c = v.shape
    vext_ref[CONV_HALO:CONV_HALO + ts, :] = v
    span = ts + CONV_HALO - 8
    for phase in range(1, 8):
        vsh_ref[phase - 1] = vext_ref[pl.ds(phase, span), :]
    acc = jnp.broadcast_to(bdw_ref[...], (ts, c))
    first = CONV_HALO - (CONV_WIDTH - 1)
    for j in range(CONV_WIDTH):
        base, phase = (first + j) // 8 * 8, (first + j) % 8
        rows = vext_ref[base:base + ts, :] if phase == 0 else vsh_ref[phase - 1, base:base + ts, :]
        acc = acc + wdw_ref[j:j + 1, :] * rows
    vext_ref[0:CONV_HALO, :] = v[ts - CONV_HALO:ts, :]
    mu = jnp.mean(acc, axis=-1, keepdims=True)
    cen = acc - mu
    var = jnp.mean(cen * cen, axis=-1, keepdims=True)
    ln = (cen * lax.rsqrt(var + EPS)) * lng_ref[...] + lnb_ref[...]
    return _silu(ln).astype(BF16)


def _lru_rows(ux, gate, wc4_ref, bc4_ref, wa_ref, ba_ref, wx_ref, bx_ref, lam_ref, xext_ref, hcar_ref):
    ts, c = ux.shape
    xext_ref[LRU_HALO:LRU_HALO + ts, :] = ux
    xr = jnp.broadcast_to(bc4_ref[...], (ts, c))
    first = LRU_HALO - (LRU_CONV_WIDTH - 1)
    for j in range(LRU_CONV_WIDTH):
        xr = xr + wc4_ref[j:j + 1, :] * xext_ref[pl.ds(first + j, ts), :]
    xext_ref[0:LRU_HALO, :] = ux[ts - LRU_HALO:ts, :]
    xb = xr.astype(BF16)
    r = _sigmoid(_dot(xb, wa_ref[...]) + ba_ref[...])
    gate_i = _sigmoid(_dot(xb, wx_ref[...]) + bx_ref[...])
    z = -lam_ref[...]
    softplus = jnp.maximum(z, 0.0) + jnp.log(1.0 + jnp.exp(-jnp.abs(z)))
    log_a = (-LRU_C * r) * softplus
    a = jnp.exp(log_a)
    b = jnp.sqrt(1.0 - a * a) * (gate_i * xr)
    rows = lax.broadcasted_iota(jnp.int32, (ts, c), 0)
    shift = 1
    while shift < ts:
        a_prev = pltpu.roll(a, shift, axis=0)
        b_prev = pltpu.roll(b, shift, axis=0)
        live = rows >= shift
        b = jnp.where(live, a * b_prev + b, b)
        a = jnp.where(live, a * a_prev, a)
        shift *= 2
    h = a * hcar_ref[0:1, :] + b
    hcar_ref[...] = jnp.broadcast_to(h[ts - 1:ts, :], hcar_ref.shape)
    return (h * gate).astype(BF16)


def _compress_kernel(x_ref, wlo_ref, whi_ref, pek_ref, pev_ref, w1k_ref, w1v_ref, w2_ref,
                     kc_ref, vct_ref):
    nc = x_ref.shape[0]
    x = x_ref[...]
    h_lo = _dot(x, wlo_ref[...])
    h_hi = _dot(x, whi_ref[...])
    h_next = pltpu.roll(h_hi, nc - 1, axis=0)
    bk = _dot3(pek_ref[...], w1k_ref[...])[0:1, :]
    bv = _dot3(pev_ref[...], w1v_ref[...])[0:1, :]
    bias = jnp.concatenate([bk, bk, bv, bv], axis=1)
    hid = _gelu(h_lo + h_next + bias).astype(BF16)
    out = _dot(hid, w2_ref[...])
    cend = lax.broadcasted_iota(jnp.int32, (nc, LANES), 0) * CMP_STRIDE + (CMP_LEN - 1)
    lane = lax.broadcasted_iota(jnp.int32, (nc, LANES), 1)
    aug = _position_lanes(cend, lane)
    kc_ref[0] = (out[:, 0:128] + aug).astype(BF16)
    kc_ref[1] = (out[:, 128:256] + aug).astype(BF16)
    vct = out[:, 256:384].T.astype(BF16)
    for t in range(nc // CMP_CHUNK):
        vct_ref[t] = vct[:, t * CMP_CHUNK:(t + 1) * CMP_CHUNK]


def _compress(kvc, layer, p):
    bsz, nc, width = kvc.shape
    x = kvc
    full = lambda shape: pl.BlockSpec((None,) + shape, lambda b: (layer,) + (0,) * len(shape))
    return pl.pallas_call(
        _compress_kernel,
        grid=(bsz,),
        in_specs=[pl.BlockSpec((None, nc, width), lambda b: (b, 0, 0)),
                  full((width, 512)), full((width, 512)),
                  full((8, CMP_LEN * HEAD_DIM)), full((8, CMP_LEN * HEAD_DIM)),
                  full((CMP_LEN * HEAD_DIM, CMP_HIDDEN)), full((CMP_LEN * HEAD_DIM, CMP_HIDDEN)),
                  full((512, 384))],
        out_specs=(pl.BlockSpec((None, N_KV, nc, LANES), lambda b: (b, 0, 0, 0)),
                   pl.BlockSpec((None, nc // CMP_CHUNK, LANES, CMP_CHUNK), lambda b: (b, 0, 0, 0))),
        out_shape=(jax.ShapeDtypeStruct((bsz, N_KV, nc, LANES), BF16),
                   jax.ShapeDtypeStruct((bsz, nc // CMP_CHUNK, LANES, CMP_CHUNK), BF16)),
        compiler_params=_params(1),
        name="compress",
    )(x, p["cmp_w_lo"], p["cmp_w_hi"], p["cmp_pe_k"], p["cmp_pe_v"], p["nsa_w_ck1"], p["nsa_w_cv1"],
      p["cmp_w2"])


def _attn_kernel(q_ref, kc_ref, vct_ref, ovt_ref, ks_ref, vst_ref, kw_ref, vwt_ref, gt_ref,
                 o_ref, s0_ref, s1_ref, w_ref, selneg_ref, m_ref, l_ref, acc_ref, flag_ref, list_ref):
    qb = pl.program_id(2)
    q0 = qb * Q_BLOCK
    n_sel = ovt_ref.shape[1]
    n_rows = GROUP * Q_BLOCK
    qg = jnp.concatenate([q_ref[:, r * LANES:(r + 1) * LANES] for r in range(GROUP)], axis=0)
    qpos = q0 + (lax.broadcasted_iota(jnp.int32, (1, n_rows), 1) & (Q_BLOCK - 1))

    n_chunks = ((q0 + Q_BLOCK - CMP_LEN) // CMP_STRIDE + CMP_CHUNK) // CMP_CHUNK
    cend = lax.broadcasted_iota(jnp.int32, (CMP_CHUNK, n_rows), 0) * CMP_STRIDE + (CMP_LEN - 1)

    def cmp_branch(n_ch):
        def fn():
            s_all = _dot_nt(kc_ref[0:n_ch * CMP_CHUNK, :], qg)
            scores = []
            m_c = None
            for ci in range(n_ch):
                s = s_all[ci * CMP_CHUNK:(ci + 1) * CMP_CHUNK, :]
                if ci >= n_ch - 2:
                    s = jnp.where(cend + ci * CMP_CHUNK * CMP_STRIDE <= qpos, s, NEG)
                scores.append(s)
                mi = jnp.max(s, axis=0, keepdims=True)
                m_c = mi if m_c is None else jnp.maximum(m_c, mi)
            m_c = jnp.where(m_c > 0.5 * NEG, m_c, 0.0)
            probs = [jnp.exp2(s - m_c) for s in scores]
            den = probs[0].sum(axis=0, keepdims=True)
            for p in probs[1:]:
                den = den + jnp.sum(p, axis=0, keepdims=True)
            inv_den = 1.0 / jnp.where(den > 0, den, 1.0)
            o_c = None
            imp = None
            for ci, p in enumerate(probs):
                p = p * inv_den
                part = _dot(vct_ref[ci], p.astype(BF16))
                o_c = part if o_c is None else o_c + part
                p_sum = p[:, 0:Q_BLOCK]
                for r in range(1, GROUP):
                    p_sum = p_sum + p[:, r * Q_BLOCK:(r + 1) * Q_BLOCK]
                p_hi, p_lo = _split_bf16(p_sum)
                part = _dot(ovt_ref[ci], p_hi) + _dot(ovt_ref[ci], p_lo)
                imp = part if imp is None else imp + part
            return o_c, imp
        return fn

    o_c, imp = lax.switch(n_chunks - 1, [cmp_branch(n) for n in range(1, kc_ref.shape[0] // CMP_CHUNK + 1)])

    blk = lax.broadcasted_iota(jnp.int32, (n_sel, LANES), 0)
    blk_f = blk.astype(F32)
    works = []
    for h in range(Q_BLOCK // LANES):
        qp = q0 + h * LANES + lax.broadcasted_iota(jnp.int32, (n_sel, LANES), 1)
        qid = qp // SEL_BLOCK
        forced = (blk == 0) | (blk == qid) | (blk == qid - 1)
        valid = blk * SEL_BLOCK <= qp
        works.append(jnp.where(forced, -jnp.inf, jnp.where(valid, imp[:, h * LANES:(h + 1) * LANES], -1.0)))
    for _ in range(min(SEL_TOPK, n_sel) - 3):
        for h, work_h in enumerate(works):
            best = jnp.max(work_h, axis=0, keepdims=True)
            first = jnp.min(jnp.where(work_h == best, blk_f, float(n_sel)), axis=0, keepdims=True)
            works[h] = jnp.where(blk_f == first, -jnp.inf, work_h)
    work = jnp.concatenate(works, axis=1)

    n_tiles = q0 // KEY_TILE + 1
    krow = lax.broadcasted_iota(jnp.int32, (SEL_BLOCK, n_rows), 0)

    def scores_into(buf_ref, kt):
        k0 = pl.multiple_of(kt * KEY_TILE, KEY_TILE)
        buf_ref[...] = _dot_nt(ks_ref[pl.ds(k0, KEY_TILE), :], qg)

    k_last = pl.multiple_of((n_tiles - 1) * KEY_TILE, KEY_TILE)
    s_both = _dot_nt(jnp.concatenate([ks_ref[pl.ds(k_last, KEY_TILE), :], ks_ref[0:LANES, :]], axis=0), qg)
    s0_ref[...] = s_both[0:KEY_TILE, :]
    hrow = lax.broadcasted_iota(jnp.int32, (LANES, n_rows), 0)
    s_first = jnp.where((hrow < SEL_BLOCK) & (hrow <= qpos), s_both[KEY_TILE:KEY_TILE + LANES, :], NEG)
    m_first = jnp.max(s_first, axis=0, keepdims=True)
    p_first = jnp.exp2(s_first - m_first)
    m_ref[...] = m_first
    l_ref[...] = jnp.sum(p_first, axis=0, keepdims=True)
    acc_ref[...] = _dot(vst_ref[0, :, 0:LANES], p_first.astype(BF16))

    n_win = (WINDOW + Q_BLOCK) // WIN_TILE
    t0 = jnp.maximum(q0 - WINDOW, 0) // WIN_TILE
    kstart = pl.multiple_of(t0 * WIN_TILE, WIN_TILE)
    w_ref[...] = _dot_nt(kw_ref[pl.ds(kstart, n_win * WIN_TILE), :], qg)
    wrow = lax.broadcasted_iota(jnp.int32, (WIN_TILE, n_rows), 0)
    qrel = qpos - kstart

    top = None
    for t in range(n_win):
        newest = qrel - t * WIN_TILE
        visible = wrow <= newest
        if t * WIN_TILE < Q_BLOCK:
            visible = visible & (wrow > newest - WINDOW)
        st = jnp.where(visible, w_ref[t * WIN_TILE:(t + 1) * WIN_TILE, :], NEG)
        w_ref[t * WIN_TILE:(t + 1) * WIN_TILE, :] = st
        mt = _fold8(st, jnp.maximum)
        top = mt if top is None else jnp.maximum(top, mt)
    m_w = jnp.max(top, axis=0, keepdims=True)
    total = None
    o_w = None
    for t in range(n_win):
        pt = jnp.exp2(w_ref[t * WIN_TILE:(t + 1) * WIN_TILE, :] - m_w)
        lt = _fold8(pt, jnp.add)
        total = lt if total is None else total + lt
        part = _dot(vwt_ref[t0 + t], pt.astype(BF16))
        o_w = part if o_w is None else o_w + part
    o_w = o_w / jnp.sum(total, axis=0, keepdims=True)

    taken = (work == -jnp.inf) & (lax.broadcasted_iota(jnp.int32, (n_sel, Q_BLOCK), 0) > 0)
    chosen = jnp.where(taken, 1.0, 0.0)
    selneg = jnp.where(taken, 0.0, NEG)
    selneg_ref[...] = jnp.concatenate([selneg] * GROUP, axis=1)
    blocks_per_tile = KEY_TILE // SEL_BLOCK
    any_q = jnp.max(chosen, axis=1, keepdims=True)
    for t in range(n_sel // blocks_per_tile):
        tile_any = jnp.max(any_q[t * blocks_per_tile:(t + 1) * blocks_per_tile, :])
        flag_ref[t] = (tile_any > 0).astype(jnp.int32)

    def compact(kt, n):
        @pl.when(flag_ref[kt] > 0)
        def _():
            list_ref[n] = kt
        return n + flag_ref[kt]

    n_act = lax.fori_loop(0, n_tiles - 1, compact, 0)
    list_ref[n_act] = 0

    def sel_tile(buf_ref, kt, causal, n_blocks=KEY_TILE // SEL_BLOCK):
        sel_rows = selneg_ref[pl.ds(pl.multiple_of(kt * blocks_per_tile, blocks_per_tile),
                                    blocks_per_tile), :]

        def block_scores(j):
            return buf_ref[j * SEL_BLOCK:(j + 1) * SEL_BLOCK, :]

        m_old = m_ref[...]
        top = None
        for j in range(n_blocks):
            sj = block_scores(j)
            if causal:
                sj = jnp.where(krow <= qpos - (kt * KEY_TILE + j * SEL_BLOCK), sj, NEG)
                buf_ref[j * SEL_BLOCK:(j + 1) * SEL_BLOCK, :] = sj
            mj = _fold8(sj, jnp.maximum) + sel_rows[j:j + 1, :]
            top = mj if top is None else jnp.maximum(top, mj)
        m_new = jnp.maximum(m_old, jnp.max(top, axis=0, keepdims=True))
        alpha = jnp.exp2(m_old - m_new)
        total = None
        probs = []
        for j in range(n_blocks):
            pj = jnp.exp2(block_scores(j) - (m_new - sel_rows[j:j + 1, :]))
            lj = _fold8(pj, jnp.add)
            total = lj if total is None else total + lj
            probs.append(pj.astype(BF16))
        m_ref[...] = m_new
        l_ref[...] = alpha * l_ref[...] + jnp.sum(total, axis=0, keepdims=True)
        acc_ref[...] = alpha * acc_ref[...] + _dot(vst_ref[kt, :, 0:n_blocks * SEL_BLOCK],
                                                   jnp.concatenate(probs, axis=0))

    half_tile = (q0 % KEY_TILE) + Q_BLOCK <= KEY_TILE // 2

    @pl.when(half_tile)
    def _():
        scores_into(s1_ref, list_ref[0])
        sel_tile(s0_ref, n_tiles - 1, True, n_blocks=KEY_TILE // SEL_BLOCK // 2)

    @pl.when(jnp.logical_not(half_tile))
    def _():
        scores_into(s1_ref, list_ref[0])
        sel_tile(s0_ref, n_tiles - 1, True)

    def pipelined(i, carry):
        scores_into(s0_ref, list_ref[2 * i + 1])
        sel_tile(s1_ref, list_ref[2 * i], False)

        @pl.when(2 * i + 1 < n_act)
        def _():
            scores_into(s1_ref, list_ref[2 * i + 2])
            sel_tile(s0_ref, list_ref[2 * i + 1], False)
        return carry

    lax.fori_loop(0, (n_act + 1) // 2, pipelined, 0)
    o_s = acc_ref[...] / l_ref[...]

    def gate(branch):
        return jnp.concatenate([gt_ref[branch * GROUP + r:branch * GROUP + r + 1, :] for r in range(GROUP)],
                               axis=1)
    o = gate(0) * o_c + gate(1) * o_s + gate(2) * o_w
    o_ref[...] = jnp.concatenate([o[:, r * Q_BLOCK:(r + 1) * Q_BLOCK].T for r in range(GROUP)],
                                 axis=1).astype(BF16)


def _attention(q, kc, vct, ovt, ks, vst, kw, vwt, gt):
    bsz, s, _ = q.shape
    nc = kc.shape[2]
    n_sel = s // SEL_BLOCK
    n_qb = s // Q_BLOCK
    n_kt = s // KEY_TILE
    n_wt = s // WIN_TILE
    n_ch = nc // CMP_CHUNK
    n_rows = GROUP * Q_BLOCK
    assert s % KEY_TILE == 0 and s >= (WINDOW + Q_BLOCK) and n_sel % 8 == 0 and nc % CMP_CHUNK == 0
    assert KEY_TILE % Q_BLOCK == 0 and Q_BLOCK % WIN_TILE == 0
    return pl.pallas_call(
        _attn_kernel,
        grid=(bsz, N_KV, n_qb),
        in_specs=[pl.BlockSpec((None, Q_BLOCK, GROUP * LANES), lambda b, g, i: (b, i, g)),
                  pl.BlockSpec((None, None, nc, LANES), lambda b, g, i: (b, g, 0, 0)),
                  pl.BlockSpec((None, n_ch, HEAD_DIM, CMP_CHUNK), lambda b, g, i: (b, 0, g, 0)),
                  pl.BlockSpec((n_ch, n_sel, CMP_CHUNK), lambda b, g, i: (0, 0, 0)),
                  pl.BlockSpec((None, None, s, LANES), lambda b, g, i: (b, g, 0, 0)),
                  pl.BlockSpec((None, n_kt, HEAD_DIM, KEY_TILE), lambda b, g, i: (b, 0, g, 0)),
                  pl.BlockSpec((None, None, s, LANES), lambda b, g, i: (b, g, 0, 0)),
                  pl.BlockSpec((None, n_wt, HEAD_DIM, WIN_TILE), lambda b, g, i: (b, 0, g, 0)),
                  pl.BlockSpec((None, 16, Q_BLOCK), lambda b, g, i: (b, g, i))],
        out_specs=pl.BlockSpec((None, Q_BLOCK, GROUP * HEAD_DIM), lambda b, g, i: (b, i, g)),
        out_shape=jax.ShapeDtypeStruct((bsz, s, N_HEADS * HEAD_DIM), BF16),
        scratch_shapes=[pltpu.VMEM((KEY_TILE, n_rows), F32),
                        pltpu.VMEM((KEY_TILE, n_rows), F32),
                        pltpu.VMEM((WINDOW + Q_BLOCK, n_rows), F32),
                        pltpu.VMEM((n_sel, n_rows), F32),
                        pltpu.VMEM((1, n_rows), F32),
                        pltpu.VMEM((1, n_rows), F32),
                        pltpu.VMEM((HEAD_DIM, n_rows), F32),
                        pltpu.SMEM((n_kt,), jnp.int32),
                        pltpu.SMEM((n_kt + 1,), jnp.int32)],
        compiler_params=_params(3),
        name="nsa_attention",
    )(q, kc, vct, ovt, ks, vst, kw, vwt, gt)


def _merge_kernel(x_ref, g_ref, sc_ref, sh_ref, gate_ref, uconv_ref, ylru_ref, onsa_ref,
                  wdw_ref, bdw_ref, lng_ref, lnb_ref, wm_ref, bm_ref, wc_ref, wn_ref, wl_ref, wo_ref,
                  o_ref, vext_ref, vsh_ref):
    tm, d = x_ref.shape

    @pl.when(pl.program_id(1) == 0)
    def _():
        vext_ref[0:CONV_HALO, :] = jnp.zeros((CONV_HALO, CONV_DIM), F32)

    x = x_ref[...]
    hb = _mod_norm(x, g_ref[...], sc_ref[...], sh_ref[...]).astype(BF16)

    def merge_gate(k):
        return _sigmoid(_dot(hb, wm_ref[:, k * d:(k + 1) * d]) + bm_ref[:, k * d:(k + 1) * d])

    early = merge_gate(1) * _dot(onsa_ref[...], wn_ref[...]) + merge_gate(2) * _dot(ylru_ref[...], wl_ref[...])
    vconv = jnp.concatenate(
        [_conv_rows(uconv_ref[r0:r0 + BRANCH_TILE, :], wdw_ref, bdw_ref, lng_ref, lnb_ref, vext_ref, vsh_ref)
         for r0 in range(0, tm, BRANCH_TILE)], axis=0)
    merged = early + merge_gate(0) * _dot(vconv, wc_ref[...])
    o_ref[...] = x + gate_ref[...] * _dot(merged.astype(BF16), wo_ref[...])


def _merge(x, mod, layer, uconv, ylru, onsa, p):
    bsz, s, d = x.shape
    tm = TOKEN_TILE
    ts = BRANCH_TILE
    c = CONV_DIM
    assert tm % ts == 0
    row = lambda k: pl.BlockSpec((None, None, None, 1, d), lambda b, i: (layer, b, k, 0, 0))
    full = lambda shape: pl.BlockSpec((None,) + shape, lambda b, i: (layer,) + (0,) * len(shape))
    tok = lambda w: pl.BlockSpec((None, tm, w), lambda b, i: (b, i, 0))
    return pl.pallas_call(
        _merge_kernel,
        grid=(bsz, s // tm),
        in_specs=[tok(d), full((1, d)), row(1), row(0), row(2), tok(c), tok(c), tok(c),
                  full((CONV_WIDTH, c)), full((1, c)), full((1, c)), full((1, c)),
                  full((d, N_BRANCH * d)), full((1, N_BRANCH * d)),
                  full((c, d)), full((c, d)), full((c, d)), full((d, d))],
        out_specs=tok(d),
        out_shape=jax.ShapeDtypeStruct((bsz, s, d), F32),
        scratch_shapes=[pltpu.VMEM((ts + CONV_HALO, c), F32),
                        pltpu.VMEM((7, ts + CONV_HALO - 8, c), F32)],
        compiler_params=_params(2),
        name="merge",
    )(x, p["norm_mix"], mod, mod, mod, uconv, ylru, onsa,
      p["conv_w_dw"], p["conv_b_dw"], p["conv_ln_g"], p["conv_ln_b"],
      p["w_merge"], p["b_merge"], p["conv_w_out"], p["nsa_w_out"], p["lru_w_out"], p["w_out"])


def _ffn_kernel(x_ref, g_ref, sc_ref, sh_ref, gate_ref, wa_ref, wb_ref, wo_ref, fin_ref, o_ref,
                *, chunk, final):
    x = x_ref[...]
    hb = _mod_norm(x, g_ref[...], sc_ref[...], sh_ref[...]).astype(BF16)
    ffn = wa_ref.shape[1]
    acc = None
    for c0 in range(0, ffn, chunk):
        a = _dot(hb, wa_ref[:, c0:c0 + chunk])
        b = _dot(hb, wb_ref[:, c0:c0 + chunk])
        part = _dot((_silu(a) * b).astype(BF16), wo_ref[c0:c0 + chunk, :])
        acc = part if acc is None else acc + part
    y = x + gate_ref[...] * acc
    if final:
        y = (y * lax.rsqrt(jnp.mean(y * y, axis=-1, keepdims=True) + EPS)) * fin_ref[...]
    o_ref[...] = y


def _ffn(x, mod, layer, norm_g, p, final_norm, final):
    bsz, s, d = x.shape
    tm = TOKEN_TILE
    ffn = p["w_ffn_out"].shape[1]
    chunk = 256
    assert ffn % chunk == 0
    row = lambda k: pl.BlockSpec((None, None, None, 1, d), lambda b, i: (layer, b, k, 0, 0))
    full = lambda shape: pl.BlockSpec((None,) + shape, lambda b, i: (layer,) + (0,) * len(shape))
    half = lambda k: pl.BlockSpec((None, d, ffn), lambda b, i: (layer, 0, k))
    tok = pl.BlockSpec((None, tm, d), lambda b, i: (b, i, 0))
    return pl.pallas_call(
        functools.partial(_ffn_kernel, chunk=chunk, final=final),
        grid=(bsz, s // tm),
        in_specs=[tok, full((1, d)), row(4), row(3), row(5),
                  half(0), half(1), full((ffn, d)), pl.BlockSpec((1, d), lambda b, i: (0, 0))],
        out_specs=tok,
        out_shape=jax.ShapeDtypeStruct((bsz, s, d), F32),
        compiler_params=_params(2),
        name="ffn",
    )(x, norm_g, mod, mod, mod, p["w_ffn_in"], p["w_ffn_in"], p["w_ffn_out"], final_norm)


def _pack_input_projection(w_in):
    lead = w_in.shape[:-1]
    q0, kv0, gate0, lru0 = 1024, 1536, 2304, 2328
    keep = ((0, 0),) * len(lead)

    gap = jnp.zeros(lead + (LANES - HEAD_DIM,), BF16)

    def head_slots(w, n):
        pieces = []
        for h in range(n):
            pieces += [w[..., h * HEAD_DIM:(h + 1) * HEAD_DIM].astype(BF16), gap]
        return pieces

    kv = lambda i: w_in[..., kv0 + i * 128:kv0 + (i + 1) * 128]
    gates = w_in[..., gate0:gate0 + N_KV * GROUP * 3].reshape(lead + (N_KV, GROUP, 3))
    gates = jnp.swapaxes(gates, -1, -2).reshape(lead + (N_KV, 3 * GROUP))
    gates = jnp.pad(gates, keep + ((0, 0), (0, 4))).reshape(lead + (N_KV * 16,))
    gates = jnp.pad(gates, keep + ((0, LANES - N_KV * 16),))
    cast = lambda w: w.astype(BF16)
    return jnp.concatenate(
        [cast(w_in[..., 0:q0]), cast(w_in[..., q0:kv0] * (HEAD_DIM ** -0.5 * LOG2E)), cast(kv(0)), cast(kv(1))]
        + head_slots(kv(2), N_KV) + head_slots(kv(4), N_KV)
        + [cast(kv(3)), cast(kv(5)), cast(gates), cast(w_in[..., lru0:lru0 + 2 * LRU_DIM])], axis=-1)


def _bf16_pieces(x):
    pieces = []
    for _ in range(SLOPE_PIECES):
        piece = float(np.asarray(x, np.float32).astype(BF16).astype(np.float64))
        pieces.append(piece)
        x = x - piece
    return pieces


def _q_slope_rows():
    rows = np.zeros((N_HEADS, LANES), np.float32)
    for h in range(N_HEADS):
        slope = 2.0 ** (-8.0 * (h + 1) / N_HEADS) * LOG2E
        rows[h, HEAD_DIM:HEAD_DIM + SLOPE_PIECES] = _bf16_pieces(slope * 128.0)
        rows[h, HEAD_DIM + SLOPE_PIECES:HEAD_DIM + 2 * SLOPE_PIECES] = _bf16_pieces(slope)
    return rows


def _overlap_matrix(s):
    nc = s // CMP_STRIDE
    n_sel = s // SEL_BLOCK
    cs = np.arange(nc)[:, None] * CMP_STRIDE
    ss = np.arange(n_sel)[None, :] * SEL_BLOCK
    ov = np.clip(np.minimum(cs + CMP_LEN, ss + SEL_BLOCK) - np.maximum(cs, ss), 0, None) / CMP_LEN
    ovt = ov.T.astype(np.float32).reshape(n_sel, nc // CMP_CHUNK, CMP_CHUNK)
    return np.ascontiguousarray(ovt.transpose(1, 0, 2))


def _block_diag(w):
    depth, heads, n, _ = w.shape
    eye = jnp.asarray(np.eye(heads, dtype=np.float32))
    return (w[:, :, :, None, :] * eye[None, :, None, :, None]).reshape(depth, heads * n, heads * n)


def _prepare_params(a):
    depth = a["w_in"].shape[0]
    w_pack = _pack_input_projection(a["w_in"])
    assert w_pack.shape[-1] == COL_END
    row = lambda v: v.reshape(depth, 1, -1)

    def chunk_weights(half):
        rows = slice(half * CMP_STRIDE, (half + 1) * CMP_STRIDE)
        wk = a["nsa_w_ck1"].reshape(depth, CMP_LEN, HEAD_DIM, CMP_HIDDEN)[:, rows].astype(BF16)
        wv = a["nsa_w_cv1"].reshape(depth, CMP_LEN, HEAD_DIM, CMP_HIDDEN)[:, rows].astype(BF16)
        slots = [jnp.pad(w, ((0, 0), (0, 0), (0, 0), (k * CMP_HIDDEN, (3 - k) * CMP_HIDDEN)))
                 for k, w in enumerate((wk, wk, wv, wv))]
        return jnp.concatenate(slots, axis=2).reshape(depth, CMP_STRIDE * 256, 4 * CMP_HIDDEN)

    eye2 = jnp.asarray(np.eye(N_KV, dtype=np.float32))
    ck2 = jnp.pad(a["nsa_w_ck2"], ((0, 0), (0, 0), (0, LANES - HEAD_DIM)))
    block2 = lambda w: (w[:, None, :, None, :] * eye2[None, :, None, :, None]).reshape(
        depth, N_KV * w.shape[1], N_KV * w.shape[2])
    w2 = jnp.concatenate([jnp.pad(block2(ck2), ((0, 0), (0, 0), (0, N_KV * HEAD_DIM))),
                          jnp.pad(block2(a["nsa_w_cv2"]), ((0, 0), (0, 0), (N_KV * LANES, 0)))], axis=1)
    pe_rows = lambda pe: jnp.pad(pe.reshape(depth, 1, -1), ((0, 0), (0, 7), (0, 0)))
    return {
        "w_pack": w_pack.astype(BF16),
        "norm_mix": row(a["norm_mix"]), "norm_ffn": row(a["norm_ffn"]),
        "conv_w_dw": a["conv_w_dw"], "conv_b_dw": row(a["conv_b_dw"]),
        "conv_ln_g": row(a["conv_ln_g"]), "conv_ln_b": row(a["conv_ln_b"]),
        "lru_w_conv": a["lru_w_conv"], "lru_b_conv": row(a["lru_b_conv"]),
        "lru_wa_bd": _block_diag(a["lru_w_a"]).astype(BF16), "lru_b_a": row(a["lru_b_a"]),
        "lru_wx_bd": _block_diag(a["lru_w_x"]).astype(BF16), "lru_b_x": row(a["lru_b_x"]),
        "lru_lam": row(a["lru_lam"]),
        "cmp_w_lo": chunk_weights(0), "cmp_w_hi": chunk_weights(1),
        "cmp_pe_k": pe_rows(a["nsa_pe_k"]), "cmp_pe_v": pe_rows(a["nsa_pe_v"]),
        "nsa_w_ck1": a["nsa_w_ck1"], "nsa_w_cv1": a["nsa_w_cv1"],
        "cmp_w2": w2.astype(BF16),
        "w_merge": a["w_merge"].astype(BF16), "b_merge": row(a["b_merge"]),
        "conv_w_out": a["conv_w_out"].astype(BF16), "nsa_w_out": a["nsa_w_out"].astype(BF16),
        "lru_w_out": a["lru_w_out"].astype(BF16), "w_out": a["w_out"].astype(BF16),
        "w_ffn_in": a["w_ffn_in"].astype(BF16), "w_ffn_out": a["w_ffn_out"].astype(BF16),
    }


def kernel(x, c, w_mod, b_mod, norm_mix, norm_ffn, w_in, conv_w_dw, conv_b_dw, conv_ln_g, conv_ln_b,
           conv_w_out, nsa_pe_k, nsa_w_ck1, nsa_w_ck2, nsa_pe_v, nsa_w_cv1, nsa_w_cv2, nsa_w_out,
           lru_w_conv, lru_b_conv, lru_w_a, lru_b_a, lru_w_x, lru_b_x, lru_lam, lru_w_out,
           w_merge, b_merge, w_out, w_ffn_in, w_ffn_out, final_norm):
    a = dict(norm_mix=norm_mix, norm_ffn=norm_ffn, w_in=w_in, conv_w_dw=conv_w_dw, conv_b_dw=conv_b_dw, conv_ln_g=conv_ln_g, conv_ln_b=conv_ln_b,
             conv_w_out=conv_w_out, nsa_pe_k=nsa_pe_k, nsa_w_ck1=nsa_w_ck1, nsa_w_ck2=nsa_w_ck2,
             nsa_pe_v=nsa_pe_v, nsa_w_cv1=nsa_w_cv1, nsa_w_cv2=nsa_w_cv2, nsa_w_out=nsa_w_out,
             lru_w_conv=lru_w_conv, lru_b_conv=lru_b_conv, lru_w_a=lru_w_a, lru_b_a=lru_b_a,
             lru_w_x=lru_w_x, lru_b_x=lru_b_x, lru_lam=lru_lam, lru_w_out=lru_w_out,
             w_merge=w_merge, b_merge=b_merge, w_out=w_out, w_ffn_in=w_ffn_in, w_ffn_out=w_ffn_out)
    depth = w_in.shape[0]
    s = x.shape[1]
    mod = _modulation(c, w_mod, b_mod)
    qaug = jnp.asarray(_q_slope_rows())
    ovt = jnp.asarray(_overlap_matrix(s)).astype(BF16)
    fin = final_norm.reshape(1, -1)
    p = _prepare_params(a)
    for l in range(depth):
        uconv, ylru, q, kvc, ks, kw, vst, vwt, gt = _premix(x, mod, l, qaug, p)
        kc, vct = _compress(kvc, l, p)
        onsa = _attention(q, kc, vct, ovt, ks, vst, kw, vwt, gt)
        x = _merge(x, mod, l, uconv, ylru, onsa, p)
        x = _ffn(x, mod, l, p["norm_ffn"], p, fin, final=(l == depth - 1))
    return x
```

```python
import functools

import numpy as np
import jax
import jax.numpy as jnp
from jax import lax
from jax.experimental import pallas as pl
from jax.experimental.pallas import tpu as pltpu

F32 = jnp.float32
BF16 = jnp.bfloat16

EPS = 1e-6
CONV_DIM = 512
CONV_WIDTH = 31
N_HEADS = 8
N_KV = 2
GROUP = N_HEADS // N_KV
HEAD_DIM = 64
CMP_LEN = 32
CMP_STRIDE = 16
CMP_HIDDEN = 128
SEL_BLOCK = 64
SEL_TOPK = 16
WINDOW = 512
Q_BLOCK = 256
FORCE_SCORE = 1e4
LRU_DIM = 512
LRU_HEADS = 8
LRU_CONV_WIDTH = 4
LRU_C = 8.0
N_BRANCH = 3

LANES = 128
NEG = -1e30
LOG2E = 1.4426950408889634
SLOPE_PIECES = 3
KEY_TILE = 512
WIN_TILE = 128
CMP_CHUNK = 256
CMP_PER_SEL = SEL_BLOCK // CMP_STRIDE
CMP_PAD = 8
TOKEN_TILE = 512
BRANCH_TILE = 256
CONV_HALO = 32
LRU_HALO = 8
VMEM_LIMIT = 56 * 1024 * 1024

COL_CONV = 0
COL_Q = 1024
COL_KVC = 1536
COL_K = 1792
COL_V = 2304
COL_LRU = 2688
COL_END = 3712


def _params(n_grid):
    return pltpu.CompilerParams(dimension_semantics=("arbitrary",) * n_grid,
                                vmem_limit_bytes=VMEM_LIMIT)


def _dot(a, b):
    return jnp.dot(a, b, preferred_element_type=F32)


def _dot_nt(a, b):
    return lax.dot_general(a, b, (((1,), (1,)), ((), ())), preferred_element_type=F32)


def _split_bf16(a):
    hi = a.astype(BF16)
    lo = (a - hi.astype(F32)).astype(BF16)
    return hi, lo


def _dot3(a, b):
    a_hi, a_lo = _split_bf16(a)
    b_hi, b_lo = _split_bf16(b)
    return _dot(a_hi, b_hi) + (_dot(a_hi, b_lo) + _dot(a_lo, b_hi))


def _gelu(x):
    return 0.5 * x * (1.0 + jnp.tanh(0.7978845608028654 * (x + 0.044715 * (x * x * x))))


def _sigmoid(x):
    return 0.5 * jnp.tanh(0.5 * x) + 0.5


def _silu(x):
    return x * _sigmoid(x)


def _fold8(x, op):
    groups = [x[i:i + 8, :] for i in range(0, x.shape[0], 8)]
    while len(groups) > 1:
        groups = [op(groups[i], groups[i + 1]) if i + 1 < len(groups) else groups[i]
                  for i in range(0, len(groups), 2)]
    return groups[0]


def _position_lanes(pos, lane):
    hi = (pos >> 7).astype(F32)
    lo = (pos & 127).astype(F32)
    off = lane - HEAD_DIM
    return jnp.where((off >= 0) & (off < SLOPE_PIECES), hi,
                     jnp.where((off >= SLOPE_PIECES) & (off < 2 * SLOPE_PIECES), lo, 0.0))


def _mod_norm(x, g, sc, sh):
    y = x * lax.rsqrt(jnp.mean(x * x, axis=-1, keepdims=True) + EPS)
    return (y * g) * (1.0 + sc) + sh


def _mod_kernel(c_ref, w_ref, b_ref, o_ref):
    o_ref[...] = _dot3(c_ref[...], w_ref[...]) + b_ref[...]


def _modulation(c, w_mod, b_mod):
    depth, d, n = w_mod.shape
    bsz = c.shape[0]
    rows = 8
    c_pad = jnp.zeros((rows, d), F32).at[:bsz].set(c)
    out = pl.pallas_call(
        _mod_kernel,
        grid=(depth, n // d),
        in_specs=[pl.BlockSpec((rows, d), lambda l, j: (0, 0)),
                  pl.BlockSpec((None, d, d), lambda l, j: (l, 0, j)),
                  pl.BlockSpec((None, 1, d), lambda l, j: (l, 0, j))],
        out_specs=pl.BlockSpec((None, rows, d), lambda l, j: (l, 0, j)),
        out_shape=jax.ShapeDtypeStruct((depth, rows, n), F32),
        compiler_params=_params(2),
        name="modulation",
    )(c_pad, w_mod, b_mod.reshape(depth, 1, n))
    return out[:, :bsz].reshape(depth, bsz, n // d, 1, d)


def _premix_kernel(x_ref, g_ref, sc_ref, sh_ref, w_ref, qaug_ref,
                   wc4_ref, bc4_ref, wa_ref, ba_ref, wx_ref, bx_ref, lam_ref,
                   uconv_ref, ylru_ref, q_ref, kvc_ref, ks_ref, kw_ref, vst_ref, vwt_ref, gt_ref,
                   kvc_tmp_ref, xext_ref, hcar_ref):
    i = pl.program_id(1)
    tm = x_ref.shape[0]

    @pl.when(i == 0)
    def _():
        xext_ref[0:LRU_HALO, :] = jnp.zeros((LRU_HALO, LRU_DIM), F32)
        hcar_ref[...] = jnp.zeros(hcar_ref.shape, F32)

    hb = _mod_norm(x_ref[...], g_ref[...], sc_ref[...], sh_ref[...]).astype(BF16)

    def proj(a, b):
        return _dot(hb, w_ref[:, a:b])

    u_conv = proj(COL_CONV, COL_Q)
    uconv_ref[...] = u_conv[:, 0:CONV_DIM] * _sigmoid(u_conv[:, CONV_DIM:2 * CONV_DIM])
    u_lru = proj(COL_LRU, COL_END)
    for r0 in range(0, tm, BRANCH_TILE):
        ylru_ref[r0:r0 + BRANCH_TILE, :] = _lru_rows(
            u_lru[r0:r0 + BRANCH_TILE, 0:LRU_DIM], _gelu(u_lru[r0:r0 + BRANCH_TILE, LRU_DIM:2 * LRU_DIM]),
            wc4_ref, bc4_ref, wa_ref, ba_ref, wx_ref, bx_ref, lam_ref, xext_ref, hcar_ref)
    uq = proj(COL_Q, COL_KVC)
    low_half = lax.broadcasted_iota(jnp.int32, (tm, LANES), 1) < HEAD_DIM
    for h in range(N_HEADS):
        pair = uq[:, (h // 2) * LANES:(h // 2 + 1) * LANES]
        if h % 2:
            pair = pltpu.roll(pair, HEAD_DIM, axis=1)
        q_ref[:, h * LANES:(h + 1) * LANES] = jnp.where(low_half, pair, qaug_ref[h:h + 1, :]).astype(BF16)
    ukv = proj(COL_KVC, COL_K)
    for half in range(2):
        kvc_tmp_ref[half] = ukv[:, half * LANES:(half + 1) * LANES]
    for t in range(CMP_STRIDE):
        for half in range(2):
            col = t * 256 + half * LANES
            kvc_ref[:, col:col + LANES] = (
                kvc_tmp_ref[half, pl.ds(t, tm // CMP_STRIDE, stride=CMP_STRIDE), :].astype(BF16))
    pos = i * tm + lax.broadcasted_iota(jnp.int32, (tm, LANES), 0)
    lane = lax.broadcasted_iota(jnp.int32, (tm, LANES), 1)
    kaug = _position_lanes(pos, lane)
    uk = proj(COL_K, COL_V)
    ks_ref[0] = (uk[:, 0:128] + kaug).astype(BF16)
    ks_ref[1] = (uk[:, 128:256] + kaug).astype(BF16)
    kw_ref[0] = (uk[:, 256:384] + kaug).astype(BF16)
    kw_ref[1] = (uk[:, 384:512] + kaug).astype(BF16)
    uv = proj(COL_V, COL_LRU)
    vst_ref[...] = uv[:, 0:128].T.astype(BF16)
    vwt = uv[:, 128:256].T.astype(BF16)
    for t in range(tm // WIN_TILE):
        vwt_ref[t] = vwt[:, t * WIN_TILE:(t + 1) * WIN_TILE]
    gt = _sigmoid(uv[:, 256:384]).T
    gt_ref[...] = gt[0:32, :]


def _premix(x, mod, layer, qaug, p):
    bsz, s, d = x.shape
    tm = TOKEN_TILE
    c = LRU_DIM
    assert tm == KEY_TILE and s % tm == 0 and tm % BRANCH_TILE == 0
    n_t = s // tm
    row = lambda k: pl.BlockSpec((None, None, None, 1, d), lambda b, i: (layer, b, k, 0, 0))
    full = lambda shape: pl.BlockSpec((None,) + shape, lambda b, i: (layer,) + (0,) * len(shape))
    out_shapes = (
        jax.ShapeDtypeStruct((bsz, s, CONV_DIM), F32),
        jax.ShapeDtypeStruct((bsz, s, LRU_DIM), BF16),
        jax.ShapeDtypeStruct((bsz, s, N_HEADS * LANES), BF16),
        jax.ShapeDtypeStruct((bsz, s // CMP_STRIDE, CMP_STRIDE * 256), BF16),
        jax.ShapeDtypeStruct((bsz, N_KV, s, LANES), BF16),
        jax.ShapeDtypeStruct((bsz, N_KV, s, LANES), BF16),
        jax.ShapeDtypeStruct((bsz, n_t, LANES, KEY_TILE), BF16),
        jax.ShapeDtypeStruct((bsz, s // WIN_TILE, LANES, WIN_TILE), BF16),
        jax.ShapeDtypeStruct((bsz, 32, s), F32),
    )
    out_specs = (
        pl.BlockSpec((None, tm, CONV_DIM), lambda b, i: (b, i, 0)),
        pl.BlockSpec((None, tm, LRU_DIM), lambda b, i: (b, i, 0)),
        pl.BlockSpec((None, tm, N_HEADS * LANES), lambda b, i: (b, i, 0)),
        pl.BlockSpec((None, tm // CMP_STRIDE, CMP_STRIDE * 256), lambda b, i: (b, i, 0)),
        pl.BlockSpec((None, N_KV, tm, LANES), lambda b, i: (b, 0, i, 0)),
        pl.BlockSpec((None, N_KV, tm, LANES), lambda b, i: (b, 0, i, 0)),
        pl.BlockSpec((None, None, LANES, KEY_TILE), lambda b, i: (b, i, 0, 0)),
        pl.BlockSpec((None, tm // WIN_TILE, LANES, WIN_TILE), lambda b, i: (b, i, 0, 0)),
        pl.BlockSpec((None, 32, tm), lambda b, i: (b, 0, i)),
    )
    return pl.pallas_call(
        _premix_kernel,
        grid=(bsz, n_t),
        in_specs=[pl.BlockSpec((None, tm, d), lambda b, i: (b, i, 0)),
                  full((1, d)), row(1), row(0), full((d, COL_END)),
                  pl.BlockSpec((N_HEADS, LANES), lambda b, i: (0, 0)),
                  full((LRU_CONV_WIDTH, c)), full((1, c)),
                  full((c, c)), full((1, c)), full((c, c)), full((1, c)), full((1, c))],
        out_specs=out_specs,
        out_shape=out_shapes,
        scratch_shapes=[pltpu.VMEM((2, tm, LANES), F32),
                        pltpu.VMEM((BRANCH_TILE + LRU_HALO, c), F32),
                        pltpu.VMEM((8, c), F32)],
        compiler_params=_params(2),
        name="premix",
    )(x, p["norm_mix"], mod, mod, p["w_pack"], qaug,
      p["lru_w_conv"], p["lru_b_conv"], p["lru_wa_bd"], p["lru_b_a"], p["lru_wx_bd"], p["lru_b_x"],
      p["lru_lam"])


def _conv_rows(v, wdw_ref, bdw_ref, lng_ref, lnb_ref, vext_ref, vsh_ref):
    ts, c = v.shape
    vext_ref[CONV_HALO:CONV_HALO + ts, :] = v
    span = ts + CONV_HALO - 8
    for phase in range(1, 8):
        vsh_ref[phase - 1] = vext_ref[pl.ds(phase, span), :]
    acc = jnp.broadcast_to(bdw_ref[...], (ts, c))
    first = CONV_HALO - (CONV_WIDTH - 1)
    for j in range(CONV_WIDTH):
        base, phase = (first + j) // 8 * 8, (first + j) % 8
        rows = vext_ref[base:base + ts, :] if phase == 0 else vsh_ref[phase - 1, base:base + ts, :]
        acc = acc + wdw_ref[j:j + 1, :] * rows
    vext_ref[0:CONV_HALO, :] = v[ts - CONV_HALO:ts, :]
    mu = jnp.mean(acc, axis=-1, keepdims=True)
    cen = acc - mu
    var = jnp.mean(cen * cen, axis=-1, keepdims=True)
    ln = (cen * lax.rsqrt(var + EPS)) * lng_ref[...] + lnb_ref[...]
    return _silu(ln).astype(BF16)


def _lru_rows(ux, gate, wc4_ref, bc4_ref, wa_ref, ba_ref, wx_ref, bx_ref, lam_ref, xext_ref, hcar_ref):
    ts, c = ux.shape
    xext_ref[LRU_HALO:LRU_HALO + ts, :] = ux
    xr = jnp.broadcast_to(bc4_ref[...], (ts, c))
    first = LRU_HALO - (LRU_CONV_WIDTH - 1)
    for j in range(LRU_CONV_WIDTH):
        xr = xr + wc4_ref[j:j + 1, :] * xext_ref[pl.ds(first + j, ts), :]
    xext_ref[0:LRU_HALO, :] = ux[ts - LRU_HALO:ts, :]
    xb = xr.astype(BF16)
    r = _sigmoid(_dot(xb, wa_ref[...]) + ba_ref[...])
    gate_i = _sigmoid(_dot(xb, wx_ref[...]) + bx_ref[...])
    z = -lam_ref[...]
    softplus = jnp.maximum(z, 0.0) + jnp.log(1.0 + jnp.exp(-jnp.abs(z)))
    log_a = (-LRU_C * r) * softplus
    a = jnp.exp(log_a)
    b = jnp.sqrt(1.0 - a * a) * (gate_i * xr)
    rows = lax.broadcasted_iota(jnp.int32, (ts, c), 0)
    shift = 1
    while shift < ts:
        a_prev = pltpu.roll(a, shift, axis=0)
        b_prev = pltpu.roll(b, shift, axis=0)
        live = rows >= shift
        b = jnp.where(live, a * b_prev + b, b)
        a = jnp.where(live, a * a_prev, a)
        shift *= 2
    h = a * hcar_ref[0:1, :] + b
    hcar_ref[...] = jnp.broadcast_to(h[ts - 1:ts, :], hcar_ref.shape)
    return (h * gate).astype(BF16)


def _compress_kernel(x_ref, wlo_ref, whi_ref, pek_ref, pev_ref, w1k_ref, w1v_ref, w2_ref,
                     kc_ref, vct_ref):
    nc = x_ref.shape[0]
    x = x_ref[...]
    h_lo = _dot(x, wlo_ref[...])
    h_hi = _dot(x, whi_ref[...])
    h_next = pltpu.roll(h_hi, nc - 1, axis=0)
    bk = _dot3(pek_ref[...], w1k_ref[...])[0:1, :]
    bv = _dot3(pev_ref[...], w1v_ref[...])[0:1, :]
    bias = jnp.concatenate([bk, bk, bv, bv], axis=1)
    hid = _gelu(h_lo + h_next + bias).astype(BF16)
    out = _dot(hid, w2_ref[...])
    cend = lax.broadcasted_iota(jnp.int32, (nc, LANES), 0) * CMP_STRIDE + (CMP_LEN - 1)
    lane = lax.broadcasted_iota(jnp.int32, (nc, LANES), 1)
    aug = _position_lanes(cend, lane)
    kc_ref[0] = (out[:, 0:128] + aug).astype(BF16)
    kc_ref[1] = (out[:, 128:256] + aug).astype(BF16)
    vct = out[:, 256:384].T.astype(BF16)
    for t in range(nc // CMP_CHUNK):
        vct_ref[t] = vct[:, t * CMP_CHUNK:(t + 1) * CMP_CHUNK]


def _compress(kvc, layer, p):
    bsz, nc, width = kvc.shape
    x = kvc
    full = lambda shape: pl.BlockSpec((None,) + shape, lambda b: (layer,) + (0,) * len(shape))
    return pl.pallas_call(
        _compress_kernel,
        grid=(bsz,),
        in_specs=[pl.BlockSpec((None, nc, width), lambda b: (b, 0, 0)),
                  full((width, 512)), full((width, 512)),
                  full((8, CMP_LEN * HEAD_DIM)), full((8, CMP_LEN * HEAD_DIM)),
                  full((CMP_LEN * HEAD_DIM, CMP_HIDDEN)), full((CMP_LEN * HEAD_DIM, CMP_HIDDEN)),
                  full((512, 384))],
        out_specs=(pl.BlockSpec((None, N_KV, nc, LANES), lambda b: (b, 0, 0, 0)),
                   pl.BlockSpec((None, nc // CMP_CHUNK, LANES, CMP_CHUNK), lambda b: (b, 0, 0, 0))),
        out_shape=(jax.ShapeDtypeStruct((bsz, N_KV, nc, LANES), BF16),
                   jax.ShapeDtypeStruct((bsz, nc // CMP_CHUNK, LANES, CMP_CHUNK), BF16)),
        compiler_params=_params(1),
        name="compress",
    )(x, p["cmp_w_lo"], p["cmp_w_hi"], p["cmp_pe_k"], p["cmp_pe_v"], p["nsa_w_ck1"], p["nsa_w_cv1"],
      p["cmp_w2"])


def _attn_kernel(q_ref, kc_ref, vct_ref, ks_ref, vst_ref, kw_ref, vwt_ref, gt_ref,
                 o_ref, s0_ref, s1_ref, w_ref, psum_ref, selneg_ref, m_ref, l_ref, acc_ref, flag_ref, list_ref):
    qb = pl.program_id(2)
    q0 = qb * Q_BLOCK
    n_sel = selneg_ref.shape[0]
    n_rows = GROUP * Q_BLOCK
    qg = jnp.concatenate([q_ref[:, r * LANES:(r + 1) * LANES] for r in range(GROUP)], axis=0)
    qpos = q0 + (lax.broadcasted_iota(jnp.int32, (1, n_rows), 1) & (Q_BLOCK - 1))

    n_chunks = ((q0 + Q_BLOCK - CMP_LEN) // CMP_STRIDE + CMP_CHUNK) // CMP_CHUNK
    cend = lax.broadcasted_iota(jnp.int32, (CMP_CHUNK, n_rows), 0) * CMP_STRIDE + (CMP_LEN - 1)

    def cmp_branch(n_ch):
        def fn():
            s_all = _dot_nt(kc_ref[0:n_ch * CMP_CHUNK, :], qg)
            scores = []
            m_c = None
            for ci in range(n_ch):
                s = s_all[ci * CMP_CHUNK:(ci + 1) * CMP_CHUNK, :]
                if ci >= n_ch - 2:
                    s = jnp.where(cend + ci * CMP_CHUNK * CMP_STRIDE <= qpos, s, NEG)
                scores.append(s)
                mi = jnp.max(s, axis=0, keepdims=True)
                m_c = mi if m_c is None else jnp.maximum(m_c, mi)
            m_c = jnp.where(m_c > 0.5 * NEG, m_c, 0.0)
            probs = [jnp.exp2(s - m_c) for s in scores]
            den = probs[0].sum(axis=0, keepdims=True)
            for p in probs[1:]:
                den = den + jnp.sum(p, axis=0, keepdims=True)
            inv_den = 1.0 / jnp.where(den > 0, den, 1.0)
            o_c = None
            for ci, p in enumerate(probs):
                p = p * inv_den
                part = _dot(vct_ref[ci], p.astype(BF16))
                o_c = part if o_c is None else o_c + part
                p_sum = p[:, 0:Q_BLOCK]
                for r in range(1, GROUP):
                    p_sum = p_sum + p[:, r * Q_BLOCK:(r + 1) * Q_BLOCK]
                for h in range(Q_BLOCK // LANES):
                    psum_ref[h, CMP_PAD + ci * CMP_CHUNK:CMP_PAD + (ci + 1) * CMP_CHUNK, :] = (
                        p_sum[:, h * LANES:(h + 1) * LANES])
            assert CMP_LEN == 2 * CMP_STRIDE and CMP_PER_SEL == 4
            n_blk = n_ch * CMP_CHUNK // CMP_PER_SEL
            halves = []
            for h in range(Q_BLOCK // LANES):
                taps = [psum_ref[h, pl.ds(CMP_PAD - 1 + k, n_blk, stride=CMP_PER_SEL), :]
                        for k in range(CMP_PER_SEL + 1)]
                imp_h = 0.5 * (taps[0] + taps[4]) + (taps[1] + taps[2] + taps[3])
                if n_blk < n_sel:
                    last = CMP_PAD + n_ch * CMP_CHUNK - 1
                    imp_h = jnp.concatenate([imp_h, 0.5 * psum_ref[h, last:last + 1, :],
                                             jnp.zeros((n_sel - n_blk - 1, LANES), F32)], axis=0)
                halves.append(imp_h)
            return o_c, jnp.concatenate(halves, axis=1)
        return fn

    for h in range(Q_BLOCK // LANES):
        psum_ref[h, 0:CMP_PAD, :] = jnp.zeros((CMP_PAD, LANES), F32)
    o_c, imp = lax.switch(n_chunks - 1, [cmp_branch(n) for n in range(1, kc_ref.shape[0] // CMP_CHUNK + 1)])

    blk = lax.broadcasted_iota(jnp.int32, (n_sel, LANES), 0)
    blk_f = blk.astype(F32)
    works = []
    for h in range(Q_BLOCK // LANES):
        qp = q0 + h * LANES + lax.broadcasted_iota(jnp.int32, (n_sel, LANES), 1)
        qid = qp // SEL_BLOCK
        forced = (blk == 0) | (blk == qid) | (blk == qid - 1)
        valid = blk * SEL_BLOCK <= qp
        works.append(jnp.where(forced, -jnp.inf, jnp.where(valid, imp[:, h * LANES:(h + 1) * LANES], -1.0)))
    for _ in range(min(SEL_TOPK, n_sel) - 3):
        for h, work_h in enumerate(works):
            best = jnp.max(work_h, axis=0, keepdims=True)
            first = jnp.min(jnp.where(work_h == best, blk_f, float(n_sel)), axis=0, keepdims=True)
            works[h] = jnp.where(blk_f == first, -jnp.inf, work_h)
    work = jnp.concatenate(works, axis=1)

    n_tiles = q0 // KEY_TILE + 1
    krow = lax.broadcasted_iota(jnp.int32, (SEL_BLOCK, n_rows), 0)

    def scores_into(buf_ref, kt):
        k0 = pl.multiple_of(kt * KEY_TILE, KEY_TILE)
        buf_ref[...] = _dot_nt(ks_ref[pl.ds(k0, KEY_TILE), :], qg)

    k_last = pl.multiple_of((n_tiles - 1) * KEY_TILE, KEY_TILE)
    s_both = _dot_nt(jnp.concatenate([ks_ref[pl.ds(k_last, KEY_TILE), :], ks_ref[0:LANES, :]], axis=0), qg)
    s0_ref[...] = s_both[0:KEY_TILE, :]
    hrow = lax.broadcasted_iota(jnp.int32, (LANES, n_rows), 0)
    s_first = jnp.where((hrow < SEL_BLOCK) & (hrow <= qpos), s_both[KEY_TILE:KEY_TILE + LANES, :], NEG)
    m_first = jnp.max(s_first, axis=0, keepdims=True)
    p_first = jnp.exp2(s_first - m_first)
    m_ref[...] = m_first
    l_ref[...] = jnp.sum(p_first, axis=0, keepdims=True)
    acc_ref[...] = _dot(vst_ref[0, :, 0:LANES], p_first.astype(BF16))

    n_win = (WINDOW + Q_BLOCK) // WIN_TILE
    t0 = jnp.maximum(q0 - WINDOW, 0) // WIN_TILE
    kstart = pl.multiple_of(t0 * WIN_TILE, WIN_TILE)
    w_ref[...] = _dot_nt(kw_ref[pl.ds(kstart, n_win * WIN_TILE), :], qg)
    wrow = lax.broadcasted_iota(jnp.int32, (WIN_TILE, n_rows), 0)
    qrel = qpos - kstart

    top = None
    for t in range(n_win):
        newest = qrel - t * WIN_TILE
        visible = wrow <= newest
        if t * WIN_TILE < Q_BLOCK:
            visible = visible & (wrow > newest - WINDOW)
        st = jnp.where(visible, w_ref[t * WIN_TILE:(t + 1) * WIN_TILE, :], NEG)
        w_ref[t * WIN_TILE:(t + 1) * WIN_TILE, :] = st
        mt = _fold8(st, jnp.maximum)
        top = mt if top is None else jnp.maximum(top, mt)
    m_w = jnp.max(top, axis=0, keepdims=True)
    total = None
    o_w = None
    for t in range(n_win):
        pt = jnp.exp2(w_ref[t * WIN_TILE:(t + 1) * WIN_TILE, :] - m_w)
        lt = _fold8(pt, jnp.add)
        total = lt if total is None else total + lt
        part = _dot(vwt_ref[t0 + t], pt.astype(BF16))
        o_w = part if o_w is None else o_w + part
    o_w = o_w / jnp.sum(total, axis=0, keepdims=True)

    taken = (work == -jnp.inf) & (lax.broadcasted_iota(jnp.int32, (n_sel, Q_BLOCK), 0) > 0)
    chosen = jnp.where(taken, 1.0, 0.0)
    selneg = jnp.where(taken, 0.0, NEG)
    selneg_ref[...] = jnp.concatenate([selneg] * GROUP, axis=1)
    blocks_per_tile = KEY_TILE // SEL_BLOCK
    any_q = jnp.max(chosen, axis=1, keepdims=True)
    for t in range(n_sel // blocks_per_tile):
        tile_any = jnp.max(any_q[t * blocks_per_tile:(t + 1) * blocks_per_tile, :])
        flag_ref[t] = (tile_any > 0).astype(jnp.int32)

    def compact(kt, n):
        @pl.when(flag_ref[kt] > 0)
        def _():
            list_ref[n] = kt
        return n + flag_ref[kt]

    n_act = lax.fori_loop(0, n_tiles - 1, compact, 0)
    list_ref[n_act] = 0

    def sel_tile(buf_ref, kt, causal, n_blocks=KEY_TILE // SEL_BLOCK):
        sel_rows = selneg_ref[pl.ds(pl.multiple_of(kt * blocks_per_tile, blocks_per_tile),
                                    blocks_per_tile), :]

        def block_scores(j):
            return buf_ref[j * SEL_BLOCK:(j + 1) * SEL_BLOCK, :]

        m_old = m_ref[...]
        top = None
        for j in range(n_blocks):
            sj = block_scores(j)
            if causal:
                sj = jnp.where(krow <= qpos - (kt * KEY_TILE + j * SEL_BLOCK), sj, NEG)
                buf_ref[j * SEL_BLOCK:(j + 1) * SEL_BLOCK, :] = sj
            mj = _fold8(sj, jnp.maximum) + sel_rows[j:j + 1, :]
            top = mj if top is None else jnp.maximum(top, mj)
        m_new = jnp.maximum(m_old, jnp.max(top, axis=0, keepdims=True))
        alpha = jnp.exp2(m_old - m_new)
        total = None
        probs = []
        for j in range(n_blocks):
            pj = jnp.exp2(block_scores(j) - (m_new - sel_rows[j:j + 1, :]))
            lj = _fold8(pj, jnp.add)
            total = lj if total is None else total + lj
            probs.append(pj.astype(BF16))
        m_ref[...] = m_new
        l_ref[...] = alpha * l_ref[...] + jnp.sum(total, axis=0, keepdims=True)
        acc_ref[...] = alpha * acc_ref[...] + _dot(vst_ref[kt, :, 0:n_blocks * SEL_BLOCK],
                                                   jnp.concatenate(probs, axis=0))

    half_tile = (q0 % KEY_TILE) + Q_BLOCK <= KEY_TILE // 2

    @pl.when(half_tile)
    def _():
        scores_into(s1_ref, list_ref[0])
        sel_tile(s0_ref, n_tiles - 1, True, n_blocks=KEY_TILE // SEL_BLOCK // 2)

    @pl.when(jnp.logical_not(half_tile))
    def _():
        scores_into(s1_ref, list_ref[0])
        sel_tile(s0_ref, n_tiles - 1, True)

    def pipelined(i, carry):
        scores_into(s0_ref, list_ref[2 * i + 1])
        sel_tile(s1_ref, list_ref[2 * i], False)

        @pl.when(2 * i + 1 < n_act)
        def _():
            scores_into(s1_ref, list_ref[2 * i + 2])
            sel_tile(s0_ref, list_ref[2 * i + 1], False)
        return carry

    lax.fori_loop(0, (n_act + 1) // 2, pipelined, 0)
    o_s = acc_ref[...] / l_ref[...]

    def gate(branch):
        return jnp.concatenate([gt_ref[branch * GROUP + r:branch * GROUP + r + 1, :] for r in range(GROUP)],
                               axis=1)
    o = gate(0) * o_c + gate(1) * o_s + gate(2) * o_w
    o_ref[...] = jnp.concatenate([o[:, r * Q_BLOCK:(r + 1) * Q_BLOCK].T for r in range(GROUP)],
                                 axis=1).astype(BF16)


def _attention(q, kc, vct, ks, vst, kw, vwt, gt):
    bsz, s, _ = q.shape
    nc = kc.shape[2]
    n_sel = s // SEL_BLOCK
    n_qb = s // Q_BLOCK
    n_kt = s // KEY_TILE
    n_wt = s // WIN_TILE
    n_ch = nc // CMP_CHUNK
    n_rows = GROUP * Q_BLOCK
    assert s % KEY_TILE == 0 and s >= (WINDOW + Q_BLOCK) and n_sel % 8 == 0 and nc % CMP_CHUNK == 0
    assert KEY_TILE % Q_BLOCK == 0 and Q_BLOCK % WIN_TILE == 0
    return pl.pallas_call(
        _attn_kernel,
        grid=(bsz, N_KV, n_qb),
        in_specs=[pl.BlockSpec((None, Q_BLOCK, GROUP * LANES), lambda b, g, i: (b, i, g)),
                  pl.BlockSpec((None, None, nc, LANES), lambda b, g, i: (b, g, 0, 0)),
                  pl.BlockSpec((None, n_ch, HEAD_DIM, CMP_CHUNK), lambda b, g, i: (b, 0, g, 0)),
                  pl.BlockSpec((None, None, s, LANES), lambda b, g, i: (b, g, 0, 0)),
                  pl.BlockSpec((None, n_kt, HEAD_DIM, KEY_TILE), lambda b, g, i: (b, 0, g, 0)),
                  pl.BlockSpec((None, None, s, LANES), lambda b, g, i: (b, g, 0, 0)),
                  pl.BlockSpec((None, n_wt, HEAD_DIM, WIN_TILE), lambda b, g, i: (b, 0, g, 0)),
                  pl.BlockSpec((None, 16, Q_BLOCK), lambda b, g, i: (b, g, i))],
        out_specs=pl.BlockSpec((None, Q_BLOCK, GROUP * HEAD_DIM), lambda b, g, i: (b, i, g)),
        out_shape=jax.ShapeDtypeStruct((bsz, s, N_HEADS * HEAD_DIM), BF16),
        scratch_shapes=[pltpu.VMEM((KEY_TILE, n_rows), F32),
                        pltpu.VMEM((KEY_TILE, n_rows), F32),
                        pltpu.VMEM((WINDOW + Q_BLOCK, n_rows), F32),
                        pltpu.VMEM((Q_BLOCK // LANES, CMP_PAD + nc, LANES), F32),
                        pltpu.VMEM((n_sel, n_rows), F32),
                        pltpu.VMEM((1, n_rows), F32),
                        pltpu.VMEM((1, n_rows), F32),
                        pltpu.VMEM((HEAD_DIM, n_rows), F32),
                        pltpu.SMEM((n_kt,), jnp.int32),
                        pltpu.SMEM((n_kt + 1,), jnp.int32)],
        compiler_params=_params(3),
        name="nsa_attention",
    )(q, kc, vct, ks, vst, kw, vwt, gt)


def _merge_kernel(x_ref, g_ref, sc_ref, sh_ref, gate_ref, uconv_ref, ylru_ref, onsa_ref,
                  wdw_ref, bdw_ref, lng_ref, lnb_ref, wm_ref, bm_ref, wc_ref, wn_ref, wl_ref, wo_ref,
                  o_ref, vext_ref, vsh_ref):
    tm, d = x_ref.shape

    @pl.when(pl.program_id(1) == 0)
    def _():
        vext_ref[0:CONV_HALO, :] = jnp.zeros((CONV_HALO, CONV_DIM), F32)

    x = x_ref[...]
    hb = _mod_norm(x, g_ref[...], sc_ref[...], sh_ref[...]).astype(BF16)

    def merge_gate(k):
        return _sigmoid(_dot(hb, wm_ref[:, k * d:(k + 1) * d]) + bm_ref[:, k * d:(k + 1) * d])

    early = merge_gate(1) * _dot(onsa_ref[...], wn_ref[...]) + merge_gate(2) * _dot(ylru_ref[...], wl_ref[...])
    vconv = jnp.concatenate(
        [_conv_rows(uconv_ref[r0:r0 + BRANCH_TILE, :], wdw_ref, bdw_ref, lng_ref, lnb_ref, vext_ref, vsh_ref)
         for r0 in range(0, tm, BRANCH_TILE)], axis=0)
    merged = early + merge_gate(0) * _dot(vconv, wc_ref[...])
    o_ref[...] = x + gate_ref[...] * _dot(merged.astype(BF16), wo_ref[...])


def _merge(x, mod, layer, uconv, ylru, onsa, p):
    bsz, s, d = x.shape
    tm = TOKEN_TILE
    ts = BRANCH_TILE
    c = CONV_DIM
    assert tm % ts == 0
    row = lambda k: pl.BlockSpec((None, None, None, 1, d), lambda b, i: (layer, b, k, 0, 0))
    full = lambda shape: pl.BlockSpec((None,) + shape, lambda b, i: (layer,) + (0,) * len(shape))
    tok = lambda w: pl.BlockSpec((None, tm, w), lambda b, i: (b, i, 0))
    return pl.pallas_call(
        _merge_kernel,
        grid=(bsz, s // tm),
        in_specs=[tok(d), full((1, d)), row(1), row(0), row(2), tok(c), tok(c), tok(c),
                  full((CONV_WIDTH, c)), full((1, c)), full((1, c)), full((1, c)),
                  full((d, N_BRANCH * d)), full((1, N_BRANCH * d)),
                  full((c, d)), full((c, d)), full((c, d)), full((d, d))],
        out_specs=tok(d),
        out_shape=jax.ShapeDtypeStruct((bsz, s, d), F32),
        scratch_shapes=[pltpu.VMEM((ts + CONV_HALO, c), F32),
                        pltpu.VMEM((7, ts + CONV_HALO - 8, c), F32)],
        compiler_params=_params(2),
        name="merge",
    )(x, p["norm_mix"], mod, mod, mod, uconv, ylru, onsa,
      p["conv_w_dw"], p["conv_b_dw"], p["conv_ln_g"], p["conv_ln_b"],
      p["w_merge"], p["b_merge"], p["conv_w_out"], p["nsa_w_out"], p["lru_w_out"], p["w_out"])


def _ffn_kernel(x_ref, g_ref, sc_ref, sh_ref, gate_ref, wa_ref, wb_ref, wo_ref, fin_ref, o_ref,
                *, chunk, final):
    x = x_ref[...]
    hb = _mod_norm(x, g_ref[...], sc_ref[...], sh_ref[...]).astype(BF16)
    ffn = wa_ref.shape[1]
    acc = None
    for c0 in range(0, ffn, chunk):
        a = _dot(hb, wa_ref[:, c0:c0 + chunk])
        b = _dot(hb, wb_ref[:, c0:c0 + chunk])
        part = _dot((_silu(a) * b).astype(BF16), wo_ref[c0:c0 + chunk, :])
        acc = part if acc is None else acc + part
    y = x + gate_ref[...] * acc
    if final:
        y = (y * lax.rsqrt(jnp.mean(y * y, axis=-1, keepdims=True) + EPS)) * fin_ref[...]
    o_ref[...] = y


def _ffn(x, mod, layer, norm_g, p, final_norm, final):
    bsz, s, d = x.shape
    tm = TOKEN_TILE
    ffn = p["w_ffn_out"].shape[1]
    chunk = 256
    assert ffn % chunk == 0
    row = lambda k: pl.BlockSpec((None, None, None, 1, d), lambda b, i: (layer, b, k, 0, 0))
    full = lambda shape: pl.BlockSpec((None,) + shape, lambda b, i: (layer,) + (0,) * len(shape))
    half = lambda k: pl.BlockSpec((None, d, ffn), lambda b, i: (layer, 0, k))
    tok = pl.BlockSpec((None, tm, d), lambda b, i: (b, i, 0))
    return pl.pallas_call(
        functools.partial(_ffn_kernel, chunk=chunk, final=final),
        grid=(bsz, s // tm),
        in_specs=[tok, full((1, d)), row(4), row(3), row(5),
                  half(0), half(1), full((ffn, d)), pl.BlockSpec((1, d), lambda b, i: (0, 0))],
        out_specs=tok,
        out_shape=jax.ShapeDtypeStruct((bsz, s, d), F32),
        compiler_params=_params(2),
        name="ffn",
    )(x, norm_g, mod, mod, mod, p["w_ffn_in"], p["w_ffn_in"], p["w_ffn_out"], final_norm)


def _pack_input_projection(w_in):
    lead = w_in.shape[:-1]
    q0, kv0, gate0, lru0 = 1024, 1536, 2304, 2328
    keep = ((0, 0),) * len(lead)

    gap = jnp.zeros(lead + (LANES - HEAD_DIM,), BF16)

    def head_slots(w, n):
        pieces = []
        for h in range(n):
            pieces += [w[..., h * HEAD_DIM:(h + 1) * HEAD_DIM].astype(BF16), gap]
        return pieces

    kv = lambda i: w_in[..., kv0 + i * 128:kv0 + (i + 1) * 128]
    gates = w_in[..., gate0:gate0 + N_KV * GROUP * 3].reshape(lead + (N_KV, GROUP, 3))
    gates = jnp.swapaxes(gates, -1, -2).reshape(lead + (N_KV, 3 * GROUP))
    gates = jnp.pad(gates, keep + ((0, 0), (0, 4))).reshape(lead + (N_KV * 16,))
    gates = jnp.pad(gates, keep + ((0, LANES - N_KV * 16),))
    cast = lambda w: w.astype(BF16)
    return jnp.concatenate(
        [cast(w_in[..., 0:q0]), cast(w_in[..., q0:kv0] * (HEAD_DIM ** -0.5 * LOG2E)), cast(kv(0)), cast(kv(1))]
        + head_slots(kv(2), N_KV) + head_slots(kv(4), N_KV)
        + [cast(kv(3)), cast(kv(5)), cast(gates), cast(w_in[..., lru0:lru0 + 2 * LRU_DIM])], axis=-1)


def _bf16_pieces(x):
    pieces = []
    for _ in range(SLOPE_PIECES):
        piece = float(np.asarray(x, np.float32).astype(BF16).astype(np.float64))
        pieces.append(piece)
        x = x - piece
    return pieces


def _q_slope_rows():
    rows = np.zeros((N_HEADS, LANES), np.float32)
    for h in range(N_HEADS):
        slope = 2.0 ** (-8.0 * (h + 1) / N_HEADS) * LOG2E
        rows[h, HEAD_DIM:HEAD_DIM + SLOPE_PIECES] = _bf16_pieces(slope * 128.0)
        rows[h, HEAD_DIM + SLOPE_PIECES:HEAD_DIM + 2 * SLOPE_PIECES] = _bf16_pieces(slope)
    return rows


def _block_diag(w):
    depth, heads, n, _ = w.shape
    eye = jnp.asarray(np.eye(heads, dtype=np.float32))
    return (w[:, :, :, None, :] * eye[None, :, None, :, None]).reshape(depth, heads * n, heads * n)


def _prepare_params(a):
    depth = a["w_in"].shape[0]
    w_pack = _pack_input_projection(a["w_in"])
    assert w_pack.shape[-1] == COL_END
    row = lambda v: v.reshape(depth, 1, -1)

    def chunk_weights(half):
        rows = slice(half * CMP_STRIDE, (half + 1) * CMP_STRIDE)
        wk = a["nsa_w_ck1"].reshape(depth, CMP_LEN, HEAD_DIM, CMP_HIDDEN)[:, rows].astype(BF16)
        wv = a["nsa_w_cv1"].reshape(depth, CMP_LEN, HEAD_DIM, CMP_HIDDEN)[:, rows].astype(BF16)
        slots = [jnp.pad(w, ((0, 0), (0, 0), (0, 0), (k * CMP_HIDDEN, (3 - k) * CMP_HIDDEN)))
                 for k, w in enumerate((wk, wk, wv, wv))]
        return jnp.concatenate(slots, axis=2).reshape(depth, CMP_STRIDE * 256, 4 * CMP_HIDDEN)

    eye2 = jnp.asarray(np.eye(N_KV, dtype=np.float32))
    ck2 = jnp.pad(a["nsa_w_ck2"], ((0, 0), (0, 0), (0, LANES - HEAD_DIM)))
    block2 = lambda w: (w[:, None, :, None, :] * eye2[None, :, None, :, None]).reshape(
        depth, N_KV * w.shape[1], N_KV * w.shape[2])
    w2 = jnp.concatenate([jnp.pad(block2(ck2), ((0, 0), (0, 0), (0, N_KV * HEAD_DIM))),
                          jnp.pad(block2(a["nsa_w_cv2"]), ((0, 0), (0, 0), (N_KV * LANES, 0)))], axis=1)
    pe_rows = lambda pe: jnp.pad(pe.reshape(depth, 1, -1), ((0, 0), (0, 7), (0, 0)))
    return {
        "w_pack": w_pack.astype(BF16),
        "norm_mix": row(a["norm_mix"]), "norm_ffn": row(a["norm_ffn"]),
        "conv_w_dw": a["conv_w_dw"], "conv_b_dw": row(a["conv_b_dw"]),
        "conv_ln_g": row(a["conv_ln_g"]), "conv_ln_b": row(a["conv_ln_b"]),
        "lru_w_conv": a["lru_w_conv"], "lru_b_conv": row(a["lru_b_conv"]),
        "lru_wa_bd": _block_diag(a["lru_w_a"]).astype(BF16), "lru_b_a": row(a["lru_b_a"]),
        "lru_wx_bd": _block_diag(a["lru_w_x"]).astype(BF16), "lru_b_x": row(a["lru_b_x"]),
        "lru_lam": row(a["lru_lam"]),
        "cmp_w_lo": chunk_weights(0), "cmp_w_hi": chunk_weights(1),
        "cmp_pe_k": pe_rows(a["nsa_pe_k"]), "cmp_pe_v": pe_rows(a["nsa_pe_v"]),
        "nsa_w_ck1": a["nsa_w_ck1"], "nsa_w_cv1": a["nsa_w_cv1"],
        "cmp_w2": w2.astype(BF16),
        "w_merge": a["w_merge"].astype(BF16), "b_merge": row(a["b_merge"]),
        "conv_w_out": a["conv_w_out"].astype(BF16), "nsa_w_out": a["nsa_w_out"].astype(BF16),
        "lru_w_out": a["lru_w_out"].astype(BF16), "w_out": a["w_out"].astype(BF16),
        "w_ffn_in": a["w_ffn_in"].astype(BF16), "w_ffn_out": a["w_ffn_out"].astype(BF16),
    }


def kernel(x, c, w_mod, b_mod, norm_mix, norm_ffn, w_in, conv_w_dw, conv_b_dw, conv_ln_g, conv_ln_b,
           conv_w_out, nsa_pe_k, nsa_w_ck1, nsa_w_ck2, nsa_pe_v, nsa_w_cv1, nsa_w_cv2, nsa_w_out,
           lru_w_conv, lru_b_conv, lru_w_a, lru_b_a, lru_w_x, lru_b_x, lru_lam, lru_w_out,
           w_merge, b_merge, w_out, w_ffn_in, w_ffn_out, final_norm):
    a = dict(norm_mix=norm_mix, norm_ffn=norm_ffn, w_in=w_in, conv_w_dw=conv_w_dw, conv_b_dw=conv_b_dw, conv_ln_g=conv_ln_g, conv_ln_b=conv_ln_b,
             conv_w_out=conv_w_out, nsa_pe_k=nsa_pe_k, nsa_w_ck1=nsa_w_ck1, nsa_w_ck2=nsa_w_ck2,
             nsa_pe_v=nsa_pe_v, nsa_w_cv1=nsa_w_cv1, nsa_w_cv2=nsa_w_cv2, nsa_w_out=nsa_w_out,
             lru_w_conv=lru_w_conv, lru_b_conv=lru_b_conv, lru_w_a=lru_w_a, lru_b_a=lru_b_a,
             lru_w_x=lru_w_x, lru_b_x=lru_b_x, lru_lam=lru_lam, lru_w_out=lru_w_out,
             w_merge=w_merge, b_merge=b_merge, w_out=w_out, w_ffn_in=w_ffn_in, w_ffn_out=w_ffn_out)
    depth = w_in.shape[0]
    s = x.shape[1]
    mod = _modulation(c, w_mod, b_mod)
    qaug = jnp.asarray(_q_slope_rows())
    fin = final_norm.reshape(1, -1)
    p = _prepare_params(a)
    for l in range(depth):
        uconv, ylru, q, kvc, ks, kw, vst, vwt, gt = _premix(x, mod, l, qaug, p)
        kc, vct = _compress(kvc, l, p)
        onsa = _attention(q, kc, vct, ks, vst, kw, vwt, gt)
        x = _merge(x, mod, l, uconv, ylru, onsa, p)
        x = _ffn(x, mod, l, p["norm_ffn"], p, fin, final=(l == depth - 1))
    return x
```

```python
import functools

import numpy as np
import jax
import jax.numpy as jnp
from jax import lax
from jax.experimental import pallas as pl
from jax.experimental.pallas import tpu as pltpu

F32 = jnp.float32
BF16 = jnp.bfloat16

EPS = 1e-6
CONV_DIM = 512
CONV_WIDTH = 31
N_HEADS = 8
N_KV = 2
GROUP = N_HEADS // N_KV
HEAD_DIM = 64
CMP_LEN = 32
CMP_STRIDE = 16
CMP_HIDDEN = 128
SEL_BLOCK = 64
SEL_TOPK = 16
WINDOW = 512
Q_BLOCK = 256
LRU_DIM = 512
LRU_CONV_WIDTH = 4
LRU_C = 8.0
N_BRANCH = 3

LANES = 128
NEG = -1e30
LOG2E = 1.4426950408889634
SLOPE_PIECES = 3
POS_BITS = 7
KVC_WIDTH = 2 * LANES
CMP_SLOTS = 4
FFN_CHUNK = 256
GATE_ROWS = 16
KEY_TILE = 512
WIN_TILE = 128
CMP_CHUNK = 256
CMP_PER_SEL = SEL_BLOCK // CMP_STRIDE
CMP_PAD = 8
TOKEN_TILE = 512
BRANCH_TILE = 256
CONV_HALO = 32
LRU_HALO = 8
VMEM_LIMIT = 56 * 1024 * 1024

COL_CONV = 0
COL_Q = 1024
COL_KVC = 1536
COL_K = 1792
COL_V = 2304
COL_LRU = 2688
COL_END = 3712


def _params(n_grid):
    return pltpu.CompilerParams(dimension_semantics=("arbitrary",) * n_grid,
                                vmem_limit_bytes=VMEM_LIMIT)


def _dot(a, b):
    return jnp.dot(a, b, preferred_element_type=F32)


def _dot_nt(a, b):
    return lax.dot_general(a, b, (((1,), (1,)), ((), ())), preferred_element_type=F32)


def _split_bf16(a):
    hi = a.astype(BF16)
    lo = (a - hi.astype(F32)).astype(BF16)
    return hi, lo


def _dot3(a, b):
    a_hi, a_lo = _split_bf16(a)
    b_hi, b_lo = _split_bf16(b)
    return _dot(a_hi, b_hi) + (_dot(a_hi, b_lo) + _dot(a_lo, b_hi))


def _gelu(x):
    return 0.5 * x * (1.0 + jnp.tanh(0.7978845608028654 * (x + 0.044715 * (x * x * x))))


def _sigmoid(x):
    return 0.5 * jnp.tanh(0.5 * x) + 0.5


def _silu(x):
    return x * _sigmoid(x)


def _fold8(x, op):
    groups = [x[i:i + 8, :] for i in range(0, x.shape[0], 8)]
    while len(groups) > 1:
        groups = [op(groups[i], groups[i + 1]) if i + 1 < len(groups) else groups[i]
                  for i in range(0, len(groups), 2)]
    return groups[0]


def _position_lanes(pos, lane):
    hi = (pos >> POS_BITS).astype(F32)
    lo = (pos & ((1 << POS_BITS) - 1)).astype(F32)
    off = lane - HEAD_DIM
    return jnp.where((off >= 0) & (off < SLOPE_PIECES), hi,
                     jnp.where((off >= SLOPE_PIECES) & (off < 2 * SLOPE_PIECES), lo, 0.0))


def _mod_norm(x, g, sc, sh):
    y = x * lax.rsqrt(jnp.mean(x * x, axis=-1, keepdims=True) + EPS)
    return (y * g) * (1.0 + sc) + sh


def _mod_kernel(c_ref, w_ref, b_ref, o_ref):
    o_ref[...] = _dot3(c_ref[...], w_ref[...]) + b_ref[...]


def _modulation(c, w_mod, b_mod):
    depth, d, n = w_mod.shape
    bsz = c.shape[0]
    rows = 8
    c_pad = jnp.zeros((rows, d), F32).at[:bsz].set(c)
    out = pl.pallas_call(
        _mod_kernel,
        grid=(depth, n // d),
        in_specs=[pl.BlockSpec((rows, d), lambda l, j: (0, 0)),
                  pl.BlockSpec((None, d, d), lambda l, j: (l, 0, j)),
                  pl.BlockSpec((None, 1, d), lambda l, j: (l, 0, j))],
        out_specs=pl.BlockSpec((None, rows, d), lambda l, j: (l, 0, j)),
        out_shape=jax.ShapeDtypeStruct((depth, rows, n), F32),
        compiler_params=_params(2),
        name="modulation",
    )(c_pad, w_mod, b_mod.reshape(depth, 1, n))
    return out[:, :bsz].reshape(depth, bsz, n // d, 1, d)


def _premix_kernel(x_ref, g_ref, sc_ref, sh_ref, w_ref, qaug_ref,
                   wc4_ref, bc4_ref, wa_ref, ba_ref, wx_ref, bx_ref, lam_ref,
                   uconv_ref, ylru_ref, q_ref, kvc_ref, ks_ref, kw_ref, vst_ref, vwt_ref, gt_ref,
                   kvc_tmp_ref, xext_ref, hcar_ref):
    i = pl.program_id(1)
    tm = x_ref.shape[0]

    @pl.when(i == 0)
    def _():
        xext_ref[0:LRU_HALO, :] = jnp.zeros((LRU_HALO, LRU_DIM), F32)
        hcar_ref[...] = jnp.zeros(hcar_ref.shape, F32)

    hb = _mod_norm(x_ref[...], g_ref[...], sc_ref[...], sh_ref[...]).astype(BF16)

    def proj(a, b):
        return _dot(hb, w_ref[:, a:b])

    u_conv = proj(COL_CONV, COL_Q)
    uconv_ref[...] = u_conv[:, 0:CONV_DIM] * _sigmoid(u_conv[:, CONV_DIM:2 * CONV_DIM])
    u_lru = proj(COL_LRU, COL_END)
    for r0 in range(0, tm, BRANCH_TILE):
        ylru_ref[r0:r0 + BRANCH_TILE, :] = _lru_rows(
            u_lru[r0:r0 + BRANCH_TILE, 0:LRU_DIM], _gelu(u_lru[r0:r0 + BRANCH_TILE, LRU_DIM:2 * LRU_DIM]),
            wc4_ref, bc4_ref, wa_ref, ba_ref, wx_ref, bx_ref, lam_ref, xext_ref, hcar_ref)
    uq = proj(COL_Q, COL_KVC)
    low_half = lax.broadcasted_iota(jnp.int32, (tm, LANES), 1) < HEAD_DIM
    for h in range(N_HEADS):
        pair = uq[:, (h // 2) * LANES:(h // 2 + 1) * LANES]
        if h % 2:
            pair = pltpu.roll(pair, HEAD_DIM, axis=1)
        q_ref[:, h * LANES:(h + 1) * LANES] = jnp.where(low_half, pair, qaug_ref[h:h + 1, :]).astype(BF16)
    ukv = proj(COL_KVC, COL_K)
    for half in range(2):
        kvc_tmp_ref[half] = ukv[:, half * LANES:(half + 1) * LANES]
    for t in range(CMP_STRIDE):
        for half in range(2):
            col = t * KVC_WIDTH + half * LANES
            kvc_ref[:, col:col + LANES] = (
                kvc_tmp_ref[half, pl.ds(t, tm // CMP_STRIDE, stride=CMP_STRIDE), :].astype(BF16))
    pos = i * tm + lax.broadcasted_iota(jnp.int32, (tm, LANES), 0)
    lane = lax.broadcasted_iota(jnp.int32, (tm, LANES), 1)
    kaug = _position_lanes(pos, lane)
    uk = proj(COL_K, COL_V)
    for g in range(N_KV):
        ks_ref[g] = (uk[:, g * LANES:(g + 1) * LANES] + kaug).astype(BF16)
        kw_ref[g] = (uk[:, (N_KV + g) * LANES:(N_KV + g + 1) * LANES] + kaug).astype(BF16)
    uv = proj(COL_V, COL_LRU)
    vst_ref[...] = uv[:, 0:LANES].T.astype(BF16)
    vwt = uv[:, LANES:2 * LANES].T.astype(BF16)
    for t in range(tm // WIN_TILE):
        vwt_ref[t] = vwt[:, t * WIN_TILE:(t + 1) * WIN_TILE]
    gt = _sigmoid(uv[:, 2 * LANES:3 * LANES]).T
    gt_ref[...] = gt[0:gt_ref.shape[0], :]


def _premix(x, mod, layer, qaug, p):
    bsz, s, d = x.shape
    tm = TOKEN_TILE
    c = LRU_DIM
    assert tm == KEY_TILE and s % tm == 0 and tm % BRANCH_TILE == 0
    n_t = s // tm
    row = lambda k: pl.BlockSpec((None, None, None, 1, d), lambda b, i: (layer, b, k, 0, 0))
    full = lambda shape: pl.BlockSpec((None,) + shape, lambda b, i: (layer,) + (0,) * len(shape))
    out_shapes = (
        jax.ShapeDtypeStruct((bsz, s, CONV_DIM), F32),
        jax.ShapeDtypeStruct((bsz, s, LRU_DIM), BF16),
        jax.ShapeDtypeStruct((bsz, s, N_HEADS * LANES), BF16),
        jax.ShapeDtypeStruct((bsz, s // CMP_STRIDE, CMP_STRIDE * KVC_WIDTH), BF16),
        jax.ShapeDtypeStruct((bsz, N_KV, s, LANES), BF16),
        jax.ShapeDtypeStruct((bsz, N_KV, s, LANES), BF16),
        jax.ShapeDtypeStruct((bsz, n_t, LANES, KEY_TILE), BF16),
        jax.ShapeDtypeStruct((bsz, s // WIN_TILE, LANES, WIN_TILE), BF16),
        jax.ShapeDtypeStruct((bsz, N_KV * GATE_ROWS, s), F32),
    )
    out_specs = (
        pl.BlockSpec((None, tm, CONV_DIM), lambda b, i: (b, i, 0)),
        pl.BlockSpec((None, tm, LRU_DIM), lambda b, i: (b, i, 0)),
        pl.BlockSpec((None, tm, N_HEADS * LANES), lambda b, i: (b, i, 0)),
        pl.BlockSpec((None, tm // CMP_STRIDE, CMP_STRIDE * KVC_WIDTH), lambda b, i: (b, i, 0)),
        pl.BlockSpec((None, N_KV, tm, LANES), lambda b, i: (b, 0, i, 0)),
        pl.BlockSpec((None, N_KV, tm, LANES), lambda b, i: (b, 0, i, 0)),
        pl.BlockSpec((None, None, LANES, KEY_TILE), lambda b, i: (b, i, 0, 0)),
        pl.BlockSpec((None, tm // WIN_TILE, LANES, WIN_TILE), lambda b, i: (b, i, 0, 0)),
        pl.BlockSpec((None, N_KV * GATE_ROWS, tm), lambda b, i: (b, 0, i)),
    )
    return pl.pallas_call(
        _premix_kernel,
        grid=(bsz, n_t),
        in_specs=[pl.BlockSpec((None, tm, d), lambda b, i: (b, i, 0)),
                  full((1, d)), row(1), row(0), full((d, COL_END)),
                  pl.BlockSpec((N_HEADS, LANES), lambda b, i: (0, 0)),
                  full((LRU_CONV_WIDTH, c)), full((1, c)),
                  full((c, c)), full((1, c)), full((c, c)), full((1, c)), full((1, c))],
        out_specs=out_specs,
        out_shape=out_shapes,
        scratch_shapes=[pltpu.VMEM((2, tm, LANES), F32),
                        pltpu.VMEM((BRANCH_TILE + LRU_HALO, c), F32),
                        pltpu.VMEM((8, c), F32)],
        compiler_params=_params(2),
        name="premix",
    )(x, p["norm_mix"], mod, mod, p["w_pack"], qaug,
      p["lru_w_conv"], p["lru_b_conv"], p["lru_wa_bd"], p["lru_b_a"], p["lru_wx_bd"], p["lru_b_x"],
      p["lru_lam"])


def _conv_rows(v, wdw_ref, bdw_ref, lng_ref, lnb_ref, vext_ref, vsh_ref):
    ts, c = v.shape
    vext_ref[CONV_HALO:CONV_HALO + ts, :] = v
    span = ts + CONV_HALO - 8
    for phase in range(1, 8):
        vsh_ref[phase - 1] = vext_ref[pl.ds(phase, span), :]
    acc = jnp.broadcast_to(bdw_ref[...], (ts, c))
    first = CONV_HALO - (CONV_WIDTH - 1)
    for j in range(CONV_WIDTH):
        base, phase = (first + j) // 8 * 8, (first + j) % 8
        rows = vext_ref[base:base + ts, :] if phase == 0 else vsh_ref[phase - 1, base:base + ts, :]
        acc = acc + wdw_ref[j:j + 1, :] * rows
    vext_ref[0:CONV_HALO, :] = v[ts - CONV_HALO:ts, :]
    mu = jnp.mean(acc, axis=-1, keepdims=True)
    cen = acc - mu
    var = jnp.mean(cen * cen, axis=-1, keepdims=True)
    ln = (cen * lax.rsqrt(var + EPS)) * lng_ref[...] + lnb_ref[...]
    return _silu(ln).astype(BF16)


def _lru_rows(ux, gate, wc4_ref, bc4_ref, wa_ref, ba_ref, wx_ref, bx_ref, lam_ref, xext_ref, hcar_ref):
    ts, c = ux.shape
    xext_ref[LRU_HALO:LRU_HALO + ts, :] = ux
    xr = jnp.broadcast_to(bc4_ref[...], (ts, c))
    first = LRU_HALO - (LRU_CONV_WIDTH - 1)
    for j in range(LRU_CONV_WIDTH):
        xr = xr + wc4_ref[j:j + 1, :] * xext_ref[pl.ds(first + j, ts), :]
    xext_ref[0:LRU_HALO, :] = ux[ts - LRU_HALO:ts, :]
    xb = xr.astype(BF16)
    r = _sigmoid(_dot(xb, wa_ref[...]) + ba_ref[...])
    gate_i = _sigmoid(_dot(xb, wx_ref[...]) + bx_ref[...])
    z = -lam_ref[...]
    softplus = jnp.maximum(z, 0.0) + jnp.log(1.0 + jnp.exp(-jnp.abs(z)))
    log_a = (-LRU_C * r) * softplus
    a = jnp.exp(log_a)
    b = jnp.sqrt(1.0 - a * a) * (gate_i * xr)
    rows = lax.broadcasted_iota(jnp.int32, (ts, c), 0)
    shift = 1
    while shift < ts:
        a_prev = pltpu.roll(a, shift, axis=0)
        b_prev = pltpu.roll(b, shift, axis=0)
        live = rows >= shift
        b = jnp.where(live, a * b_prev + b, b)
        a = jnp.where(live, a * a_prev, a)
        shift *= 2
    h = a * hcar_ref[0:1, :] + b
    hcar_ref[...] = jnp.broadcast_to(h[ts - 1:ts, :], hcar_ref.shape)
    return (h * gate).astype(BF16)


def _compress_kernel(x_ref, wlo_ref, whi_ref, pek_ref, pev_ref, w1k_ref, w1v_ref, w2_ref,
                     kc_ref, vct_ref):
    nc = x_ref.shape[0]
    x = x_ref[...]
    h_lo = _dot(x, wlo_ref[...])
    h_hi = _dot(x, whi_ref[...])
    h_next = pltpu.roll(h_hi, nc - 1, axis=0)
    bk = _dot3(pek_ref[...], w1k_ref[...])[0:1, :]
    bv = _dot3(pev_ref[...], w1v_ref[...])[0:1, :]
    bias = jnp.concatenate([bk, bk, bv, bv], axis=1)
    hid = _gelu(h_lo + h_next + bias).astype(BF16)
    out = _dot(hid, w2_ref[...])
    cend = lax.broadcasted_iota(jnp.int32, (nc, LANES), 0) * CMP_STRIDE + (CMP_LEN - 1)
    lane = lax.broadcasted_iota(jnp.int32, (nc, LANES), 1)
    aug = _position_lanes(cend, lane)
    for g in range(N_KV):
        kc_ref[g] = (out[:, g * LANES:(g + 1) * LANES] + aug).astype(BF16)
    vct = out[:, N_KV * LANES:(N_KV + 1) * LANES].T.astype(BF16)
    for t in range(nc // CMP_CHUNK):
        vct_ref[t] = vct[:, t * CMP_CHUNK:(t + 1) * CMP_CHUNK]


def _compress(kvc, layer, p):
    bsz, nc, width = kvc.shape
    x = kvc
    full = lambda shape: pl.BlockSpec((None,) + shape, lambda b: (layer,) + (0,) * len(shape))
    return pl.pallas_call(
        _compress_kernel,
        grid=(bsz,),
        in_specs=[pl.BlockSpec((None, nc, width), lambda b: (b, 0, 0)),
                  full((width, CMP_SLOTS * CMP_HIDDEN)), full((width, CMP_SLOTS * CMP_HIDDEN)),
                  full((8, CMP_LEN * HEAD_DIM)), full((8, CMP_LEN * HEAD_DIM)),
                  full((CMP_LEN * HEAD_DIM, CMP_HIDDEN)), full((CMP_LEN * HEAD_DIM, CMP_HIDDEN)),
                  full((CMP_SLOTS * CMP_HIDDEN, (N_KV + 1) * LANES))],
        out_specs=(pl.BlockSpec((None, N_KV, nc, LANES), lambda b: (b, 0, 0, 0)),
                   pl.BlockSpec((None, nc // CMP_CHUNK, LANES, CMP_CHUNK), lambda b: (b, 0, 0, 0))),
        out_shape=(jax.ShapeDtypeStruct((bsz, N_KV, nc, LANES), BF16),
                   jax.ShapeDtypeStruct((bsz, nc // CMP_CHUNK, LANES, CMP_CHUNK), BF16)),
        compiler_params=_params(1),
        name="compress",
    )(x, p["cmp_w_lo"], p["cmp_w_hi"], p["cmp_pe_k"], p["cmp_pe_v"], p["nsa_w_ck1"], p["nsa_w_cv1"],
      p["cmp_w2"])


def _attn_kernel(q_ref, kc_ref, vct_ref, ks_ref, vst_ref, kw_ref, vwt_ref, gt_ref,
                 o_ref, s0_ref, s1_ref, w_ref, psum_ref, selneg_ref, m_ref, l_ref, acc_ref, flag_ref, list_ref):
    qb = pl.program_id(2)
    q0 = qb * Q_BLOCK
    n_sel = selneg_ref.shape[0]
    n_rows = GROUP * Q_BLOCK
    qg = jnp.concatenate([q_ref[:, r * LANES:(r + 1) * LANES] for r in range(GROUP)], axis=0)
    qpos = q0 + (lax.broadcasted_iota(jnp.int32, (1, n_rows), 1) & (Q_BLOCK - 1))

    n_chunks = ((q0 + Q_BLOCK - CMP_LEN) // CMP_STRIDE + CMP_CHUNK) // CMP_CHUNK
    cend = lax.broadcasted_iota(jnp.int32, (CMP_CHUNK, n_rows), 0) * CMP_STRIDE + (CMP_LEN - 1)

    def cmp_branch(n_ch):
        def fn():
            s_all = _dot_nt(kc_ref[0:n_ch * CMP_CHUNK, :], qg)
            scores = []
            m_c = None
            for ci in range(n_ch):
                s = s_all[ci * CMP_CHUNK:(ci + 1) * CMP_CHUNK, :]
                if ci >= n_ch - 2:
                    s = jnp.where(cend + ci * CMP_CHUNK * CMP_STRIDE <= qpos, s, NEG)
                scores.append(s)
                mi = jnp.max(s, axis=0, keepdims=True)
                m_c = mi if m_c is None else jnp.maximum(m_c, mi)
            m_c = jnp.where(m_c > 0.5 * NEG, m_c, 0.0)
            probs = [jnp.exp2(s - m_c) for s in scores]
            den = probs[0].sum(axis=0, keepdims=True)
            for p in probs[1:]:
                den = den + jnp.sum(p, axis=0, keepdims=True)
            inv_den = 1.0 / jnp.where(den > 0, den, 1.0)
            o_c = None
            for ci, p in enumerate(probs):
                p = p * inv_den
                part = _dot(vct_ref[ci], p.astype(BF16))
                o_c = part if o_c is None else o_c + part
                p_sum = p[:, 0:Q_BLOCK]
                for r in range(1, GROUP):
                    p_sum = p_sum + p[:, r * Q_BLOCK:(r + 1) * Q_BLOCK]
                for h in range(Q_BLOCK // LANES):
                    psum_ref[h, CMP_PAD + ci * CMP_CHUNK:CMP_PAD + (ci + 1) * CMP_CHUNK, :] = (
                        p_sum[:, h * LANES:(h + 1) * LANES])
            assert CMP_LEN == 2 * CMP_STRIDE and CMP_PER_SEL == 4
            n_blk = n_ch * CMP_CHUNK // CMP_PER_SEL
            halves = []
            for h in range(Q_BLOCK // LANES):
                taps = [psum_ref[h, pl.ds(CMP_PAD - 1 + k, n_blk, stride=CMP_PER_SEL), :]
                        for k in range(CMP_PER_SEL + 1)]
                imp_h = 0.5 * (taps[0] + taps[4]) + (taps[1] + taps[2] + taps[3])
                if n_blk < n_sel:
                    last = CMP_PAD + n_ch * CMP_CHUNK - 1
                    imp_h = jnp.concatenate([imp_h, 0.5 * psum_ref[h, last:last + 1, :],
                                             jnp.zeros((n_sel - n_blk - 1, LANES), F32)], axis=0)
                halves.append(imp_h)
            return o_c, jnp.concatenate(halves, axis=1)
        return fn

    for h in range(Q_BLOCK // LANES):
        psum_ref[h, 0:CMP_PAD, :] = jnp.zeros((CMP_PAD, LANES), F32)
    o_c, imp = lax.switch(n_chunks - 1, [cmp_branch(n) for n in range(1, kc_ref.shape[0] // CMP_CHUNK + 1)])

    blk = lax.broadcasted_iota(jnp.int32, (n_sel, LANES), 0)
    blk_f = blk.astype(F32)
    works = []
    for h in range(Q_BLOCK // LANES):
        qp = q0 + h * LANES + lax.broadcasted_iota(jnp.int32, (n_sel, LANES), 1)
        qid = qp // SEL_BLOCK
        forced = (blk == 0) | (blk == qid) | (blk == qid - 1)
        valid = blk * SEL_BLOCK <= qp
        works.append(jnp.where(forced, -jnp.inf, jnp.where(valid, imp[:, h * LANES:(h + 1) * LANES], -1.0)))
    for _ in range(min(SEL_TOPK, n_sel) - 3):
        for h, work_h in enumerate(works):
            best = jnp.max(work_h, axis=0, keepdims=True)
            first = jnp.min(jnp.where(work_h == best, blk_f, float(n_sel)), axis=0, keepdims=True)
            works[h] = jnp.where(blk_f == first, -jnp.inf, work_h)
    work = jnp.concatenate(works, axis=1)

    n_tiles = q0 // KEY_TILE + 1
    krow = lax.broadcasted_iota(jnp.int32, (SEL_BLOCK, n_rows), 0)

    def scores_into(buf_ref, kt):
        k0 = pl.multiple_of(kt * KEY_TILE, KEY_TILE)
        buf_ref[...] = _dot_nt(ks_ref[pl.ds(k0, KEY_TILE), :], qg)

    k_last = pl.multiple_of((n_tiles - 1) * KEY_TILE, KEY_TILE)
    s_both = _dot_nt(jnp.concatenate([ks_ref[pl.ds(k_last, KEY_TILE), :], ks_ref[0:LANES, :]], axis=0), qg)
    s0_ref[...] = s_both[0:KEY_TILE, :]
    hrow = lax.broadcasted_iota(jnp.int32, (LANES, n_rows), 0)
    s_first = jnp.where((hrow < SEL_BLOCK) & (hrow <= qpos), s_both[KEY_TILE:KEY_TILE + LANES, :], NEG)
    m_first = jnp.max(s_first, axis=0, keepdims=True)
    p_first = jnp.exp2(s_first - m_first)
    m_ref[...] = m_first
    l_ref[...] = jnp.sum(p_first, axis=0, keepdims=True)
    acc_ref[...] = _dot(vst_ref[0, :, 0:LANES], p_first.astype(BF16))

    n_win = (WINDOW + Q_BLOCK) // WIN_TILE
    t0 = jnp.maximum(q0 - WINDOW, 0) // WIN_TILE
    kstart = pl.multiple_of(t0 * WIN_TILE, WIN_TILE)
    w_ref[...] = _dot_nt(kw_ref[pl.ds(kstart, n_win * WIN_TILE), :], qg)
    wrow = lax.broadcasted_iota(jnp.int32, (WIN_TILE, n_rows), 0)
    qrel = qpos - kstart

    top = None
    for t in range(n_win):
        newest = qrel - t * WIN_TILE
        visible = wrow <= newest
        if t * WIN_TILE < Q_BLOCK:
            visible = visible & (wrow > newest - WINDOW)
        st = jnp.where(visible, w_ref[t * WIN_TILE:(t + 1) * WIN_TILE, :], NEG)
        w_ref[t * WIN_TILE:(t + 1) * WIN_TILE, :] = st
        mt = _fold8(st, jnp.maximum)
        top = mt if top is None else jnp.maximum(top, mt)
    m_w = jnp.max(top, axis=0, keepdims=True)
    total = None
    o_w = None
    for t in range(n_win):
        pt = jnp.exp2(w_ref[t * WIN_TILE:(t + 1) * WIN_TILE, :] - m_w)
        lt = _fold8(pt, jnp.add)
        total = lt if total is None else total + lt
        part = _dot(vwt_ref[t0 + t], pt.astype(BF16))
        o_w = part if o_w is None else o_w + part
    o_w = o_w / jnp.sum(total, axis=0, keepdims=True)

    taken = (work == -jnp.inf) & (lax.broadcasted_iota(jnp.int32, (n_sel, Q_BLOCK), 0) > 0)
    chosen = jnp.where(taken, 1.0, 0.0)
    selneg = jnp.where(taken, 0.0, NEG)
    selneg_ref[...] = jnp.concatenate([selneg] * GROUP, axis=1)
    blocks_per_tile = KEY_TILE // SEL_BLOCK
    any_q = jnp.max(chosen, axis=1, keepdims=True)
    for t in range(n_sel // blocks_per_tile):
        tile_any = jnp.max(any_q[t * blocks_per_tile:(t + 1) * blocks_per_tile, :])
        flag_ref[t] = (tile_any > 0).astype(jnp.int32)

    def compact(kt, n):
        @pl.when(flag_ref[kt] > 0)
        def _():
            list_ref[n] = kt
        return n + flag_ref[kt]

    n_act = lax.fori_loop(0, n_tiles - 1, compact, 0)
    list_ref[n_act] = 0

    def sel_tile(buf_ref, kt, causal, n_blocks=KEY_TILE // SEL_BLOCK):
        sel_rows = selneg_ref[pl.ds(pl.multiple_of(kt * blocks_per_tile, blocks_per_tile),
                                    blocks_per_tile), :]

        def block_scores(j):
            return buf_ref[j * SEL_BLOCK:(j + 1) * SEL_BLOCK, :]

        m_old = m_ref[...]
        top = None
        for j in range(n_blocks):
            sj = block_scores(j)
            if causal:
                sj = jnp.where(krow <= qpos - (kt * KEY_TILE + j * SEL_BLOCK), sj, NEG)
                buf_ref[j * SEL_BLOCK:(j + 1) * SEL_BLOCK, :] = sj
            mj = _fold8(sj, jnp.maximum) + sel_rows[j:j + 1, :]
            top = mj if top is None else jnp.maximum(top, mj)
        m_new = jnp.maximum(m_old, jnp.max(top, axis=0, keepdims=True))
        alpha = jnp.exp2(m_old - m_new)
        total = None
        probs = []
        for j in range(n_blocks):
            pj = jnp.exp2(block_scores(j) - (m_new - sel_rows[j:j + 1, :]))
            lj = _fold8(pj, jnp.add)
            total = lj if total is None else total + lj
            probs.append(pj.astype(BF16))
        m_ref[...] = m_new
        l_ref[...] = alpha * l_ref[...] + jnp.sum(total, axis=0, keepdims=True)
        acc_ref[...] = alpha * acc_ref[...] + _dot(vst_ref[kt, :, 0:n_blocks * SEL_BLOCK],
                                                   jnp.concatenate(probs, axis=0))

    half_tile = (q0 % KEY_TILE) + Q_BLOCK <= KEY_TILE // 2

    @pl.when(half_tile)
    def _():
        scores_into(s1_ref, list_ref[0])
        sel_tile(s0_ref, n_tiles - 1, True, n_blocks=KEY_TILE // SEL_BLOCK // 2)

    @pl.when(jnp.logical_not(half_tile))
    def _():
        scores_into(s1_ref, list_ref[0])
        sel_tile(s0_ref, n_tiles - 1, True)

    def pipelined(i, carry):
        scores_into(s0_ref, list_ref[2 * i + 1])
        sel_tile(s1_ref, list_ref[2 * i], False)

        @pl.when(2 * i + 1 < n_act)
        def _():
            scores_into(s1_ref, list_ref[2 * i + 2])
            sel_tile(s0_ref, list_ref[2 * i + 1], False)
        return carry

    lax.fori_loop(0, (n_act + 1) // 2, pipelined, 0)
    o_s = acc_ref[...] / l_ref[...]

    def gate(branch):
        return jnp.concatenate([gt_ref[branch * GROUP + r:branch * GROUP + r + 1, :] for r in range(GROUP)],
                               axis=1)
    o = gate(0) * o_c + gate(1) * o_s + gate(2) * o_w
    o_ref[...] = jnp.concatenate([o[:, r * Q_BLOCK:(r + 1) * Q_BLOCK].T for r in range(GROUP)],
                                 axis=1).astype(BF16)


def _attention(q, kc, vct, ks, vst, kw, vwt, gt):
    bsz, s, _ = q.shape
    nc = kc.shape[2]
    n_sel = s // SEL_BLOCK
    n_qb = s // Q_BLOCK
    n_kt = s // KEY_TILE
    n_wt = s // WIN_TILE
    n_ch = nc // CMP_CHUNK
    n_rows = GROUP * Q_BLOCK
    assert s % KEY_TILE == 0 and s >= (WINDOW + Q_BLOCK) and n_sel % 8 == 0 and nc % CMP_CHUNK == 0
    assert KEY_TILE % Q_BLOCK == 0 and Q_BLOCK % WIN_TILE == 0
    return pl.pallas_call(
        _attn_kernel,
        grid=(bsz, N_KV, n_qb),
        in_specs=[pl.BlockSpec((None, Q_BLOCK, GROUP * LANES), lambda b, g, i: (b, i, g)),
                  pl.BlockSpec((None, None, nc, LANES), lambda b, g, i: (b, g, 0, 0)),
                  pl.BlockSpec((None, n_ch, HEAD_DIM, CMP_CHUNK), lambda b, g, i: (b, 0, g, 0)),
                  pl.BlockSpec((None, None, s, LANES), lambda b, g, i: (b, g, 0, 0)),
                  pl.BlockSpec((None, n_kt, HEAD_DIM, KEY_TILE), lambda b, g, i: (b, 0, g, 0)),
                  pl.BlockSpec((None, None, s, LANES), lambda b, g, i: (b, g, 0, 0)),
                  pl.BlockSpec((None, n_wt, HEAD_DIM, WIN_TILE), lambda b, g, i: (b, 0, g, 0)),
                  pl.BlockSpec((None, GATE_ROWS, Q_BLOCK), lambda b, g, i: (b, g, i))],
        out_specs=pl.BlockSpec((None, Q_BLOCK, GROUP * HEAD_DIM), lambda b, g, i: (b, i, g)),
        out_shape=jax.ShapeDtypeStruct((bsz, s, N_HEADS * HEAD_DIM), BF16),
        scratch_shapes=[pltpu.VMEM((KEY_TILE, n_rows), F32),
                        pltpu.VMEM((KEY_TILE, n_rows), F32),
                        pltpu.VMEM((WINDOW + Q_BLOCK, n_rows), F32),
                        pltpu.VMEM((Q_BLOCK // LANES, CMP_PAD + nc, LANES), F32),
                        pltpu.VMEM((n_sel, n_rows), F32),
                        pltpu.VMEM((1, n_rows), F32),
                        pltpu.VMEM((1, n_rows), F32),
                        pltpu.VMEM((HEAD_DIM, n_rows), F32),
                        pltpu.SMEM((n_kt,), jnp.int32),
                        pltpu.SMEM((n_kt + 1,), jnp.int32)],
        compiler_params=_params(3),
        name="nsa_attention",
    )(q, kc, vct, ks, vst, kw, vwt, gt)


def _merge_kernel(x_ref, g_ref, sc_ref, sh_ref, gate_ref, uconv_ref, ylru_ref, onsa_ref,
                  wdw_ref, bdw_ref, lng_ref, lnb_ref, wm_ref, bm_ref, wc_ref, wn_ref, wl_ref, wo_ref,
                  o_ref, vext_ref, vsh_ref):
    tm, d = x_ref.shape

    @pl.when(pl.program_id(1) == 0)
    def _():
        vext_ref[0:CONV_HALO, :] = jnp.zeros((CONV_HALO, CONV_DIM), F32)

    x = x_ref[...]
    hb = _mod_norm(x, g_ref[...], sc_ref[...], sh_ref[...]).astype(BF16)

    def merge_gate(k):
        return _sigmoid(_dot(hb, wm_ref[:, k * d:(k + 1) * d]) + bm_ref[:, k * d:(k + 1) * d])

    early = merge_gate(1) * _dot(onsa_ref[...], wn_ref[...]) + merge_gate(2) * _dot(ylru_ref[...], wl_ref[...])
    vconv = jnp.concatenate(
        [_conv_rows(uconv_ref[r0:r0 + BRANCH_TILE, :], wdw_ref, bdw_ref, lng_ref, lnb_ref, vext_ref, vsh_ref)
         for r0 in range(0, tm, BRANCH_TILE)], axis=0)
    merged = early + merge_gate(0) * _dot(vconv, wc_ref[...])
    o_ref[...] = x + gate_ref[...] * _dot(merged.astype(BF16), wo_ref[...])


def _merge(x, mod, layer, uconv, ylru, onsa, p):
    bsz, s, d = x.shape
    tm = TOKEN_TILE
    ts = BRANCH_TILE
    c = CONV_DIM
    assert tm % ts == 0
    row = lambda k: pl.BlockSpec((None, None, None, 1, d), lambda b, i: (layer, b, k, 0, 0))
    full = lambda shape: pl.BlockSpec((None,) + shape, lambda b, i: (layer,) + (0,) * len(shape))
    tok = lambda w: pl.BlockSpec((None, tm, w), lambda b, i: (b, i, 0))
    return pl.pallas_call(
        _merge_kernel,
        grid=(bsz, s // tm),
        in_specs=[tok(d), full((1, d)), row(1), row(0), row(2), tok(c), tok(c), tok(c),
                  full((CONV_WIDTH, c)), full((1, c)), full((1, c)), full((1, c)),
                  full((d, N_BRANCH * d)), full((1, N_BRANCH * d)),
                  full((c, d)), full((c, d)), full((c, d)), full((d, d))],
        out_specs=tok(d),
        out_shape=jax.ShapeDtypeStruct((bsz, s, d), F32),
        scratch_shapes=[pltpu.VMEM((ts + CONV_HALO, c), F32),
                        pltpu.VMEM((7, ts + CONV_HALO - 8, c), F32)],
        compiler_params=_params(2),
        name="merge",
    )(x, p["norm_mix"], mod, mod, mod, uconv, ylru, onsa,
      p["conv_w_dw"], p["conv_b_dw"], p["conv_ln_g"], p["conv_ln_b"],
      p["w_merge"], p["b_merge"], p["conv_w_out"], p["nsa_w_out"], p["lru_w_out"], p["w_out"])


def _ffn_kernel(x_ref, g_ref, sc_ref, sh_ref, gate_ref, wa_ref, wb_ref, wo_ref, fin_ref, o_ref,
                *, chunk, final):
    x = x_ref[...]
    hb = _mod_norm(x, g_ref[...], sc_ref[...], sh_ref[...]).astype(BF16)
    ffn = wa_ref.shape[1]
    acc = None
    for c0 in range(0, ffn, chunk):
        a = _dot(hb, wa_ref[:, c0:c0 + chunk])
        b = _dot(hb, wb_ref[:, c0:c0 + chunk])
        part = _dot((_silu(a) * b).astype(BF16), wo_ref[c0:c0 + chunk, :])
        acc = part if acc is None else acc + part
    y = x + gate_ref[...] * acc
    if final:
        y = (y * lax.rsqrt(jnp.mean(y * y, axis=-1, keepdims=True) + EPS)) * fin_ref[...]
    o_ref[...] = y


def _ffn(x, mod, layer, norm_g, p, final_norm, final):
    bsz, s, d = x.shape
    tm = TOKEN_TILE
    ffn = p["w_ffn_out"].shape[1]
    chunk = FFN_CHUNK
    assert ffn % chunk == 0
    row = lambda k: pl.BlockSpec((None, None, None, 1, d), lambda b, i: (layer, b, k, 0, 0))
    full = lambda shape: pl.BlockSpec((None,) + shape, lambda b, i: (layer,) + (0,) * len(shape))
    half = lambda k: pl.BlockSpec((None, d, ffn), lambda b, i: (layer, 0, k))
    tok = pl.BlockSpec((None, tm, d), lambda b, i: (b, i, 0))
    return pl.pallas_call(
        functools.partial(_ffn_kernel, chunk=chunk, final=final),
        grid=(bsz, s // tm),
        in_specs=[tok, full((1, d)), row(4), row(3), row(5),
                  half(0), half(1), full((ffn, d)), pl.BlockSpec((1, d), lambda b, i: (0, 0))],
        out_specs=tok,
        out_shape=jax.ShapeDtypeStruct((bsz, s, d), F32),
        compiler_params=_params(2),
        name="ffn",
    )(x, norm_g, mod, mod, mod, p["w_ffn_in"], p["w_ffn_in"], p["w_ffn_out"], final_norm)


def _pack_input_projection(w_in):
    lead = w_in.shape[:-1]
    q0 = 2 * CONV_DIM
    kv0 = q0 + N_HEADS * HEAD_DIM
    gate0 = kv0 + 6 * N_KV * HEAD_DIM
    lru0 = gate0 + 3 * N_HEADS
    kv_w = N_KV * HEAD_DIM
    keep = ((0, 0),) * len(lead)

    gap = jnp.zeros(lead + (LANES - HEAD_DIM,), BF16)

    def head_slots(w, n):
        pieces = []
        for h in range(n):
            pieces += [w[..., h * HEAD_DIM:(h + 1) * HEAD_DIM].astype(BF16), gap]
        return pieces

    kv = lambda i: w_in[..., kv0 + i * kv_w:kv0 + (i + 1) * kv_w]
    gates = w_in[..., gate0:gate0 + N_KV * GROUP * 3].reshape(lead + (N_KV, GROUP, 3))
    gates = jnp.swapaxes(gates, -1, -2).reshape(lead + (N_KV, 3 * GROUP))
    gates = jnp.pad(gates, keep + ((0, 0), (0, GATE_ROWS - 3 * GROUP))).reshape(lead + (N_KV * GATE_ROWS,))
    gates = jnp.pad(gates, keep + ((0, LANES - N_KV * GATE_ROWS),))
    cast = lambda w: w.astype(BF16)
    return jnp.concatenate(
        [cast(w_in[..., 0:q0]), cast(w_in[..., q0:kv0] * (HEAD_DIM ** -0.5 * LOG2E)), cast(kv(0)), cast(kv(1))]
        + head_slots(kv(2), N_KV) + head_slots(kv(4), N_KV)
        + [cast(kv(3)), cast(kv(5)), cast(gates), cast(w_in[..., lru0:lru0 + 2 * LRU_DIM])], axis=-1)


def _bf16_pieces(x):
    pieces = []
    for _ in range(SLOPE_PIECES):
        piece = float(np.asarray(x, np.float32).astype(BF16).astype(np.float64))
        pieces.append(piece)
        x = x - piece
    return pieces


def _q_slope_rows():
    rows = np.zeros((N_HEADS, LANES), np.float32)
    for h in range(N_HEADS):
        slope = 2.0 ** (-8.0 * (h + 1) / N_HEADS) * LOG2E
        rows[h, HEAD_DIM:HEAD_DIM + SLOPE_PIECES] = _bf16_pieces(slope * float(1 << POS_BITS))
        rows[h, HEAD_DIM + SLOPE_PIECES:HEAD_DIM + 2 * SLOPE_PIECES] = _bf16_pieces(slope)
    return rows


def _block_diag(w):
    depth, heads, n, _ = w.shape
    eye = jnp.asarray(np.eye(heads, dtype=np.float32))
    return (w[:, :, :, None, :] * eye[None, :, None, :, None]).reshape(depth, heads * n, heads * n)


def _prepare_params(a):
    depth = a["w_in"].shape[0]
    w_pack = _pack_input_projection(a["w_in"])
    assert w_pack.shape[-1] == COL_END
    row = lambda v: v.reshape(depth, 1, -1)

    def chunk_weights(half):
        rows = slice(half * CMP_STRIDE, (half + 1) * CMP_STRIDE)
        wk = a["nsa_w_ck1"].reshape(depth, CMP_LEN, HEAD_DIM, CMP_HIDDEN)[:, rows].astype(BF16)
        wv = a["nsa_w_cv1"].reshape(depth, CMP_LEN, HEAD_DIM, CMP_HIDDEN)[:, rows].astype(BF16)
        slots = [jnp.pad(w, ((0, 0), (0, 0), (0, 0), (k * CMP_HIDDEN, (CMP_SLOTS - 1 - k) * CMP_HIDDEN)))
                 for k, w in enumerate((wk, wk, wv, wv))]
        return jnp.concatenate(slots, axis=2).reshape(depth, CMP_STRIDE * KVC_WIDTH, CMP_SLOTS * CMP_HIDDEN)

    eye2 = jnp.asarray(np.eye(N_KV, dtype=np.float32))
    ck2 = jnp.pad(a["nsa_w_ck2"], ((0, 0), (0, 0), (0, LANES - HEAD_DIM)))
    block2 = lambda w: (w[:, None, :, None, :] * eye2[None, :, None, :, None]).reshape(
        depth, N_KV * w.shape[1], N_KV * w.shape[2])
    w2 = jnp.concatenate([jnp.pad(block2(ck2), ((0, 0), (0, 0), (0, N_KV * HEAD_DIM))),
                          jnp.pad(block2(a["nsa_w_cv2"]), ((0, 0), (0, 0), (N_KV * LANES, 0)))], axis=1)
    pe_rows = lambda pe: jnp.pad(pe.reshape(depth, 1, -1), ((0, 0), (0, 7), (0, 0)))
    return {
        "w_pack": w_pack.astype(BF16),
        "norm_mix": row(a["norm_mix"]), "norm_ffn": row(a["norm_ffn"]),
        "conv_w_dw": a["conv_w_dw"], "conv_b_dw": row(a["conv_b_dw"]),
        "conv_ln_g": row(a["conv_ln_g"]), "conv_ln_b": row(a["conv_ln_b"]),
        "lru_w_conv": a["lru_w_conv"], "lru_b_conv": row(a["lru_b_conv"]),
        "lru_wa_bd": _block_diag(a["lru_w_a"]).astype(BF16), "lru_b_a": row(a["lru_b_a"]),
        "lru_wx_bd": _block_diag(a["lru_w_x"]).astype(BF16), "lru_b_x": row(a["lru_b_x"]),
        "lru_lam": row(a["lru_lam"]),
        "cmp_w_lo": chunk_weights(0), "cmp_w_hi": chunk_weights(1),
        "cmp_pe_k": pe_rows(a["nsa_pe_k"]), "cmp_pe_v": pe_rows(a["nsa_pe_v"]),
        "nsa_w_ck1": a["nsa_w_ck1"], "nsa_w_cv1": a["nsa_w_cv1"],
        "cmp_w2": w2.astype(BF16),
        "w_merge": a["w_merge"].astype(BF16), "b_merge": row(a["b_merge"]),
        "conv_w_out": a["conv_w_out"].astype(BF16), "nsa_w_out": a["nsa_w_out"].astype(BF16),
        "lru_w_out": a["lru_w_out"].astype(BF16), "w_out": a["w_out"].astype(BF16),
        "w_ffn_in": a["w_ffn_in"].astype(BF16), "w_ffn_out": a["w_ffn_out"].astype(BF16),
    }


def kernel(x, c, w_mod, b_mod, norm_mix, norm_ffn, w_in, conv_w_dw, conv_b_dw, conv_ln_g, conv_ln_b,
           conv_w_out, nsa_pe_k, nsa_w_ck1, nsa_w_ck2, nsa_pe_v, nsa_w_cv1, nsa_w_cv2, nsa_w_out,
           lru_w_conv, lru_b_conv, lru_w_a, lru_b_a, lru_w_x, lru_b_x, lru_lam, lru_w_out,
           w_merge, b_merge, w_out, w_ffn_in, w_ffn_out, final_norm):
    a = dict(norm_mix=norm_mix, norm_ffn=norm_ffn, w_in=w_in, conv_w_dw=conv_w_dw, conv_b_dw=conv_b_dw, conv_ln_g=conv_ln_g, conv_ln_b=conv_ln_b,
             conv_w_out=conv_w_out, nsa_pe_k=nsa_pe_k, nsa_w_ck1=nsa_w_ck1, nsa_w_ck2=nsa_w_ck2,
             nsa_pe_v=nsa_pe_v, nsa_w_cv1=nsa_w_cv1, nsa_w_cv2=nsa_w_cv2, nsa_w_out=nsa_w_out,
             lru_w_conv=lru_w_conv, lru_b_conv=lru_b_conv, lru_w_a=lru_w_a, lru_b_a=lru_b_a,
             lru_w_x=lru_w_x, lru_b_x=lru_b_x, lru_lam=lru_lam, lru_w_out=lru_w_out,
             w_merge=w_merge, b_merge=b_merge, w_out=w_out, w_ffn_in=w_ffn_in, w_ffn_out=w_ffn_out)
    depth = w_in.shape[0]
    s = x.shape[1]
    mod = _modulation(c, w_mod, b_mod)
    qaug = jnp.asarray(_q_slope_rows())
    fin = final_norm.reshape(1, -1)
    p = _prepare_params(a)
    for l in range(depth):
        uconv, ylru, q, kvc, ks, kw, vst, vwt, gt = _premix(x, mod, l, qaug, p)
        kc, vct = _compress(kvc, l, p)
        onsa = _attention(q, kc, vct, ks, vst, kw, vwt, gt)
        x = _merge(x, mod, l, uconv, ylru, onsa, p)
        x = _ffn(x, mod, l, p["norm_ffn"], p, fin, final=(l == depth - 1))
    return x
```

```python
import functools

import numpy as np
import jax
import jax.numpy as jnp
from jax import lax
from jax.experimental import pallas as pl
from jax.experimental.pallas import tpu as pltpu

F32 = jnp.float32
BF16 = jnp.bfloat16

EPS = 1e-6
CONV_DIM = 512
CONV_WIDTH = 31
N_HEADS = 8
N_KV = 2
GROUP = N_HEADS // N_KV
HEAD_DIM = 64
CMP_LEN = 32
CMP_STRIDE = 16
CMP_HIDDEN = 128
SEL_BLOCK = 64
SEL_TOPK = 16
WINDOW = 512
Q_BLOCK = 256
LRU_DIM = 512
LRU_CONV_WIDTH = 4
LRU_C = 8.0
N_BRANCH = 3

LANES = 128
NEG = -1e30
LOG2E = 1.4426950408889634
SLOPE_PIECES = 3
POS_BITS = 7
KVC_WIDTH = 2 * LANES
CMP_SLOTS = 4
FFN_CHUNK = 256
GATE_ROWS = 16
KEY_TILE = 512
WIN_TILE = 128
CMP_CHUNK = 128
CMP_PER_SEL = SEL_BLOCK // CMP_STRIDE
CMP_PAD = 8
TOKEN_TILE = 512
BRANCH_TILE = 256
CONV_HALO = 32
LRU_HALO = 8
VMEM_LIMIT = 56 * 1024 * 1024

COL_CONV = 0
COL_Q = 1024
COL_KVC = 1536
COL_K = 1792
COL_V = 2304
COL_LRU = 2688
COL_END = 3712


def _params(n_grid):
    return pltpu.CompilerParams(dimension_semantics=("arbitrary",) * n_grid,
                                vmem_limit_bytes=VMEM_LIMIT)


def _dot(a, b):
    return jnp.dot(a, b, preferred_element_type=F32)


def _dot_nt(a, b):
    return lax.dot_general(a, b, (((1,), (1,)), ((), ())), preferred_element_type=F32)


def _split_bf16(a):
    hi = a.astype(BF16)
    lo = (a - hi.astype(F32)).astype(BF16)
    return hi, lo


def _dot3(a, b):
    a_hi, a_lo = _split_bf16(a)
    b_hi, b_lo = _split_bf16(b)
    return _dot(a_hi, b_hi) + (_dot(a_hi, b_lo) + _dot(a_lo, b_hi))


def _gelu(x):
    return 0.5 * x * (1.0 + jnp.tanh(0.7978845608028654 * (x + 0.044715 * (x * x * x))))


def _sigmoid(x):
    return 0.5 * jnp.tanh(0.5 * x) + 0.5


def _silu(x):
    return x * _sigmoid(x)


def _fold8(x, op):
    groups = [x[i:i + 8, :] for i in range(0, x.shape[0], 8)]
    while len(groups) > 1:
        groups = [op(groups[i], groups[i + 1]) if i + 1 < len(groups) else groups[i]
                  for i in range(0, len(groups), 2)]
    return groups[0]


def _position_lanes(pos, lane):
    hi = (pos >> POS_BITS).astype(F32)
    lo = (pos & ((1 << POS_BITS) - 1)).astype(F32)
    off = lane - HEAD_DIM
    return jnp.where((off >= 0) & (off < SLOPE_PIECES), hi,
                     jnp.where((off >= SLOPE_PIECES) & (off < 2 * SLOPE_PIECES), lo, 0.0))


def _mod_norm(x, g, sc, sh):
    y = x * lax.rsqrt(jnp.mean(x * x, axis=-1, keepdims=True) + EPS)
    return (y * g) * (1.0 + sc) + sh


def _mod_kernel(c_ref, w_ref, b_ref, o_ref):
    o_ref[...] = _dot3(c_ref[...], w_ref[...]) + b_ref[...]


def _modulation(c, w_mod, b_mod):
    depth, d, n = w_mod.shape
    bsz = c.shape[0]
    rows = 8
    c_pad = jnp.zeros((rows, d), F32).at[:bsz].set(c)
    out = pl.pallas_call(
        _mod_kernel,
        grid=(depth, n // d),
        in_specs=[pl.BlockSpec((rows, d), lambda l, j: (0, 0)),
                  pl.BlockSpec((None, d, d), lambda l, j: (l, 0, j)),
                  pl.BlockSpec((None, 1, d), lambda l, j: (l, 0, j))],
        out_specs=pl.BlockSpec((None, rows, d), lambda l, j: (l, 0, j)),
        out_shape=jax.ShapeDtypeStruct((depth, rows, n), F32),
        compiler_params=_params(2),
        name="modulation",
    )(c_pad, w_mod, b_mod.reshape(depth, 1, n))
    return out[:, :bsz].reshape(depth, bsz, n // d, 1, d)


def _premix_kernel(x_ref, g_ref, sc_ref, sh_ref, w_ref, qaug_ref,
                   wc4_ref, bc4_ref, wa_ref, ba_ref, wx_ref, bx_ref, lam_ref,
                   uconv_ref, ylru_ref, q_ref, kvc_ref, ks_ref, kw_ref, vst_ref, vwt_ref, gt_ref,
                   kvc_tmp_ref, xext_ref, hcar_ref):
    i = pl.program_id(1)
    tm = x_ref.shape[0]

    @pl.when(i == 0)
    def _():
        xext_ref[0:LRU_HALO, :] = jnp.zeros((LRU_HALO, LRU_DIM), F32)
        hcar_ref[...] = jnp.zeros(hcar_ref.shape, F32)

    hb = _mod_norm(x_ref[...], g_ref[...], sc_ref[...], sh_ref[...]).astype(BF16)

    def proj(a, b):
        return _dot(hb, w_ref[:, a:b])

    u_conv = proj(COL_CONV, COL_Q)
    uconv_ref[...] = u_conv[:, 0:CONV_DIM] * _sigmoid(u_conv[:, CONV_DIM:2 * CONV_DIM])
    u_lru = proj(COL_LRU, COL_END)
    for r0 in range(0, tm, BRANCH_TILE):
        ylru_ref[r0:r0 + BRANCH_TILE, :] = _lru_rows(
            u_lru[r0:r0 + BRANCH_TILE, 0:LRU_DIM], _gelu(u_lru[r0:r0 + BRANCH_TILE, LRU_DIM:2 * LRU_DIM]),
            wc4_ref, bc4_ref, wa_ref, ba_ref, wx_ref, bx_ref, lam_ref, xext_ref, hcar_ref)
    uq = proj(COL_Q, COL_KVC)
    low_half = lax.broadcasted_iota(jnp.int32, (tm, LANES), 1) < HEAD_DIM
    for h in range(N_HEADS):
        pair = uq[:, (h // 2) * LANES:(h // 2 + 1) * LANES]
        if h % 2:
            pair = pltpu.roll(pair, HEAD_DIM, axis=1)
        q_ref[:, h * LANES:(h + 1) * LANES] = jnp.where(low_half, pair, qaug_ref[h:h + 1, :]).astype(BF16)
    ukv = proj(COL_KVC, COL_K)
    for half in range(2):
        kvc_tmp_ref[half] = ukv[:, half * LANES:(half + 1) * LANES]
    for t in range(CMP_STRIDE):
        for half in range(2):
            col = t * KVC_WIDTH + half * LANES
            kvc_ref[:, col:col + LANES] = (
                kvc_tmp_ref[half, pl.ds(t, tm // CMP_STRIDE, stride=CMP_STRIDE), :].astype(BF16))
    pos = i * tm + lax.broadcasted_iota(jnp.int32, (tm, LANES), 0)
    lane = lax.broadcasted_iota(jnp.int32, (tm, LANES), 1)
    kaug = _position_lanes(pos, lane)
    uk = proj(COL_K, COL_V)
    for g in range(N_KV):
        ks_ref[g] = (uk[:, g * LANES:(g + 1) * LANES] + kaug).astype(BF16)
        kw_ref[g] = (uk[:, (N_KV + g) * LANES:(N_KV + g + 1) * LANES] + kaug).astype(BF16)
    uv = proj(COL_V, COL_LRU)
    vst_ref[...] = uv[:, 0:LANES].T.astype(BF16)
    vwt = uv[:, LANES:2 * LANES].T.astype(BF16)
    for t in range(tm // WIN_TILE):
        vwt_ref[t] = vwt[:, t * WIN_TILE:(t + 1) * WIN_TILE]
    gt = _sigmoid(uv[:, 2 * LANES:3 * LANES]).T
    gt_ref[...] = gt[0:gt_ref.shape[0], :]


def _premix(x, mod, layer, qaug, p):
    bsz, s, d = x.shape
    tm = TOKEN_TILE
    c = LRU_DIM
    assert tm == KEY_TILE and s % tm == 0 and tm % BRANCH_TILE == 0
    n_t = s // tm
    row = lambda k: pl.BlockSpec((None, None, None, 1, d), lambda b, i: (layer, b, k, 0, 0))
    full = lambda shape: pl.BlockSpec((None,) + shape, lambda b, i: (layer,) + (0,) * len(shape))
    out_shapes = (
        jax.ShapeDtypeStruct((bsz, s, CONV_DIM), F32),
        jax.ShapeDtypeStruct((bsz, s, LRU_DIM), BF16),
        jax.ShapeDtypeStruct((bsz, s, N_HEADS * LANES), BF16),
        jax.ShapeDtypeStruct((bsz, s // CMP_STRIDE, CMP_STRIDE * KVC_WIDTH), BF16),
        jax.ShapeDtypeStruct((bsz, N_KV, s, LANES), BF16),
        jax.ShapeDtypeStruct((bsz, N_KV, s, LANES), BF16),
        jax.ShapeDtypeStruct((bsz, n_t, LANES, KEY_TILE), BF16),
        jax.ShapeDtypeStruct((bsz, s // WIN_TILE, LANES, WIN_TILE), BF16),
        jax.ShapeDtypeStruct((bsz, N_KV * GATE_ROWS, s), F32),
    )
    out_specs = (
        pl.BlockSpec((None, tm, CONV_DIM), lambda b, i: (b, i, 0)),
        pl.BlockSpec((None, tm, LRU_DIM), lambda b, i: (b, i, 0)),
        pl.BlockSpec((None, tm, N_HEADS * LANES), lambda b, i: (b, i, 0)),
        pl.BlockSpec((None, tm // CMP_STRIDE, CMP_STRIDE * KVC_WIDTH), lambda b, i: (b, i, 0)),
        pl.BlockSpec((None, N_KV, tm, LANES), lambda b, i: (b, 0, i, 0)),
        pl.BlockSpec((None, N_KV, tm, LANES), lambda b, i: (b, 0, i, 0)),
        pl.BlockSpec((None, None, LANES, KEY_TILE), lambda b, i: (b, i, 0, 0)),
        pl.BlockSpec((None, tm // WIN_TILE, LANES, WIN_TILE), lambda b, i: (b, i, 0, 0)),
        pl.BlockSpec((None, N_KV * GATE_ROWS, tm), lambda b, i: (b, 0, i)),
    )
    return pl.pallas_call(
        _premix_kernel,
        grid=(bsz, n_t),
        in_specs=[pl.BlockSpec((None, tm, d), lambda b, i: (b, i, 0)),
                  full((1, d)), row(1), row(0), full((d, COL_END)),
                  pl.BlockSpec((N_HEADS, LANES), lambda b, i: (0, 0)),
                  full((LRU_CONV_WIDTH, c)), full((1, c)),
                  full((c, c)), full((1, c)), full((c, c)), full((1, c)), full((1, c))],
        out_specs=out_specs,
        out_shape=out_shapes,
        scratch_shapes=[pltpu.VMEM((2, tm, LANES), F32),
                        pltpu.VMEM((BRANCH_TILE + LRU_HALO, c), F32),
                        pltpu.VMEM((8, c), F32)],
        compiler_params=_params(2),
        name="premix",
    )(x, p["norm_mix"], mod, mod, p["w_pack"], qaug,
      p["lru_w_conv"], p["lru_b_conv"], p["lru_wa_bd"], p["lru_b_a"], p["lru_wx_bd"], p["lru_b_x"],
      p["lru_lam"])


def _conv_rows(v, wdw_ref, bdw_ref, lng_ref, lnb_ref, vext_ref, vsh_ref):
    ts, c = v.shape
    vext_ref[CONV_HALO:CONV_HALO + ts, :] = v
    span = ts + CONV_HALO - 8
    for phase in range(1, 8):
        vsh_ref[phase - 1] = vext_ref[pl.ds(phase, span), :]
    acc = jnp.broadcast_to(bdw_ref[...], (ts, c))
    first = CONV_HALO - (CONV_WIDTH - 1)
    for j in range(CONV_WIDTH):
        base, phase = (first + j) // 8 * 8, (first + j) % 8
        rows = vext_ref[base:base + ts, :] if phase == 0 else vsh_ref[phase - 1, base:base + ts, :]
        acc = acc + wdw_ref[j:j + 1, :] * rows
    vext_ref[0:CONV_HALO, :] = v[ts - CONV_HALO:ts, :]
    mu = jnp.mean(acc, axis=-1, keepdims=True)
    cen = acc - mu
    var = jnp.mean(cen * cen, axis=-1, keepdims=True)
    ln = (cen * lax.rsqrt(var + EPS)) * lng_ref[...] + lnb_ref[...]
    return _silu(ln).astype(BF16)


def _lru_rows(ux, gate, wc4_ref, bc4_ref, wa_ref, ba_ref, wx_ref, bx_ref, lam_ref, xext_ref, hcar_ref):
    ts, c = ux.shape
    xext_ref[LRU_HALO:LRU_HALO + ts, :] = ux
    xr = jnp.broadcast_to(bc4_ref[...], (ts, c))
    first = LRU_HALO - (LRU_CONV_WIDTH - 1)
    for j in range(LRU_CONV_WIDTH):
        xr = xr + wc4_ref[j:j + 1, :] * xext_ref[pl.ds(first + j, ts), :]
    xext_ref[0:LRU_HALO, :] = ux[ts - LRU_HALO:ts, :]
    xb = xr.astype(BF16)
    r = _sigmoid(_dot(xb, wa_ref[...]) + ba_ref[...])
    gate_i = _sigmoid(_dot(xb, wx_ref[...]) + bx_ref[...])
    z = -lam_ref[...]
    softplus = jnp.maximum(z, 0.0) + jnp.log(1.0 + jnp.exp(-jnp.abs(z)))
    log_a = (-LRU_C * r) * softplus
    a = jnp.exp(log_a)
    b = jnp.sqrt(1.0 - a * a) * (gate_i * xr)
    rows = lax.broadcasted_iota(jnp.int32, (ts, c), 0)
    shift = 1
    while shift < ts:
        a_prev = pltpu.roll(a, shift, axis=0)
        b_prev = pltpu.roll(b, shift, axis=0)
        live = rows >= shift
        b = jnp.where(live, a * b_prev + b, b)
        a = jnp.where(live, a * a_prev, a)
        shift *= 2
    h = a * hcar_ref[0:1, :] + b
    hcar_ref[...] = jnp.broadcast_to(h[ts - 1:ts, :], hcar_ref.shape)
    return (h * gate).astype(BF16)


def _compress_kernel(x_ref, wlo_ref, whi_ref, pek_ref, pev_ref, w1k_ref, w1v_ref, w2_ref,
                     kc_ref, vct_ref):
    nc = x_ref.shape[0]
    x = x_ref[...]
    h_lo = _dot(x, wlo_ref[...])
    h_hi = _dot(x, whi_ref[...])
    h_next = pltpu.roll(h_hi, nc - 1, axis=0)
    bk = _dot3(pek_ref[...], w1k_ref[...])[0:1, :]
    bv = _dot3(pev_ref[...], w1v_ref[...])[0:1, :]
    bias = jnp.concatenate([bk, bk, bv, bv], axis=1)
    hid = _gelu(h_lo + h_next + bias).astype(BF16)
    out = _dot(hid, w2_ref[...])
    cend = lax.broadcasted_iota(jnp.int32, (nc, LANES), 0) * CMP_STRIDE + (CMP_LEN - 1)
    lane = lax.broadcasted_iota(jnp.int32, (nc, LANES), 1)
    aug = _position_lanes(cend, lane)
    for g in range(N_KV):
        kc_ref[g] = (out[:, g * LANES:(g + 1) * LANES] + aug).astype(BF16)
    vct = out[:, N_KV * LANES:(N_KV + 1) * LANES].T.astype(BF16)
    for t in range(nc // CMP_CHUNK):
        vct_ref[t] = vct[:, t * CMP_CHUNK:(t + 1) * CMP_CHUNK]


def _compress(kvc, layer, p):
    bsz, nc, width = kvc.shape
    x = kvc
    full = lambda shape: pl.BlockSpec((None,) + shape, lambda b: (layer,) + (0,) * len(shape))
    return pl.pallas_call(
        _compress_kernel,
        grid=(bsz,),
        in_specs=[pl.BlockSpec((None, nc, width), lambda b: (b, 0, 0)),
                  full((width, CMP_SLOTS * CMP_HIDDEN)), full((width, CMP_SLOTS * CMP_HIDDEN)),
                  full((8, CMP_LEN * HEAD_DIM)), full((8, CMP_LEN * HEAD_DIM)),
                  full((CMP_LEN * HEAD_DIM, CMP_HIDDEN)), full((CMP_LEN * HEAD_DIM, CMP_HIDDEN)),
                  full((CMP_SLOTS * CMP_HIDDEN, (N_KV + 1) * LANES))],
        out_specs=(pl.BlockSpec((None, N_KV, nc, LANES), lambda b: (b, 0, 0, 0)),
                   pl.BlockSpec((None, nc // CMP_CHUNK, LANES, CMP_CHUNK), lambda b: (b, 0, 0, 0))),
        out_shape=(jax.ShapeDtypeStruct((bsz, N_KV, nc, LANES), BF16),
                   jax.ShapeDtypeStruct((bsz, nc // CMP_CHUNK, LANES, CMP_CHUNK), BF16)),
        compiler_params=_params(1),
        name="compress",
    )(x, p["cmp_w_lo"], p["cmp_w_hi"], p["cmp_pe_k"], p["cmp_pe_v"], p["nsa_w_ck1"], p["nsa_w_cv1"],
      p["cmp_w2"])


def _attn_kernel(q_ref, kc_ref, vct_ref, ks_ref, vst_ref, kw_ref, vwt_ref, gt_ref,
                 o_ref, s0_ref, s1_ref, w_ref, psum_ref, selneg_ref, m_ref, l_ref, acc_ref, flag_ref, list_ref):
    qb = pl.program_id(2)
    q0 = qb * Q_BLOCK
    n_sel = selneg_ref.shape[0]
    n_rows = GROUP * Q_BLOCK
    qg = jnp.concatenate([q_ref[:, r * LANES:(r + 1) * LANES] for r in range(GROUP)], axis=0)
    qpos = q0 + (lax.broadcasted_iota(jnp.int32, (1, n_rows), 1) & (Q_BLOCK - 1))

    n_chunks = ((q0 + Q_BLOCK - CMP_LEN) // CMP_STRIDE + CMP_CHUNK) // CMP_CHUNK
    cend = lax.broadcasted_iota(jnp.int32, (CMP_CHUNK, n_rows), 0) * CMP_STRIDE + (CMP_LEN - 1)

    def cmp_branch(n_ch):
        def fn():
            s_all = _dot_nt(kc_ref[0:n_ch * CMP_CHUNK, :], qg)
            scores = []
            m_c = None
            for ci in range(n_ch):
                s = s_all[ci * CMP_CHUNK:(ci + 1) * CMP_CHUNK, :]
                if ci >= n_ch - 2:
                    s = jnp.where(cend + ci * CMP_CHUNK * CMP_STRIDE <= qpos, s, NEG)
                scores.append(s)
                mi = jnp.max(s, axis=0, keepdims=True)
                m_c = mi if m_c is None else jnp.maximum(m_c, mi)
            m_c = jnp.where(m_c > 0.5 * NEG, m_c, 0.0)
            probs = [jnp.exp2(s - m_c) for s in scores]
            den = probs[0].sum(axis=0, keepdims=True)
            for p in probs[1:]:
                den = den + jnp.sum(p, axis=0, keepdims=True)
            inv_den = 1.0 / jnp.where(den > 0, den, 1.0)
            o_c = None
            for ci, p in enumerate(probs):
                p = p * inv_den
                part = _dot(vct_ref[ci], p.astype(BF16))
                o_c = part if o_c is None else o_c + part
                p_sum = p[:, 0:Q_BLOCK]
                for r in range(1, GROUP):
                    p_sum = p_sum + p[:, r * Q_BLOCK:(r + 1) * Q_BLOCK]
                for h in range(Q_BLOCK // LANES):
                    psum_ref[h, CMP_PAD + ci * CMP_CHUNK:CMP_PAD + (ci + 1) * CMP_CHUNK, :] = (
                        p_sum[:, h * LANES:(h + 1) * LANES])
            assert CMP_LEN == 2 * CMP_STRIDE and CMP_PER_SEL == 4
            n_blk = n_ch * CMP_CHUNK // CMP_PER_SEL
            halves = []
            for h in range(Q_BLOCK // LANES):
                taps = [psum_ref[h, pl.ds(CMP_PAD - 1 + k, n_blk, stride=CMP_PER_SEL), :]
                        for k in range(CMP_PER_SEL + 1)]
                imp_h = 0.5 * (taps[0] + taps[4]) + (taps[1] + taps[2] + taps[3])
                if n_blk < n_sel:
                    last = CMP_PAD + n_ch * CMP_CHUNK - 1
                    imp_h = jnp.concatenate([imp_h, 0.5 * psum_ref[h, last:last + 1, :],
                                             jnp.zeros((n_sel - n_blk - 1, LANES), F32)], axis=0)
                halves.append(imp_h)
            return o_c, jnp.concatenate(halves, axis=1)
        return fn

    for h in range(Q_BLOCK // LANES):
        psum_ref[h, 0:CMP_PAD, :] = jnp.zeros((CMP_PAD, LANES), F32)
    o_c, imp = lax.switch(n_chunks - 1, [cmp_branch(n) for n in range(1, kc_ref.shape[0] // CMP_CHUNK + 1)])

    blk = lax.broadcasted_iota(jnp.int32, (n_sel, LANES), 0)
    blk_f = blk.astype(F32)
    works = []
    for h in range(Q_BLOCK // LANES):
        qp = q0 + h * LANES + lax.broadcasted_iota(jnp.int32, (n_sel, LANES), 1)
        qid = qp // SEL_BLOCK
        forced = (blk == 0) | (blk == qid) | (blk == qid - 1)
        valid = blk * SEL_BLOCK <= qp
        works.append(jnp.where(forced, -jnp.inf, jnp.where(valid, imp[:, h * LANES:(h + 1) * LANES], -1.0)))
    for _ in range(min(SEL_TOPK, n_sel) - 3):
        for h, work_h in enumerate(works):
            best = jnp.max(work_h, axis=0, keepdims=True)
            first = jnp.min(jnp.where(work_h == best, blk_f, float(n_sel)), axis=0, keepdims=True)
            works[h] = jnp.where(blk_f == first, -jnp.inf, work_h)
    work = jnp.concatenate(works, axis=1)

    n_tiles = q0 // KEY_TILE + 1
    krow = lax.broadcasted_iota(jnp.int32, (SEL_BLOCK, n_rows), 0)

    def scores_into(buf_ref, kt):
        k0 = pl.multiple_of(kt * KEY_TILE, KEY_TILE)
        buf_ref[...] = _dot_nt(ks_ref[pl.ds(k0, KEY_TILE), :], qg)

    k_last = pl.multiple_of((n_tiles - 1) * KEY_TILE, KEY_TILE)
    s_both = _dot_nt(jnp.concatenate([ks_ref[pl.ds(k_last, KEY_TILE), :], ks_ref[0:LANES, :]], axis=0), qg)
    s0_ref[...] = s_both[0:KEY_TILE, :]
    hrow = lax.broadcasted_iota(jnp.int32, (LANES, n_rows), 0)
    s_first = jnp.where((hrow < SEL_BLOCK) & (hrow <= qpos), s_both[KEY_TILE:KEY_TILE + LANES, :], NEG)
    m_first = jnp.max(s_first, axis=0, keepdims=True)
    p_first = jnp.exp2(s_first - m_first)
    m_ref[...] = m_first
    l_ref[...] = jnp.sum(p_first, axis=0, keepdims=True)
    acc_ref[...] = _dot(vst_ref[0, :, 0:LANES], p_first.astype(BF16))

    n_win = (WINDOW + Q_BLOCK) // WIN_TILE
    t0 = jnp.maximum(q0 - WINDOW, 0) // WIN_TILE
    kstart = pl.multiple_of(t0 * WIN_TILE, WIN_TILE)
    w_ref[...] = _dot_nt(kw_ref[pl.ds(kstart, n_win * WIN_TILE), :], qg)
    wrow = lax.broadcasted_iota(jnp.int32, (WIN_TILE, n_rows), 0)
    qrel = qpos - kstart

    top = None
    for t in range(n_win):
        newest = qrel - t * WIN_TILE
        visible = wrow <= newest
        if t * WIN_TILE < Q_BLOCK:
            visible = visible & (wrow > newest - WINDOW)
        st = jnp.where(visible, w_ref[t * WIN_TILE:(t + 1) * WIN_TILE, :], NEG)
        w_ref[t * WIN_TILE:(t + 1) * WIN_TILE, :] = st
        mt = _fold8(st, jnp.maximum)
        top = mt if top is None else jnp.maximum(top, mt)
    m_w = jnp.max(top, axis=0, keepdims=True)
    total = None
    o_w = None
    for t in range(n_win):
        pt = jnp.exp2(w_ref[t * WIN_TILE:(t + 1) * WIN_TILE, :] - m_w)
        lt = _fold8(pt, jnp.add)
        total = lt if total is None else total + lt
        part = _dot(vwt_ref[t0 + t], pt.astype(BF16))
        o_w = part if o_w is None else o_w + part
    o_w = o_w / jnp.sum(total, axis=0, keepdims=True)

    taken = (work == -jnp.inf) & (lax.broadcasted_iota(jnp.int32, (n_sel, Q_BLOCK), 0) > 0)
    chosen = jnp.where(taken, 1.0, 0.0)
    selneg = jnp.where(taken, 0.0, NEG)
    selneg_ref[...] = jnp.concatenate([selneg] * GROUP, axis=1)
    blocks_per_tile = KEY_TILE // SEL_BLOCK
    any_q = jnp.max(chosen, axis=1, keepdims=True)
    for t in range(n_sel // blocks_per_tile):
        tile_any = jnp.max(any_q[t * blocks_per_tile:(t + 1) * blocks_per_tile, :])
        flag_ref[t] = (tile_any > 0).astype(jnp.int32)

    def compact(kt, n):
        @pl.when(flag_ref[kt] > 0)
        def _():
            list_ref[n] = kt
        return n + flag_ref[kt]

    n_act = lax.fori_loop(0, n_tiles - 1, compact, 0)
    list_ref[n_act] = 0

    def sel_tile(buf_ref, kt, causal, n_blocks=KEY_TILE // SEL_BLOCK):
        sel_rows = selneg_ref[pl.ds(pl.multiple_of(kt * blocks_per_tile, blocks_per_tile),
                                    blocks_per_tile), :]

        def block_scores(j):
            return buf_ref[j * SEL_BLOCK:(j + 1) * SEL_BLOCK, :]

        m_old = m_ref[...]
        top = None
        for j in range(n_blocks):
            sj = block_scores(j)
            if causal:
                sj = jnp.where(krow <= qpos - (kt * KEY_TILE + j * SEL_BLOCK), sj, NEG)
                buf_ref[j * SEL_BLOCK:(j + 1) * SEL_BLOCK, :] = sj
            mj = _fold8(sj, jnp.maximum) + sel_rows[j:j + 1, :]
            top = mj if top is None else jnp.maximum(top, mj)
        m_new = jnp.maximum(m_old, jnp.max(top, axis=0, keepdims=True))
        alpha = jnp.exp2(m_old - m_new)
        total = None
        probs = []
        for j in range(n_blocks):
            pj = jnp.exp2(block_scores(j) - (m_new - sel_rows[j:j + 1, :]))
            lj = _fold8(pj, jnp.add)
            total = lj if total is None else total + lj
            probs.append(pj.astype(BF16))
        m_ref[...] = m_new
        l_ref[...] = alpha * l_ref[...] + jnp.sum(total, axis=0, keepdims=True)
        acc_ref[...] = alpha * acc_ref[...] + _dot(vst_ref[kt, :, 0:n_blocks * SEL_BLOCK],
                                                   jnp.concatenate(probs, axis=0))

    half_tile = (q0 % KEY_TILE) + Q_BLOCK <= KEY_TILE // 2

    @pl.when(half_tile)
    def _():
        scores_into(s1_ref, list_ref[0])
        sel_tile(s0_ref, n_tiles - 1, True, n_blocks=KEY_TILE // SEL_BLOCK // 2)

    @pl.when(jnp.logical_not(half_tile))
    def _():
        scores_into(s1_ref, list_ref[0])
        sel_tile(s0_ref, n_tiles - 1, True)

    def pipelined(i, carry):
        scores_into(s0_ref, list_ref[2 * i + 1])
        sel_tile(s1_ref, list_ref[2 * i], False)

        @pl.when(2 * i + 1 < n_act)
        def _():
            scores_into(s1_ref, list_ref[2 * i + 2])
            sel_tile(s0_ref, list_ref[2 * i + 1], False)
        return carry

    lax.fori_loop(0, (n_act + 1) // 2, pipelined, 0)
    o_s = acc_ref[...] / l_ref[...]

    def gate(branch):
        return jnp.concatenate([gt_ref[branch * GROUP + r:branch * GROUP + r + 1, :] for r in range(GROUP)],
                               axis=1)
    o = gate(0) * o_c + gate(1) * o_s + gate(2) * o_w
    o_ref[...] = jnp.concatenate([o[:, r * Q_BLOCK:(r + 1) * Q_BLOCK].T for r in range(GROUP)],
                                 axis=1).astype(BF16)


def _attention(q, kc, vct, ks, vst, kw, vwt, gt):
    bsz, s, _ = q.shape
    nc = kc.shape[2]
    n_sel = s // SEL_BLOCK
    n_qb = s // Q_BLOCK
    n_kt = s // KEY_TILE
    n_wt = s // WIN_TILE
    n_ch = nc // CMP_CHUNK
    n_rows = GROUP * Q_BLOCK
    assert s % KEY_TILE == 0 and s >= (WINDOW + Q_BLOCK) and n_sel % 8 == 0 and nc % CMP_CHUNK == 0
    assert KEY_TILE % Q_BLOCK == 0 and Q_BLOCK % WIN_TILE == 0
    return pl.pallas_call(
        _attn_kernel,
        grid=(bsz, N_KV, n_qb),
        in_specs=[pl.BlockSpec((None, Q_BLOCK, GROUP * LANES), lambda b, g, i: (b, i, g)),
                  pl.BlockSpec((None, None, nc, LANES), lambda b, g, i: (b, g, 0, 0)),
                  pl.BlockSpec((None, n_ch, HEAD_DIM, CMP_CHUNK), lambda b, g, i: (b, 0, g, 0)),
                  pl.BlockSpec((None, None, s, LANES), lambda b, g, i: (b, g, 0, 0)),
                  pl.BlockSpec((None, n_kt, HEAD_DIM, KEY_TILE), lambda b, g, i: (b, 0, g, 0)),
                  pl.BlockSpec((None, None, s, LANES), lambda b, g, i: (b, g, 0, 0)),
                  pl.BlockSpec((None, n_wt, HEAD_DIM, WIN_TILE), lambda b, g, i: (b, 0, g, 0)),
                  pl.BlockSpec((None, GATE_ROWS, Q_BLOCK), lambda b, g, i: (b, g, i))],
        out_specs=pl.BlockSpec((None, Q_BLOCK, GROUP * HEAD_DIM), lambda b, g, i: (b, i, g)),
        out_shape=jax.ShapeDtypeStruct((bsz, s, N_HEADS * HEAD_DIM), BF16),
        scratch_shapes=[pltpu.VMEM((KEY_TILE, n_rows), F32),
                        pltpu.VMEM((KEY_TILE, n_rows), F32),
                        pltpu.VMEM((WINDOW + Q_BLOCK, n_rows), F32),
                        pltpu.VMEM((Q_BLOCK // LANES, CMP_PAD + nc, LANES), F32),
                        pltpu.VMEM((n_sel, n_rows), F32),
                        pltpu.VMEM((1, n_rows), F32),
                        pltpu.VMEM((1, n_rows), F32),
                        pltpu.VMEM((HEAD_DIM, n_rows), F32),
                        pltpu.SMEM((n_kt,), jnp.int32),
                        pltpu.SMEM((n_kt + 1,), jnp.int32)],
        compiler_params=_params(3),
        name="nsa_attention",
    )(q, kc, vct, ks, vst, kw, vwt, gt)


def _merge_kernel(x_ref, g_ref, sc_ref, sh_ref, gate_ref, uconv_ref, ylru_ref, onsa_ref,
                  wdw_ref, bdw_ref, lng_ref, lnb_ref, wm_ref, bm_ref, wc_ref, wn_ref, wl_ref, wo_ref,
                  o_ref, vext_ref, vsh_ref):
    tm, d = x_ref.shape

    @pl.when(pl.program_id(1) == 0)
    def _():
        vext_ref[0:CONV_HALO, :] = jnp.zeros((CONV_HALO, CONV_DIM), F32)

    x = x_ref[...]
    hb = _mod_norm(x, g_ref[...], sc_ref[...], sh_ref[...]).astype(BF16)

    def merge_gate(k):
        return _sigmoid(_dot(hb, wm_ref[:, k * d:(k + 1) * d]) + bm_ref[:, k * d:(k + 1) * d])

    early = merge_gate(1) * _dot(onsa_ref[...], wn_ref[...]) + merge_gate(2) * _dot(ylru_ref[...], wl_ref[...])
    vconv = jnp.concatenate(
        [_conv_rows(uconv_ref[r0:r0 + BRANCH_TILE, :], wdw_ref, bdw_ref, lng_ref, lnb_ref, vext_ref, vsh_ref)
         for r0 in range(0, tm, BRANCH_TILE)], axis=0)
    merged = early + merge_gate(0) * _dot(vconv, wc_ref[...])
    o_ref[...] = x + gate_ref[...] * _dot(merged.astype(BF16), wo_ref[...])


def _merge(x, mod, layer, uconv, ylru, onsa, p):
    bsz, s, d = x.shape
    tm = TOKEN_TILE
    ts = BRANCH_TILE
    c = CONV_DIM
    assert tm % ts == 0
    row = lambda k: pl.BlockSpec((None, None, None, 1, d), lambda b, i: (layer, b, k, 0, 0))
    full = lambda shape: pl.BlockSpec((None,) + shape, lambda b, i: (layer,) + (0,) * len(shape))
    tok = lambda w: pl.BlockSpec((None, tm, w), lambda b, i: (b, i, 0))
    return pl.pallas_call(
        _merge_kernel,
        grid=(bsz, s // tm),
        in_specs=[tok(d), full((1, d)), row(1), row(0), row(2), tok(c), tok(c), tok(c),
                  full((CONV_WIDTH, c)), full((1, c)), full((1, c)), full((1, c)),
                  full((d, N_BRANCH * d)), full((1, N_BRANCH * d)),
                  full((c, d)), full((c, d)), full((c, d)), full((d, d))],
        out_specs=tok(d),
        out_shape=jax.ShapeDtypeStruct((bsz, s, d), F32),
        scratch_shapes=[pltpu.VMEM((ts + CONV_HALO, c), F32),
                        pltpu.VMEM((7, ts + CONV_HALO - 8, c), F32)],
        compiler_params=_params(2),
        name="merge",
    )(x, p["norm_mix"], mod, mod, mod, uconv, ylru, onsa,
      p["conv_w_dw"], p["conv_b_dw"], p["conv_ln_g"], p["conv_ln_b"],
      p["w_merge"], p["b_merge"], p["conv_w_out"], p["nsa_w_out"], p["lru_w_out"], p["w_out"])


def _ffn_kernel(x_ref, g_ref, sc_ref, sh_ref, gate_ref, wa_ref, wb_ref, wo_ref, fin_ref, o_ref,
                *, chunk, final):
    x = x_ref[...]
    hb = _mod_norm(x, g_ref[...], sc_ref[...], sh_ref[...]).astype(BF16)
    ffn = wa_ref.shape[1]
    acc = None
    for c0 in range(0, ffn, chunk):
        a = _dot(hb, wa_ref[:, c0:c0 + chunk])
        b = _dot(hb, wb_ref[:, c0:c0 + chunk])
        part = _dot((_silu(a) * b).astype(BF16), wo_ref[c0:c0 + chunk, :])
        acc = part if acc is None else acc + part
    y = x + gate_ref[...] * acc
    if final:
        y = (y * lax.rsqrt(jnp.mean(y * y, axis=-1, keepdims=True) + EPS)) * fin_ref[...]
    o_ref[...] = y


def _ffn(x, mod, layer, norm_g, p, final_norm, final):
    bsz, s, d = x.shape
    tm = TOKEN_TILE
    ffn = p["w_ffn_out"].shape[1]
    chunk = FFN_CHUNK
    assert ffn % chunk == 0
    row = lambda k: pl.BlockSpec((None, None, None, 1, d), lambda b, i: (layer, b, k, 0, 0))
    full = lambda shape: pl.BlockSpec((None,) + shape, lambda b, i: (layer,) + (0,) * len(shape))
    half = lambda k: pl.BlockSpec((None, d, ffn), lambda b, i: (layer, 0, k))
    tok = pl.BlockSpec((None, tm, d), lambda b, i: (b, i, 0))
    return pl.pallas_call(
        functools.partial(_ffn_kernel, chunk=chunk, final=final),
        grid=(bsz, s // tm),
        in_specs=[tok, full((1, d)), row(4), row(3), row(5),
                  half(0), half(1), full((ffn, d)), pl.BlockSpec((1, d), lambda b, i: (0, 0))],
        out_specs=tok,
        out_shape=jax.ShapeDtypeStruct((bsz, s, d), F32),
        compiler_params=_params(2),
        name="ffn",
    )(x, norm_g, mod, mod, mod, p["w_ffn_in"], p["w_ffn_in"], p["w_ffn_out"], final_norm)


def _pack_input_projection(w_in):
    lead = w_in.shape[:-1]
    q0 = 2 * CONV_DIM
    kv0 = q0 + N_HEADS * HEAD_DIM
    gate0 = kv0 + 6 * N_KV * HEAD_DIM
    lru0 = gate0 + 3 * N_HEADS
    kv_w = N_KV * HEAD_DIM
    keep = ((0, 0),) * len(lead)

    gap = jnp.zeros(lead + (LANES - HEAD_DIM,), BF16)

    def head_slots(w, n):
        pieces = []
        for h in range(n):
            pieces += [w[..., h * HEAD_DIM:(h + 1) * HEAD_DIM].astype(BF16), gap]
        return pieces

    kv = lambda i: w_in[..., kv0 + i * kv_w:kv0 + (i + 1) * kv_w]
    gates = w_in[..., gate0:gate0 + N_KV * GROUP * 3].reshape(lead + (N_KV, GROUP, 3))
    gates = jnp.swapaxes(gates, -1, -2).reshape(lead + (N_KV, 3 * GROUP))
    gates = jnp.pad(gates, keep + ((0, 0), (0, GATE_ROWS - 3 * GROUP))).reshape(lead + (N_KV * GATE_ROWS,))
    gates = jnp.pad(gates, keep + ((0, LANES - N_KV * GATE_ROWS),))
    cast = lambda w: w.astype(BF16)
    return jnp.concatenate(
        [cast(w_in[..., 0:q0]), cast(w_in[..., q0:kv0] * (HEAD_DIM ** -0.5 * LOG2E)), cast(kv(0)), cast(kv(1))]
        + head_slots(kv(2), N_KV) + head_slots(kv(4), N_KV)
        + [cast(kv(3)), cast(kv(5)), cast(gates), cast(w_in[..., lru0:lru0 + 2 * LRU_DIM])], axis=-1)


def _bf16_pieces(x):
    pieces = []
    for _ in range(SLOPE_PIECES):
        piece = float(np.asarray(x, np.float32).astype(BF16).astype(np.float64))
        pieces.append(piece)
        x = x - piece
    return pieces


def _q_slope_rows():
    rows = np.zeros((N_HEADS, LANES), np.float32)
    for h in range(N_HEADS):
        slope = 2.0 ** (-8.0 * (h + 1) / N_HEADS) * LOG2E
        rows[h, HEAD_DIM:HEAD_DIM + SLOPE_PIECES] = _bf16_pieces(slope * float(1 << POS_BITS))
        rows[h, HEAD_DIM + SLOPE_PIECES:HEAD_DIM + 2 * SLOPE_PIECES] = _bf16_pieces(slope)
    return rows


def _block_diag(w):
    depth, heads, n, _ = w.shape
    eye = jnp.asarray(np.eye(heads, dtype=np.float32))
    return (w[:, :, :, None, :] * eye[None, :, None, :, None]).reshape(depth, heads * n, heads * n)


def _prepare_params(a):
    depth = a["w_in"].shape[0]
    w_pack = _pack_input_projection(a["w_in"])
    assert w_pack.shape[-1] == COL_END
    row = lambda v: v.reshape(depth, 1, -1)

    def chunk_weights(half):
        rows = slice(half * CMP_STRIDE, (half + 1) * CMP_STRIDE)
        wk = a["nsa_w_ck1"].reshape(depth, CMP_LEN, HEAD_DIM, CMP_HIDDEN)[:, rows].astype(BF16)
        wv = a["nsa_w_cv1"].reshape(depth, CMP_LEN, HEAD_DIM, CMP_HIDDEN)[:, rows].astype(BF16)
        slots = [jnp.pad(w, ((0, 0), (0, 0), (0, 0), (k * CMP_HIDDEN, (CMP_SLOTS - 1 - k) * CMP_HIDDEN)))
                 for k, w in enumerate((wk, wk, wv, wv))]
        return jnp.concatenate(slots, axis=2).reshape(depth, CMP_STRIDE * KVC_WIDTH, CMP_SLOTS * CMP_HIDDEN)

    eye2 = jnp.asarray(np.eye(N_KV, dtype=np.float32))
    ck2 = jnp.pad(a["nsa_w_ck2"], ((0, 0), (0, 0), (0, LANES - HEAD_DIM)))
    block2 = lambda w: (w[:, None, :, None, :] * eye2[None, :, None, :, None]).reshape(
        depth, N_KV * w.shape[1], N_KV * w.shape[2])
    w2 = jnp.concatenate([jnp.pad(block2(ck2), ((0, 0), (0, 0), (0, N_KV * HEAD_DIM))),
                          jnp.pad(block2(a["nsa_w_cv2"]), ((0, 0), (0, 0), (N_KV * LANES, 0)))], axis=1)
    pe_rows = lambda pe: jnp.pad(pe.reshape(depth, 1, -1), ((0, 0), (0, 7), (0, 0)))
    return {
        "w_pack": w_pack.astype(BF16),
        "norm_mix": row(a["norm_mix"]), "norm_ffn": row(a["norm_ffn"]),
        "conv_w_dw": a["conv_w_dw"], "conv_b_dw": row(a["conv_b_dw"]),
        "conv_ln_g": row(a["conv_ln_g"]), "conv_ln_b": row(a["conv_ln_b"]),
        "lru_w_conv": a["lru_w_conv"], "lru_b_conv": row(a["lru_b_conv"]),
        "lru_wa_bd": _block_diag(a["lru_w_a"]).astype(BF16), "lru_b_a": row(a["lru_b_a"]),
        "lru_wx_bd": _block_diag(a["lru_w_x"]).astype(BF16), "lru_b_x": row(a["lru_b_x"]),
        "lru_lam": row(a["lru_lam"]),
        "cmp_w_lo": chunk_weights(0), "cmp_w_hi": chunk_weights(1),
        "cmp_pe_k": pe_rows(a["nsa_pe_k"]), "cmp_pe_v": pe_rows(a["nsa_pe_v"]),
        "nsa_w_ck1": a["nsa_w_ck1"], "nsa_w_cv1": a["nsa_w_cv1"],
        "cmp_w2": w2.astype(BF16),
        "w_merge": a["w_merge"].astype(BF16), "b_merge": row(a["b_merge"]),
        "conv_w_out": a["conv_w_out"].astype(BF16), "nsa_w_out": a["nsa_w_out"].astype(BF16),
        "lru_w_out": a["lru_w_out"].astype(BF16), "w_out": a["w_out"].astype(BF16),
        "w_ffn_in": a["w_ffn_in"].astype(BF16), "w_ffn_out": a["w_ffn_out"].astype(BF16),
    }


def kernel(x, c, w_mod, b_mod, norm_mix, norm_ffn, w_in, conv_w_dw, conv_b_dw, conv_ln_g, conv_ln_b,
           conv_w_out, nsa_pe_k, nsa_w_ck1, nsa_w_ck2, nsa_pe_v, nsa_w_cv1, nsa_w_cv2, nsa_w_out,
           lru_w_conv, lru_b_conv, lru_w_a, lru_b_a, lru_w_x, lru_b_x, lru_lam, lru_w_out,
           w_merge, b_merge, w_out, w_ffn_in, w_ffn_out, final_norm):
    a = dict(norm_mix=norm_mix, norm_ffn=norm_ffn, w_in=w_in, conv_w_dw=conv_w_dw, conv_b_dw=conv_b_dw, conv_ln_g=conv_ln_g, conv_ln_b=conv_ln_b,
             conv_w_out=conv_w_out, nsa_pe_k=nsa_pe_k, nsa_w_ck1=nsa_w_ck1, nsa_w_ck2=nsa_w_ck2,
             nsa_pe_v=nsa_pe_v, nsa_w_cv1=nsa_w_cv1, nsa_w_cv2=nsa_w_cv2, nsa_w_out=nsa_w_out,
             lru_w_conv=lru_w_conv, lru_b_conv=lru_b_conv, lru_w_a=lru_w_a, lru_b_a=lru_b_a,
             lru_w_x=lru_w_x, lru_b_x=lru_b_x, lru_lam=lru_lam, lru_w_out=lru_w_out,
             w_merge=w_merge, b_merge=b_merge, w_out=w_out, w_ffn_in=w_ffn_in, w_ffn_out=w_ffn_out)
    depth = w_in.shape[0]
    s = x.shape[1]
    mod = _modulation(c, w_mod, b_mod)
    qaug = jnp.asarray(_q_slope_rows())
    fin = final_norm.reshape(1, -1)
    p = _prepare_params(a)
    for l in range(depth):
        uconv, ylru, q, kvc, ks, kw, vst, vwt, gt = _premix(x, mod, l, qaug, p)
        kc, vct = _compress(kvc, l, p)
        onsa = _attention(q, kc, vct, ks, vst, kw, vwt, gt)
        x = _merge(x, mod, l, uconv, ylru, onsa, p)
        x = _ffn(x, mod, l, p["norm_ffn"], p, fin, final=(l == depth - 1))
    return x
```
